```python
import jax, jax.numpy as jnp
from jax import lax
import numpy as np

D_MODEL = 1024
BATCH = 2
SEQ = 8192
DEPTH = 1

GRID_W = 64
CTX_LEN = 256
CHUNK = 128
GMLP_HEADS = 4
GMLP_DH = 128
GMLP_W = GMLP_HEADS * GMLP_DH
MLSTM_HEADS = 4
MLSTM_DH = 128
MLSTM_W = MLSTM_HEADS * MLSTM_DH
MIX_W = GMLP_W + MLSTM_W
N_DIR = 2
N_GATE = 2 * N_DIR * MLSTM_HEADS
N_IN = 2 * GMLP_W + 4 * MLSTM_W + N_GATE
NORM_HEAD = 128
D_FF = 2688
CONV_W = 3
ALPHA = (2 * DEPTH) ** 0.25
BETA = (8 * DEPTH) ** -0.25
EPS = 1e-5

kernel_name = "hymba_gmlp_mlstm_convffn_deepnorm_prefix"


def _normalize(xf):
    mu = jnp.mean(xf, -1, keepdims=True)
    var = jnp.mean(jnp.square(xf - mu), -1, keepdims=True)
    return (xf - mu) * lax.rsqrt(var + EPS)


def layer_norm(x, g, b):
    return (_normalize(x.astype(jnp.float32)) * g + b).astype(x.dtype)


def head_norm(y, g):
    B, L, W = y.shape
    yf = _normalize(y.astype(jnp.float32).reshape(B, L, W // NORM_HEAD, NORM_HEAD))
    return (yf.reshape(B, L, W) * g).astype(y.dtype)


def dwconv3(x, w, b, axis):
    L = x.shape[axis]
    pad = [(0, 0)] * x.ndim
    pad[axis] = (1, 1)
    xp = jnp.pad(x, pad)
    sl = lambda s: lax.slice_in_dim(xp, s, s + L, axis=axis)
    return sl(0) * w[0] + sl(1) * w[1] + sl(2) * w[2] + b


def gmlp_mix(u, v, ln_g, w_s, b_s):
    B, L, _ = u.shape
    u = jax.nn.gelu(u)
    vh = jax.nn.gelu(v).reshape(B, L // CHUNK, CHUNK, GMLP_HEADS, GMLP_DH)
    vh = (_normalize(vh.astype(jnp.float32)) * ln_g).astype(u.dtype)
    mixed = jnp.einsum('gts,bnsgc->bntgc', w_s, vh) + b_s.T[None, None, :, :, None]
    return u * mixed.reshape(B, L, GMLP_W)


def mlstm_chunkwise(q, k, v, log_i, log_f, state):
    B, H, L, Dh = q.shape
    nc = L // CHUNK
    to_chunks = lambda a: jnp.moveaxis(a.reshape(B, H, nc, CHUNK, *a.shape[3:]), 2, 0)
    tril = jnp.tril(jnp.ones((CHUNK, CHUNK), bool))

    def step(carry, inp):
        C, n, m = carry
        qc, kc, vc, li, lf = inp
        b = jnp.cumsum(lf, axis=-1)
        dmat = b[..., :, None] - b[..., None, :] + li[..., None, :]
        dmat = jnp.where(tril, dmat, -jnp.inf)
        inter = b + m[..., None]
        m_t = jnp.maximum(inter, dmat.max(-1))
        w_intra = jnp.exp(dmat - m_t[..., None])
        w_state = jnp.exp(inter - m_t)
        s = jnp.einsum('bhtd,bhsd->bhts', qc, kc) * w_intra
        num = jnp.einsum('bhts,bhse->bhte', s, vc) + w_state[..., None] * jnp.einsum('bhtd,bhde->bhte', qc, C)
        den = s.sum(-1) + w_state * jnp.einsum('bhtd,bhd->bht', qc, n)
        h = num / jnp.maximum(jnp.abs(den), jnp.exp(-m_t))[..., None]
        b_last = b[..., -1]
        g = b_last[..., None] - b + li
        m_new = jnp.maximum(b_last + m, g.max(-1))
        w_old = jnp.exp(b_last + m - m_new)
        w_k = jnp.exp(g - m_new[..., None])
        C = w_old[..., None, None] * C + jnp.einsum('bhs,bhsd,bhse->bhde', w_k, kc, vc)
        n = w_old[..., None] * n + jnp.einsum('bhs,bhsd->bhd', w_k, kc)
        return (C, n, m_new), h

    state, h = lax.scan(step, state, (to_chunks(q), to_chunks(k), to_chunks(v), to_chunks(log_i), to_chunks(log_f)))
    return jnp.moveaxis(h, 0, 2).reshape(B, H, L, Dh), state


def zero_state(batch):
    H2 = N_DIR * MLSTM_HEADS
    return (jnp.zeros((batch, H2, MLSTM_DH, MLSTM_DH), jnp.float32),
            jnp.zeros((batch, H2, MLSTM_DH), jnp.float32),
            jnp.zeros((batch, H2), jnp.float32))


def mlstm_branch(pm, conv_w, conv_b, b_i, b_f, state):
    B, L, _ = pm.shape
    qk = jax.nn.silu(dwconv3(pm[..., :2 * MLSTM_W], conv_w, conv_b, axis=1))
    q, k = qk[..., :MLSTM_W], qk[..., MLSTM_W:]
    vm = pm[..., 2 * MLSTM_W:3 * MLSTM_W]
    o_pre = pm[..., 3 * MLSTM_W:4 * MLSTM_W]
    gates = pm[..., 4 * MLSTM_W:].astype(jnp.float32).reshape(B, L, 2, N_DIR, MLSTM_HEADS)
    log_i = gates[:, :, 0] + b_i
    log_f = jax.nn.log_sigmoid(gates[:, :, 1] + b_f)

    def heads(a):
        return a.astype(jnp.float32).reshape(B, L, MLSTM_HEADS, MLSTM_DH).transpose(0, 2, 1, 3)

    def both_dirs(a):
        return jnp.concatenate([a, jnp.flip(a, 2)], 1)

    def dir_gates(g):
        g = jnp.moveaxis(g, 1, -1)
        return jnp.concatenate([g[:, 0], jnp.flip(g[:, 1], -1)], 1)

    h, state = mlstm_chunkwise(both_dirs(heads(q)), both_dirs(heads(k)) * (MLSTM_DH ** -0.5),
                               both_dirs(heads(vm)), dir_gates(log_i), dir_gates(log_f), state)
    h = h[:, :MLSTM_HEADS] + jnp.flip(h[:, MLSTM_HEADS:], 2)
    h = h.transpose(0, 2, 1, 3).reshape(B, L, MLSTM_W).astype(pm.dtype)
    return h, o_pre, state


def mixer_out(y_a, h_b, o_pre, norm_g, w_out):
    y = head_norm(jnp.concatenate([y_a, h_b], -1), norm_g)
    y = jnp.concatenate([y[..., :GMLP_W], y[..., GMLP_W:] * jax.nn.sigmoid(o_pre)], -1)
    return jnp.einsum('blw,wd->bld', y, w_out)


def conv_ffn(h, w_up, conv_w, conv_b, w_down, rows):
    a = jnp.einsum('bld,df->blf', h, w_up)
    B, L, F = a.shape
    if rows is None:
        a = dwconv3(a, conv_w, conv_b, axis=1)
    else:
        a = dwconv3(a.reshape(B, rows, GRID_W, F), conv_w, conv_b, axis=2).reshape(B, L, F)
    val, gate = a[..., :D_FF], a[..., D_FF:]
    return jnp.einsum('blf,fd->bld', jax.nn.silu(gate) * val, w_down)


def setup_inputs(seed: int = 0) -> dict:
    key = jax.random.key(seed)
    ks = iter(jax.random.split(key, 32))
    nrm = lambda shape, s: jax.random.normal(next(ks), shape, jnp.float32) * s
    ones_n = lambda shape: 1.0 + nrm(shape, 0.01)
    L = DEPTH
    return {
        "x": nrm((BATCH, SEQ, D_MODEL), 1.0),
        "c": nrm((BATCH, D_MODEL), 1.0),
        "ctx": nrm((BATCH, CTX_LEN, D_MODEL), 1.0),
        "c_ctx": nrm((D_MODEL,), 1.0),
        "w_ada": nrm((L, D_MODEL, 6 * D_MODEL), D_MODEL ** -0.5),
        "b_ada": nrm((L, 6 * D_MODEL), 0.01),
        "w_in": nrm((L, D_MODEL, N_IN), D_MODEL ** -0.5),
        "gmlp_ln_g": ones_n((L, GMLP_HEADS, GMLP_DH)),
        "gmlp_ws": nrm((L, GMLP_HEADS, CHUNK, CHUNK), CHUNK ** -0.5),
        "gmlp_bs": ones_n((L, GMLP_HEADS, CHUNK)),
        "qk_conv_w": nrm((L, CONV_W, 2 * MLSTM_W), CONV_W ** -0.5),
        "qk_conv_b": nrm((L, 2 * MLSTM_W), 0.01),
        "b_igate": nrm((L, N_DIR, MLSTM_HEADS), 0.1),
        "b_fgate": 3.0 + 3.0 * jax.random.uniform(next(ks), (L, N_DIR, MLSTM_HEADS), jnp.float32),
        "mix_norm_g": ones_n((L, MIX_W)),
        "w_out": nrm((L, MIX_W, D_MODEL), BETA * MIX_W ** -0.5),
        "ln1_g": ones_n((L, D_MODEL)),
        "ln1_b": nrm((L, D_MODEL), 0.01),
        "w_up": nrm((L, D_MODEL, 2 * D_FF), D_MODEL ** -0.5),
        "ffn_conv_w": nrm((L, CONV_W, 2 * D_FF), CONV_W ** -0.5),
        "ffn_conv_b": nrm((L, 2 * D_FF), 0.01),
        "w_down": nrm((L, D_FF, D_MODEL), BETA * D_FF ** -0.5),
        "ln2_g": ones_n((L, D_MODEL)),
        "ln2_b": nrm((L, D_MODEL), 0.01),
    }


def reference(x, c, ctx, c_ctx, w_ada, b_ada, w_in, gmlp_ln_g, gmlp_ws, gmlp_bs, qk_conv_w, qk_conv_b,
              b_igate, b_fgate, mix_norm_g, w_out, ln1_g, ln1_b, w_up, ffn_conv_w, ffn_conv_b, w_down,
              ln2_g, ln2_b):
    B, S, _ = x.shape
    rows = S // GRID_W
    for l in range(DEPTH):
        last = l == DEPTH - 1
        mod_x = jnp.einsum('bd,de->be', jax.nn.silu(c), w_ada[l]) + b_ada[l]
        mod_c = jnp.einsum('d,de->e', jax.nn.silu(c_ctx), w_ada[l]) + b_ada[l]
        sh1, sc1, g1, sh2, sc2, g2 = jnp.split(mod_x[:, None, :], 6, axis=-1)
        csh1, csc1, cg1, csh2, csc2, cg2 = jnp.split(mod_c, 6, axis=-1)
        mparams = (qk_conv_w[l], qk_conv_b[l], b_igate[l], b_fgate[l])

        hc = ctx * (1 + csc1) + csh1
        if last:
            pc_m = jnp.einsum('bld,dn->bln', hc, w_in[l][:, 2 * GMLP_W:])
            _, _, ctx_state = mlstm_branch(pc_m, *mparams, zero_state(B))
        else:
            pc = jnp.einsum('bld,dn->bln', hc, w_in[l])
            ya_c = gmlp_mix(pc[..., :GMLP_W], pc[..., GMLP_W:2 * GMLP_W], gmlp_ln_g[l], gmlp_ws[l], gmlp_bs[l])
            hb_c, o_c, ctx_state = mlstm_branch(pc[..., 2 * GMLP_W:], *mparams, zero_state(B))
            ctx = layer_norm(ALPHA * ctx + cg1 * mixer_out(ya_c, hb_c, o_c, mix_norm_g[l], w_out[l]), ln1_g[l], ln1_b[l])
            f_c = conv_ffn(ctx * (1 + csc2) + csh2, w_up[l], ffn_conv_w[l], ffn_conv_b[l], w_down[l], None)
            ctx = layer_norm(ALPHA * ctx + cg2 * f_c, ln2_g[l], ln2_b[l])

        hx = x * (1 + sc1) + sh1
        px = jnp.einsum('bld,dn->bln', hx, w_in[l])
        ya = gmlp_mix(px[..., :GMLP_W], px[..., GMLP_W:2 * GMLP_W], gmlp_ln_g[l], gmlp_ws[l], gmlp_bs[l])
        hb, o_pre, _ = mlstm_branch(px[..., 2 * GMLP_W:], *mparams, ctx_state)
        x = layer_norm(ALPHA * x + g1 * mixer_out(ya, hb, o_pre, mix_norm_g[l], w_out[l]), ln1_g[l], ln1_b[l])
        f_x = conv_ffn(x * (1 + sc2) + sh2, w_up[l], ffn_conv_w[l], ffn_conv_b[l], w_down[l], rows)
        x = layer_norm(ALPHA * x + g2 * f_x, ln2_g[l], ln2_b[l])
    return x
```

```python
import functools

import jax
import jax.numpy as jnp
from jax import lax
from jax.experimental import pallas as pl
from jax.experimental.pallas import tpu as pltpu

D_MODEL = 1024
GRID_W = 64
CHUNK = 128
HEADS = 4
DH = 128
HALF_W = HEADS * DH
N_HD = 2 * HEADS
D_FF = 2688
DEPTH = 1
ALPHA = (2 * DEPTH) ** 0.25
EPS = 1e-5
HALO = 8
FFN_BLOCKS = (768, 768, 768, 384)

F32 = jnp.float32
BF16 = jnp.bfloat16
VMEM_LIMIT = 56 * 1024 * 1024


def _dot(a, b):
    return jnp.dot(a, b, preferred_element_type=F32)


def _norm_lanes(z):
    mu = jnp.mean(z, axis=-1, keepdims=True)
    d = z - mu
    var = jnp.mean(d * d, axis=-1, keepdims=True)
    return d * lax.rsqrt(var + EPS)


def _gelu_tanh(x):
    return 0.5 * x * (1.0 + jnp.tanh(0.7978845608028654 * (x + 0.044715 * (x * x * x))))


def _sigmoid(x):
    return 1.0 / (1.0 + jnp.exp(-x))


def _log_sigmoid(x):
    return jnp.minimum(x, 0.0) - jnp.log(1.0 + jnp.exp(-jnp.abs(x)))


def _const_spec(shape):
    nd = len(shape)
    return pl.BlockSpec(shape, lambda *_: (0,) * nd, pipeline_mode=pl.Buffered(1))


def _mod_kernel(c_ref, w_ref, b_ref, o_ref):
    cs = c_ref[...]
    a = cs * _sigmoid(cs)
    o_ref[...] = jnp.dot(a, w_ref[...], precision=lax.Precision.HIGHEST,
                         preferred_element_type=F32) + b_ref[...]


def _modulation(c_rows, w_ada, b_ada):
    n_out = w_ada.shape[1]
    bn = 1536
    return pl.pallas_call(
        _mod_kernel,
        grid=(n_out // bn,),
        in_specs=[pl.BlockSpec((8, D_MODEL), lambda j: (0, 0)),
                  pl.BlockSpec((D_MODEL, bn), lambda j: (0, j)),
                  pl.BlockSpec((1, bn), lambda j: (0, j))],
        out_specs=pl.BlockSpec((8, bn), lambda j: (0, j)),
        out_shape=jax.ShapeDtypeStruct((8, n_out), F32),
        compiler_params=pltpu.CompilerParams(vmem_limit_bytes=VMEM_LIMIT),
        name="mod",
    )(c_rows, w_ada, b_ada)


def _inproj_kernel(x_ref, xp_ref, xn_ref, mod_ref, wuv_ref, wqk_ref, wvo_ref, wg_ref,
                   lng_ref, ws_ref, bsb_ref, cw_ref, cb_ref, gbias_ref, ng_ref, tril_ref,
                   yan_ref, q_ref, kt_ref, vm_ref, o_ref, gb_ref, gc_ref, gct_ref, *, tm):
    i = pl.program_id(1)
    n_tiles = pl.num_programs(1)
    shift = mod_ref[0:1, :]
    scale = 1.0 + mod_ref[1:2, :]

    def modulate(xv):
        return (xv * scale + shift).astype(BF16)

    hx = modulate(x_ref[...])

    uv = _dot(hx, wuv_ref[...])
    u = _gelu_tanh(uv[:, :HALF_W])
    v = _gelu_tanh(uv[:, HALF_W:])
    for g in range(HEADS):
        lanes = slice(g * DH, (g + 1) * DH)
        vh = (_norm_lanes(v[:, lanes]) * lng_ref[:, lanes]).astype(BF16)
        for c in range(tm // CHUNK):
            rows = slice(c * CHUNK, (c + 1) * CHUNK)
            mixed = _dot(ws_ref[g], vh[rows]) + bsb_ref[:, lanes]
            ya = u[rows, lanes] * mixed
            yan_ref[rows, lanes] = (_norm_lanes(ya) * ng_ref[:, lanes]).astype(BF16)

    wqk = wqk_ref[...]
    pre = _dot(hx, wqk)
    prev = jnp.where(i > 0, _dot(modulate(xp_ref[...]), wqk), 0.0)
    nxt = jnp.where(i < n_tiles - 1, _dot(modulate(xn_ref[...]), wqk), 0.0)
    ext = jnp.concatenate([prev, pre, nxt], axis=0)
    n_ext = tm + 2 * HALO
    down = pltpu.roll(ext, 1, 0)[HALO:HALO + tm]
    up = pltpu.roll(ext, n_ext - 1, 0)[HALO:HALO + tm]
    conv = down * cw_ref[0:1, :] + pre * cw_ref[1:2, :] + up * cw_ref[2:3, :] + cb_ref[...]
    qk = conv * _sigmoid(conv)
    q_ref[...] = qk[:, :HALF_W].astype(BF16)
    kt_ref[...] = (qk[:, HALF_W:] * (DH ** -0.5)).T.astype(BF16)

    vo = _dot(hx, wvo_ref[...])
    vm_ref[...] = vo[:, :HALF_W].astype(BF16)
    o_ref[...] = vo[:, HALF_W:]

    gp = _dot(hx, wg_ref[...]) + gbias_ref[...]
    log_i = gp[:, :DH]
    log_f = _log_sigmoid(gp[:, DH:])
    tril = tril_ref[...]
    lane = lax.broadcasted_iota(jnp.int32, (CHUNK, DH), 1)
    for c in range(tm // CHUNK):
        rows = slice(c * CHUNK, (c + 1) * CHUNK)
        lf = log_f[rows]
        hi = lf.astype(BF16)
        r1 = lf - hi.astype(F32)
        mid = r1.astype(BF16)
        lo = (r1 - mid.astype(F32)).astype(BF16)
        csum = _dot(tril, hi) + _dot(tril, mid) + _dot(tril, lo)
        total = csum[CHUNK - 1:CHUNK, :]
        suffix = total - csum + lf
        b = jnp.where(lane < HEADS, csum, suffix)
        cdiff = b - log_i[rows]
        gb_ref[rows, :] = b
        gc_ref[rows, :] = cdiff
        gct_ref[:, rows] = cdiff.T[0:N_HD, :]


def _inproj(xs, mod6, wts, *, tm):
    B, S, _ = xs.shape
    nt = S // tm
    hb = tm // HALO
    n_halo_blocks = S // HALO
    kern = functools.partial(_inproj_kernel, tm=tm)
    tok = lambda w: pl.BlockSpec((None, tm, w), lambda b, i: (b, i, 0))
    in_specs = [
        tok(D_MODEL),
        pl.BlockSpec((None, HALO, D_MODEL), lambda b, i: (b, jnp.maximum(i * hb - 1, 0), 0)),
        pl.BlockSpec((None, HALO, D_MODEL), lambda b, i: (b, jnp.minimum((i + 1) * hb, n_halo_blocks - 1), 0)),
        pl.BlockSpec((None, 6, D_MODEL), lambda b, i: (b, 0, 0)),
    ] + [_const_spec(w.shape) for w in wts]
    out_shape = [
        jax.ShapeDtypeStruct((B, S, HALF_W), BF16),
        jax.ShapeDtypeStruct((B, S, HALF_W), BF16),
        jax.ShapeDtypeStruct((B, HALF_W, S), BF16),
        jax.ShapeDtypeStruct((B, S, HALF_W), BF16),
        jax.ShapeDtypeStruct((B, S, HALF_W), F32),
        jax.ShapeDtypeStruct((B, S, DH), F32),
        jax.ShapeDtypeStruct((B, S, DH), F32),
        jax.ShapeDtypeStruct((B, N_HD, S), F32),
    ]
    out_specs = [
        tok(HALF_W), tok(HALF_W),
        pl.BlockSpec((None, HALF_W, tm), lambda b, i: (b, 0, i)),
        tok(HALF_W), tok(HALF_W), tok(DH), tok(DH),
        pl.BlockSpec((None, N_HD, tm), lambda b, i: (b, 0, i)),
    ]
    return pl.pallas_call(
        kern, grid=(B, nt), in_specs=in_specs, out_specs=out_specs, out_shape=out_shape,
        compiler_params=pltpu.CompilerParams(
            dimension_semantics=("arbitrary", "arbitrary"), vmem_limit_bytes=VMEM_LIMIT),
        name="inproj",
    )(xs, xs, xs, mod6, *wts)


def _scan_kernel(qf_ref, ktf_ref, vf_ref, gbf_ref, gcf_ref, gctf_ref,
                 qb_ref, ktb_ref, vb_ref, gbb_ref, gcb_ref, gctb_ref,
                 c0_ref, m0_ref, hf_ref, hb_ref, cst_ref, mst_ref):
    @pl.when(pl.program_id(1) == 0)
    def _():
        cst_ref[...] = c0_ref[...]
        mst_ref[...] = m0_ref[...]

    t_idx = lax.broadcasted_iota(jnp.int32, (CHUNK, CHUNK), 0)
    s_idx = lax.broadcasted_iota(jnp.int32, (CHUNK, CHUNK), 1)
    lane = lax.broadcasted_iota(jnp.int32, (CHUNK, DH), 1)
    for j in range(N_HD):
        fwd = j < HEADS
        h = j % HEADS
        lanes = slice(h * DH, (h + 1) * DH)
        q_ref, kt_ref, v_ref, gb_ref, gc_ref, gct_ref, out_ref = (
            (qf_ref, ktf_ref, vf_ref, gbf_ref, gcf_ref, gctf_ref, hf_ref) if fwd else
            (qb_ref, ktb_ref, vb_ref, gbb_ref, gcb_ref, gctb_ref, hb_ref))
        q = q_ref[:, lanes]
        kt = kt_ref[lanes, :]
        v = v_ref[:, lanes]
        bcol = gb_ref[:, j:j + 1]
        ccol = gc_ref[:, j:j + 1]
        crow = gct_ref[j:j + 1, :]
        total = bcol[CHUNK - 1:CHUNK, :] if fwd else bcol[0:1, :]
        m = mst_ref[j:j + 1, 0:1]
        cst = cst_ref[j]

        keep = (s_idx <= t_idx) if fwd else (s_idx >= t_idx)
        dmat = jnp.where(keep, bcol - crow, -jnp.inf)
        inter = bcol + m
        m_t = jnp.maximum(inter, jnp.max(dmat, axis=1, keepdims=True))
        w_intra = jnp.exp(dmat - m_t)
        w_state = jnp.exp(inter - m_t)
        s = _dot(q, kt) * w_intra
        qc = _dot(q, cst.astype(BF16))
        num = _dot(s.astype(BF16), v) + w_state * qc[:, :DH]
        den = jnp.sum(s, axis=1, keepdims=True) + w_state * qc[:, DH:DH + 1]
        out_ref[:, lanes] = num / jnp.maximum(jnp.abs(den), jnp.exp(-m_t))

        gvec = total - ccol
        m_new = jnp.maximum(total + m, jnp.max(gvec, axis=0, keepdims=True))
        w_old = jnp.exp(total + m - m_new)
        w_k = jnp.exp(gvec - m_new)
        v_aug = jnp.concatenate([v.astype(F32) * w_k, jnp.where(lane == 0, w_k, 0.0)], axis=1)
        cst_ref[j] = w_old * cst + _dot(kt, v_aug.astype(BF16))
        mst_ref[j:j + 1, :] = jnp.broadcast_to(m_new, (1, DH))


def _scan(q, kt, v, gb, gc, gct, c0, m0):
    B, S, _ = q.shape
    nc = S // CHUNK
    fwd_tok = lambda w: pl.BlockSpec((None, CHUNK, w), lambda b, i: (b, i, 0))
    bwd_tok = lambda w: pl.BlockSpec((None, CHUNK, w), lambda b, i: (b, nc - 1 - i, 0))
    fwd_t = lambda r: pl.BlockSpec((None, r, CHUNK), lambda b, i: (b, 0, i))
    bwd_t = lambda r: pl.BlockSpec((None, r, CHUNK), lambda b, i: (b, 0, nc - 1 - i))
    c_spec = pl.BlockSpec((None, N_HD, DH, 2 * DH), lambda b, i: (b, 0, 0, 0))
    m_spec = pl.BlockSpec((None, N_HD, DH), lambda b, i: (b, 0, 0))
    return pl.pallas_call(
        _scan_kernel, grid=(B, nc),
        in_specs=[fwd_tok(HALF_W), fwd_t(HALF_W), fwd_tok(HALF_W), fwd_tok(DH), fwd_tok(DH), fwd_t(N_HD),
                  bwd_tok(HALF_W), bwd_t(HALF_W), bwd_tok(HALF_W), bwd_tok(DH), bwd_tok(DH), bwd_t(N_HD),
                  c_spec, m_spec],
        out_specs=[fwd_tok(HALF_W), bwd_tok(HALF_W), c_spec, m_spec],
        out_shape=[jax.ShapeDtypeStruct((B, S, HALF_W), F32),
                   jax.ShapeDtypeStruct((B, S, HALF_W), F32),
                   jax.ShapeDtypeStruct((B, N_HD, DH, 2 * DH), F32),
                   jax.ShapeDtypeStruct((B, N_HD, DH), F32)],
        compiler_params=pltpu.CompilerParams(
            dimension_semantics=("arbitrary", "arbitrary"), vmem_limit_bytes=VMEM_LIMIT),
        name="scan",
    )(q, kt, v, gb, gc, gct, q, kt, v, gb, gc, gct, c0, m0)


def _layer_norm(z, g_ref, b_ref):
    return _norm_lanes(z) * g_ref[...] + b_ref[...]


def _ffn_kernel(x_ref, yan_ref, hf_ref, hb_ref, o_ref, mod_ref, ng_ref, wout_ref,
                ln1g_ref, ln1b_ref, wup_ref, cw_ref, cb_ref, wdown_ref, ln2g_ref, ln2b_ref,
                out_ref, *, tm):
    h = hf_ref[...] + hb_ref[...]
    gated = []
    for g in range(HEADS):
        lanes = slice(g * DH, (g + 1) * DH)
        gated.append((_norm_lanes(h[:, lanes]) * ng_ref[:, lanes] * _sigmoid(o_ref[:, lanes])).astype(BF16))
    y = jnp.concatenate([yan_ref[...]] + gated, axis=1)
    mix = _dot(y, wout_ref[...])
    x1 = _layer_norm(ALPHA * x_ref[...] + mod_ref[2:3, :] * mix, ln1g_ref, ln1b_ref)

    h2 = (x1 * (1.0 + mod_ref[4:5, :]) + mod_ref[3:4, :]).astype(BF16)
    row = lax.broadcasted_iota(jnp.int32, (tm, 1), 0) % GRID_W
    not_first = (row != 0).astype(F32)
    not_last = (row != GRID_W - 1).astype(F32)
    acc = jnp.zeros((tm, D_MODEL), F32)
    start = 0
    for blk in FFN_BLOCKS:
        a = _dot(h2, wup_ref[:, 2 * start:2 * (start + blk)])
        left = pltpu.roll(a, 1, 0) * not_first
        right = pltpu.roll(a, tm - 1, 0) * not_last
        cols = slice(2 * start, 2 * (start + blk))
        a = left * cw_ref[0:1, cols] + a * cw_ref[1:2, cols] + right * cw_ref[2:3, cols] + cb_ref[:, cols]
        val, gate = a[:, :blk], a[:, blk:]
        act = (gate * _sigmoid(gate) * val).astype(BF16)
        acc = acc + _dot(act, wdown_ref[start:start + blk, :])
        start += blk
    out_ref[...] = _layer_norm(ALPHA * x1 + mod_ref[5:6, :] * acc, ln2g_ref, ln2b_ref)


def _ffn(x, yan, hf, hb, o_pre, mod6, wts, *, tm):
    B, S, _ = x.shape
    kern = functools.partial(_ffn_kernel, tm=tm)
    tok = lambda w: pl.BlockSpec((None, tm, w), lambda b, i: (b, i, 0))
    in_specs = [tok(D_MODEL), tok(HALF_W), tok(HALF_W), tok(HALF_W), tok(HALF_W),
                pl.BlockSpec((None, 6, D_MODEL), lambda b, i: (b, 0, 0))] + [_const_spec(w.shape) for w in wts]
    return pl.pallas_call(
        kern, grid=(B, S // tm), in_specs=in_specs, out_specs=tok(D_MODEL),
        out_shape=jax.ShapeDtypeStruct((B, S, D_MODEL), F32),
        compiler_params=pltpu.CompilerParams(
            dimension_semantics=("arbitrary", "arbitrary"), vmem_limit_bytes=VMEM_LIMIT),
        name="ffn",
    )(x, yan, hf, hb, o_pre, mod6, *wts)


def _interleave_blocks(val, gate, axis):
    parts, start = [], 0
    for blk in FFN_BLOCKS:
        parts.append(lax.slice_in_dim(val, start, start + blk, axis=axis))
        parts.append(lax.slice_in_dim(gate, start, start + blk, axis=axis))
        start += blk
    return jnp.concatenate(parts, axis=axis)


def kernel(x, c, ctx, c_ctx, w_ada, b_ada, w_in, gmlp_ln_g, gmlp_ws, gmlp_bs, qk_conv_w, qk_conv_b,
           b_igate, b_fgate, mix_norm_g, w_out, ln1_g, ln1_b, w_up, ffn_conv_w, ffn_conv_b, w_down,
           ln2_g, ln2_b):
    B, S, _ = x.shape
    assert DEPTH == 1 and w_in.shape[0] == 1
    l = 0

    c_rows = jnp.concatenate([c, c_ctx[None, :], jnp.zeros((8 - B - 1, D_MODEL), F32)], axis=0)
    mod = _modulation(c_rows, w_ada[l], b_ada[l][None, :])
    mod_x = mod[:B].reshape(B, 6, D_MODEL)
    mod_c = jnp.broadcast_to(mod[B].reshape(1, 6, D_MODEL), (B, 6, D_MODEL))

    wi = w_in[l]
    w_uv = wi[:, :2 * HALF_W].astype(BF16)
    w_qk = wi[:, 2 * HALF_W:4 * HALF_W].astype(BF16)
    w_vo = wi[:, 4 * HALF_W:6 * HALF_W].astype(BF16)
    gate_cols = wi[:, 6 * HALF_W:]
    pad = jnp.zeros((D_MODEL, DH - N_HD), F32)
    w_g = jnp.concatenate([gate_cols[:, :N_HD], pad, gate_cols[:, N_HD:], pad], axis=1).astype(BF16)
    zpad = jnp.zeros((DH - N_HD,), F32)
    gbias = jnp.concatenate([b_igate[l].reshape(-1), zpad, b_fgate[l].reshape(-1), zpad])[None, :]
    ln_g = gmlp_ln_g[l].reshape(1, HALF_W)
    ws = gmlp_ws[l].astype(BF16)
    bsb = jnp.repeat(gmlp_bs[l].T, DH, axis=1)
    norm_g = mix_norm_g[l][None, :]
    tril = jnp.tril(jnp.ones((CHUNK, CHUNK), F32)).astype(BF16)
    in_wts = (w_uv, w_qk, w_vo, w_g, ln_g, ws, bsb, qk_conv_w[l], qk_conv_b[l][None, :], gbias,
              norm_g[:, :HALF_W], tril)

    _, q_c, kt_c, v_c, _, gb_c, gc_c, gct_c = _inproj(ctx, mod_c, in_wts, tm=ctx.shape[1])
    c_zero = jnp.zeros((B, N_HD, DH, 2 * DH), F32)
    m_zero = jnp.zeros((B, N_HD, DH), F32)
    _, _, c_state, m_state = _scan(q_c, kt_c, v_c, gb_c, gc_c, gct_c, c_zero, m_zero)

    yan, q, kt, v, o_pre, gb, gc, gct = _inproj(x, mod_x, in_wts, tm=256)
    hf, hb, _, _ = _scan(q, kt, v, gb, gc, gct, c_state, m_state)

    wu = w_up[l]
    w_up_b = _interleave_blocks(wu[:, :D_FF], wu[:, D_FF:], 1).astype(BF16)
    cw = _interleave_blocks(ffn_conv_w[l][:, :D_FF], ffn_conv_w[l][:, D_FF:], 1)
    cb = _interleave_blocks(ffn_conv_b[l][None, :D_FF], ffn_conv_b[l][None, D_FF:], 1)
    ffn_wts = (norm_g[:, HALF_W:], w_out[l].astype(BF16), ln1_g[l][None, :], ln1_b[l][None, :],
               w_up_b, cw, cb, w_down[l].astype(BF16), ln2_g[l][None, :], ln2_b[l][None, :])
    return _ffn(x, yan, hf, hb, o_pre, mod_x, ffn_wts, tm=256)
```

```python
import functools

import jax
import jax.numpy as jnp
from jax import lax
from jax.experimental import pallas as pl
from jax.experimental.pallas import tpu as pltpu

D_MODEL = 1024
GRID_W = 64
CHUNK = 128
HEADS = 4
DH = 128
HALF_W = HEADS * DH
N_HD = 2 * HEADS
D_FF = 2688
DEPTH = 1
ALPHA = (2 * DEPTH) ** 0.25
EPS = 1e-5
HALO = 8
FFN_BLOCKS = (768, 768, 768, 384)

F32 = jnp.float32
BF16 = jnp.bfloat16
VMEM_LIMIT = 56 * 1024 * 1024


def _dot(a, b):
    return jnp.dot(a, b, preferred_element_type=F32)


def _norm_lanes(z):
    mu = jnp.mean(z, axis=-1, keepdims=True)
    d = z - mu
    var = jnp.mean(d * d, axis=-1, keepdims=True)
    return d * lax.rsqrt(var + EPS)


def _gelu_tanh(x):
    return 0.5 * x * (1.0 + jnp.tanh(0.7978845608028654 * (x + 0.044715 * (x * x * x))))


def _sigmoid(x):
    return 1.0 / (1.0 + jnp.exp(-x))


def _log_sigmoid(x):
    return jnp.minimum(x, 0.0) - jnp.log(1.0 + jnp.exp(-jnp.abs(x)))


def _const_spec(shape):
    nd = len(shape)
    return pl.BlockSpec(shape, lambda *_: (0,) * nd, pipeline_mode=pl.Buffered(1))


def _mod_kernel(c_ref, w_ref, b_ref, o_ref):
    cs = c_ref[...]
    a = cs * _sigmoid(cs)
    o_ref[...] = jnp.dot(a, w_ref[...], precision=lax.Precision.HIGHEST,
                         preferred_element_type=F32) + b_ref[...]


def _modulation(c_rows, w_ada, b_ada):
    n_out = w_ada.shape[1]
    bn = 1536
    return pl.pallas_call(
        _mod_kernel,
        grid=(n_out // bn,),
        in_specs=[pl.BlockSpec((8, D_MODEL), lambda j: (0, 0)),
                  pl.BlockSpec((D_MODEL, bn), lambda j: (0, j)),
                  pl.BlockSpec((1, bn), lambda j: (0, j))],
        out_specs=pl.BlockSpec((8, bn), lambda j: (0, j)),
        out_shape=jax.ShapeDtypeStruct((8, n_out), F32),
        compiler_params=pltpu.CompilerParams(vmem_limit_bytes=VMEM_LIMIT),
        name="mod",
    )(c_rows, w_ada, b_ada)


def _inproj_kernel(x_ref, xp_ref, xn_ref, mod_ref, wuv_ref, wqk_ref, wvo_ref, wg_ref,
                   lng_ref, ws_ref, bsb_ref, cw_ref, cb_ref, gbias_ref, ng_ref, tril_ref,
                   yan_ref, q_ref, kt_ref, vm_ref, o_ref, gb_ref, gcm_ref, gct_ref, gwk_ref, *, tm):
    i = pl.program_id(1)
    n_tiles = pl.num_programs(1)
    shift = mod_ref[0:1, :]
    scale = 1.0 + mod_ref[1:2, :]

    def modulate(xv):
        return (xv * scale + shift).astype(BF16)

    hx = modulate(x_ref[...])

    uv = _dot(hx, wuv_ref[...])
    u = _gelu_tanh(uv[:, :HALF_W])
    v = _gelu_tanh(uv[:, HALF_W:])
    for g in range(HEADS):
        lanes = slice(g * DH, (g + 1) * DH)
        vh = (_norm_lanes(v[:, lanes]) * lng_ref[:, lanes]).astype(BF16)
        for c in range(tm // CHUNK):
            rows = slice(c * CHUNK, (c + 1) * CHUNK)
            mixed = _dot(ws_ref[g], vh[rows]) + bsb_ref[:, lanes]
            ya = u[rows, lanes] * mixed
            yan_ref[rows, lanes] = (_norm_lanes(ya) * ng_ref[:, lanes]).astype(BF16)

    wqk = wqk_ref[...]
    pre = _dot(hx, wqk)
    prev = jnp.where(i > 0, _dot(modulate(xp_ref[...]), wqk), 0.0)
    nxt = jnp.where(i < n_tiles - 1, _dot(modulate(xn_ref[...]), wqk), 0.0)
    ext = jnp.concatenate([prev, pre, nxt], axis=0)
    n_ext = tm + 2 * HALO
    down = pltpu.roll(ext, 1, 0)[HALO:HALO + tm]
    up = pltpu.roll(ext, n_ext - 1, 0)[HALO:HALO + tm]
    conv = down * cw_ref[0:1, :] + pre * cw_ref[1:2, :] + up * cw_ref[2:3, :] + cb_ref[...]
    qk = conv * _sigmoid(conv)
    q_ref[...] = qk[:, :HALF_W].astype(BF16)
    kt_ref[...] = (qk[:, HALF_W:] * (DH ** -0.5)).T.astype(BF16)

    vo = _dot(hx, wvo_ref[...])
    vm_ref[...] = vo[:, :HALF_W].astype(BF16)
    o_ref[...] = vo[:, HALF_W:]

    gp = _dot(hx, wg_ref[...]) + gbias_ref[...]
    log_i = gp[:, :DH]
    log_f = _log_sigmoid(gp[:, DH:])
    tril = tril_ref[...]
    is_fwd = lax.broadcasted_iota(jnp.int32, (CHUNK, DH), 1) < HEADS
    row = lax.broadcasted_iota(jnp.int32, (CHUNK, DH), 0)
    for c in range(tm // CHUNK):
        rows = slice(c * CHUNK, (c + 1) * CHUNK)
        lf = log_f[rows]
        hi = lf.astype(BF16)
        r1 = lf - hi.astype(F32)
        mid = r1.astype(BF16)
        lo = (r1 - mid.astype(F32)).astype(BF16)
        csum = _dot(tril, hi) + _dot(tril, mid) + _dot(tril, lo)
        total = csum[CHUNK - 1:CHUNK, :]
        suffix = total - csum + lf
        b = jnp.where(is_fwd, csum, suffix)
        cdiff = b - log_i[rows]
        pmin, smin = cdiff, cdiff
        step = 1
        while step < CHUNK:
            pmin = jnp.minimum(pmin, jnp.where(row >= step, pltpu.roll(pmin, step, 0), jnp.inf))
            smin = jnp.minimum(smin, jnp.where(row < CHUNK - step, pltpu.roll(smin, CHUNK - step, 0), jnp.inf))
            step *= 2
        cmin = jnp.where(is_fwd, pmin, smin)
        cmin_all = jnp.where(is_fwd[0:1], cmin[CHUNK - 1:CHUNK], cmin[0:1])
        gb_ref[rows, :] = b
        gcm_ref[rows, :] = cmin
        gct_ref[:, rows] = cdiff.T[0:N_HD, :]
        gwk_ref[:, rows] = jnp.exp(cmin_all - cdiff).T[0:N_HD, :]


def _inproj(xs, mod6, wts, *, tm):
    B, S, _ = xs.shape
    nt = S // tm
    hb = tm // HALO
    n_halo_blocks = S // HALO
    kern = functools.partial(_inproj_kernel, tm=tm)
    tok = lambda w: pl.BlockSpec((None, tm, w), lambda b, i: (b, i, 0))
    in_specs = [
        tok(D_MODEL),
        pl.BlockSpec((None, HALO, D_MODEL), lambda b, i: (b, jnp.maximum(i * hb - 1, 0), 0)),
        pl.BlockSpec((None, HALO, D_MODEL), lambda b, i: (b, jnp.minimum((i + 1) * hb, n_halo_blocks - 1), 0)),
        pl.BlockSpec((None, 6, D_MODEL), lambda b, i: (b, 0, 0)),
    ] + [_const_spec(w.shape) for w in wts]
    out_shape = [
        jax.ShapeDtypeStruct((B, S, HALF_W), BF16),
        jax.ShapeDtypeStruct((B, S, HALF_W), BF16),
        jax.ShapeDtypeStruct((B, HALF_W, S), BF16),
        jax.ShapeDtypeStruct((B, S, HALF_W), BF16),
        jax.ShapeDtypeStruct((B, S, HALF_W), F32),
        jax.ShapeDtypeStruct((B, S, DH), F32),
        jax.ShapeDtypeStruct((B, S, DH), F32),
        jax.ShapeDtypeStruct((B, N_HD, S), F32),
        jax.ShapeDtypeStruct((B, N_HD, S), F32),
    ]
    out_specs = [
        tok(HALF_W), tok(HALF_W),
        pl.BlockSpec((None, HALF_W, tm), lambda b, i: (b, 0, i)),
        tok(HALF_W), tok(HALF_W), tok(DH), tok(DH),
        pl.BlockSpec((None, N_HD, tm), lambda b, i: (b, 0, i)),
        pl.BlockSpec((None, N_HD, tm), lambda b, i: (b, 0, i)),
    ]
    return pl.pallas_call(
        kern, grid=(B, nt), in_specs=in_specs, out_specs=out_specs, out_shape=out_shape,
        compiler_params=pltpu.CompilerParams(
            dimension_semantics=("arbitrary", "arbitrary"), vmem_limit_bytes=VMEM_LIMIT),
        name="inproj",
    )(xs, xs, xs, mod6, *wts)


def _scan_kernel(qf_ref, ktf_ref, vf_ref, gbf_ref, gcmf_ref, gctf_ref, gwkf_ref,
                 qb_ref, ktb_ref, vb_ref, gbb_ref, gcmb_ref, gctb_ref, gwkb_ref,
                 c0_ref, m0_ref, hf_ref, hb_ref, cst_ref, mst_ref):
    @pl.when(pl.program_id(1) == 0)
    def _():
        cst_ref[...] = c0_ref[...]
        mst_ref[...] = m0_ref[...]

    t_idx = lax.broadcasted_iota(jnp.int32, (CHUNK, CHUNK), 0)
    s_idx = lax.broadcasted_iota(jnp.int32, (CHUNK, CHUNK), 1)
    is_fwd = lax.broadcasted_iota(jnp.int32, (CHUNK, DH), 1) < HEADS
    last = CHUNK - 1

    m_row = mst_ref[0:1, :]
    b = jnp.where(is_fwd, gbf_ref[...], gbb_ref[...])
    cmin = jnp.where(is_fwd, gcmf_ref[...], gcmb_ref[...])
    total = jnp.where(is_fwd[0:1], gbf_ref[last:CHUNK, :], gbb_ref[0:1, :])
    cmin_end = jnp.where(is_fwd[0:1], gcmf_ref[last:CHUNK, :], gcmb_ref[0:1, :])
    rmax = b - cmin
    inter = b + m_row
    m_t = jnp.maximum(inter, rmax)
    w_intra = jnp.exp(rmax - m_t)
    w_state = jnp.exp(inter - m_t)
    floor = jnp.exp(-m_t)
    gmax = total - cmin_end
    m_new = jnp.maximum(total + m_row, gmax)
    w_old = jnp.exp(total + m_row - m_new)
    w_new = jnp.exp(gmax - m_new)
    mst_ref[...] = jnp.broadcast_to(m_new, (N_HD, DH))

    ones = jnp.ones((CHUNK, DH), BF16)
    pairs = []
    for j in range(N_HD):
        fwd = j < HEADS
        lanes = slice((j % HEADS) * DH, (j % HEADS + 1) * DH)
        refs = ((qf_ref, ktf_ref, vf_ref, gctf_ref, gwkf_ref, hf_ref) if fwd else
                (qb_ref, ktb_ref, vb_ref, gctb_ref, gwkb_ref, hb_ref))
        pairs.append((fwd, lanes) + refs)

    scores, from_state, updates, v_augs = [], [], [], []
    for j, (fwd, lanes, q_ref, kt_ref, v_ref, gct_ref, gwk_ref, out_ref) in enumerate(pairs):
        scores.append(_dot(q_ref[:, lanes], kt_ref[lanes, :]))
    for j, (fwd, lanes, q_ref, kt_ref, v_ref, gct_ref, gwk_ref, out_ref) in enumerate(pairs):
        from_state.append(_dot(q_ref[:, lanes], cst_ref[j].astype(BF16)))
    for j, (fwd, lanes, q_ref, kt_ref, v_ref, gct_ref, gwk_ref, out_ref) in enumerate(pairs):
        v_aug = jnp.concatenate([v_ref[:, lanes], ones], axis=1)
        v_augs.append(v_aug)
        kt_w = (kt_ref[lanes, :].astype(F32) * gwk_ref[j:j + 1, :]).astype(BF16)
        updates.append(_dot(kt_w, v_aug))
    for j, (fwd, lanes, q_ref, kt_ref, v_ref, gct_ref, gwk_ref, out_ref) in enumerate(pairs):
        keep = (s_idx <= t_idx) if fwd else (s_idx >= t_idx)
        decay = jnp.exp(jnp.where(keep, cmin[:, j:j + 1] - gct_ref[j:j + 1, :], -jnp.inf))
        sv = _dot((scores[j] * decay).astype(BF16), v_augs[j])
        wi = jnp.broadcast_to(w_intra[:, j:j + 1], (CHUNK, DH))
        ws = jnp.broadcast_to(w_state[:, j:j + 1], (CHUNK, DH))
        num = wi * sv[:, :DH] + ws * from_state[j][:, :DH]
        den = wi * sv[:, DH:] + ws * from_state[j][:, DH:]
        out_ref[:, lanes] = num / jnp.maximum(jnp.abs(den), floor[:, j:j + 1])
        cst_ref[j] = w_old[:, j:j + 1] * cst_ref[j] + w_new[:, j:j + 1] * updates[j]


def _scan(q, kt, v, gb, gcm, gct, gwk, c0, m0):
    B, S, _ = q.shape
    nc = S // CHUNK
    fwd_tok = lambda w: pl.BlockSpec((None, CHUNK, w), lambda b, i: (b, i, 0))
    bwd_tok = lambda w: pl.BlockSpec((None, CHUNK, w), lambda b, i: (b, nc - 1 - i, 0))
    fwd_t = lambda r: pl.BlockSpec((None, r, CHUNK), lambda b, i: (b, 0, i))
    bwd_t = lambda r: pl.BlockSpec((None, r, CHUNK), lambda b, i: (b, 0, nc - 1 - i))
    c_spec = pl.BlockSpec((None, N_HD, DH, 2 * DH), lambda b, i: (b, 0, 0, 0))
    m_spec = pl.BlockSpec((None, N_HD, DH), lambda b, i: (b, 0, 0))
    return pl.pallas_call(
        _scan_kernel, grid=(B, nc),
        in_specs=[fwd_tok(HALF_W), fwd_t(HALF_W), fwd_tok(HALF_W), fwd_tok(DH), fwd_tok(DH), fwd_t(N_HD), fwd_t(N_HD),
                  bwd_tok(HALF_W), bwd_t(HALF_W), bwd_tok(HALF_W), bwd_tok(DH), bwd_tok(DH), bwd_t(N_HD), bwd_t(N_HD),
                  c_spec, m_spec],
        out_specs=[fwd_tok(HALF_W), bwd_tok(HALF_W), c_spec, m_spec],
        out_shape=[jax.ShapeDtypeStruct((B, S, HALF_W), F32),
                   jax.ShapeDtypeStruct((B, S, HALF_W), F32),
                   jax.ShapeDtypeStruct((B, N_HD, DH, 2 * DH), F32),
                   jax.ShapeDtypeStruct((B, N_HD, DH), F32)],
        compiler_params=pltpu.CompilerParams(
            dimension_semantics=("arbitrary", "arbitrary"), vmem_limit_bytes=VMEM_LIMIT),
        name="scan",
    )(q, kt, v, gb, gcm, gct, gwk, q, kt, v, gb, gcm, gct, gwk, c0, m0)


def _layer_norm(z, g_ref, b_ref):
    return _norm_lanes(z) * g_ref[...] + b_ref[...]


def _ffn_kernel(x_ref, yan_ref, hf_ref, hb_ref, o_ref, mod_ref, ng_ref, wout_ref,
                ln1g_ref, ln1b_ref, wup_ref, cw_ref, cb_ref, wdown_ref, ln2g_ref, ln2b_ref,
                out_ref, *, tm):
    h = hf_ref[...] + hb_ref[...]
    gated = []
    for g in range(HEADS):
        lanes = slice(g * DH, (g + 1) * DH)
        gated.append((_norm_lanes(h[:, lanes]) * ng_ref[:, lanes] * _sigmoid(o_ref[:, lanes])).astype(BF16))
    y = jnp.concatenate([yan_ref[...]] + gated, axis=1)
    mix = _dot(y, wout_ref[...])
    x1 = _layer_norm(ALPHA * x_ref[...] + mod_ref[2:3, :] * mix, ln1g_ref, ln1b_ref)

    h2 = (x1 * (1.0 + mod_ref[4:5, :]) + mod_ref[3:4, :]).astype(BF16)
    row = lax.broadcasted_iota(jnp.int32, (tm, 1), 0) % GRID_W
    not_first = (row != 0).astype(F32)
    not_last = (row != GRID_W - 1).astype(F32)
    acc = jnp.zeros((tm, D_MODEL), F32)
    start = 0
    for blk in FFN_BLOCKS:
        a = _dot(h2, wup_ref[:, 2 * start:2 * (start + blk)])
        left = pltpu.roll(a, 1, 0) * not_first
        right = pltpu.roll(a, tm - 1, 0) * not_last
        cols = slice(2 * start, 2 * (start + blk))
        a = left * cw_ref[0:1, cols] + a * cw_ref[1:2, cols] + right * cw_ref[2:3, cols] + cb_ref[:, cols]
        val, gate = a[:, :blk], a[:, blk:]
        act = (gate * _sigmoid(gate) * val).astype(BF16)
        acc = acc + _dot(act, wdown_ref[start:start + blk, :])
        start += blk
    out_ref[...] = _layer_norm(ALPHA * x1 + mod_ref[5:6, :] * acc, ln2g_ref, ln2b_ref)


def _ffn(x, yan, hf, hb, o_pre, mod6, wts, *, tm):
    B, S, _ = x.shape
    kern = functools.partial(_ffn_kernel, tm=tm)
    tok = lambda w: pl.BlockSpec((None, tm, w), lambda b, i: (b, i, 0))
    in_specs = [tok(D_MODEL), tok(HALF_W), tok(HALF_W), tok(HALF_W), tok(HALF_W),
                pl.BlockSpec((None, 6, D_MODEL), lambda b, i: (b, 0, 0))] + [_const_spec(w.shape) for w in wts]
    return pl.pallas_call(
        kern, grid=(B, S // tm), in_specs=in_specs, out_specs=tok(D_MODEL),
        out_shape=jax.ShapeDtypeStruct((B, S, D_MODEL), F32),
        compiler_params=pltpu.CompilerParams(
            dimension_semantics=("arbitrary", "arbitrary"), vmem_limit_bytes=VMEM_LIMIT),
        name="ffn",
    )(x, yan, hf, hb, o_pre, mod6, *wts)


def _interleave_blocks(val, gate, axis):
    parts, start = [], 0
    for blk in FFN_BLOCKS:
        parts.append(lax.slice_in_dim(val, start, start + blk, axis=axis))
        parts.append(lax.slice_in_dim(gate, start, start + blk, axis=axis))
        start += blk
    return jnp.concatenate(parts, axis=axis)


def kernel(x, c, ctx, c_ctx, w_ada, b_ada, w_in, gmlp_ln_g, gmlp_ws, gmlp_bs, qk_conv_w, qk_conv_b,
           b_igate, b_fgate, mix_norm_g, w_out, ln1_g, ln1_b, w_up, ffn_conv_w, ffn_conv_b, w_down,
           ln2_g, ln2_b):
    B, S, _ = x.shape
    assert DEPTH == 1 and w_in.shape[0] == 1
    l = 0

    c_rows = jnp.concatenate([c, c_ctx[None, :], jnp.zeros((8 - B - 1, D_MODEL), F32)], axis=0)
    mod = _modulation(c_rows, w_ada[l], b_ada[l][None, :])
    mod_x = mod[:B].reshape(B, 6, D_MODEL)
    mod_c = jnp.broadcast_to(mod[B].reshape(1, 6, D_MODEL), (B, 6, D_MODEL))

    wi = w_in[l]
    w_uv = wi[:, :2 * HALF_W].astype(BF16)
    w_qk = wi[:, 2 * HALF_W:4 * HALF_W].astype(BF16)
    w_vo = wi[:, 4 * HALF_W:6 * HALF_W].astype(BF16)
    gate_cols = wi[:, 6 * HALF_W:]
    pad = jnp.zeros((D_MODEL, DH - N_HD), F32)
    w_g = jnp.concatenate([gate_cols[:, :N_HD], pad, gate_cols[:, N_HD:], pad], axis=1).astype(BF16)
    zpad = jnp.zeros((DH - N_HD,), F32)
    gbias = jnp.concatenate([b_igate[l].reshape(-1), zpad, b_fgate[l].reshape(-1), zpad])[None, :]
    ln_g = gmlp_ln_g[l].reshape(1, HALF_W)
    ws = gmlp_ws[l].astype(BF16)
    bsb = jnp.repeat(gmlp_bs[l].T, DH, axis=1)
    norm_g = mix_norm_g[l][None, :]
    tril = jnp.tril(jnp.ones((CHUNK, CHUNK), F32)).astype(BF16)
    in_wts = (w_uv, w_qk, w_vo, w_g, ln_g, ws, bsb, qk_conv_w[l], qk_conv_b[l][None, :], gbias,
              norm_g[:, :HALF_W], tril)

    _, q_c, kt_c, v_c, _, *gates_c = _inproj(ctx, mod_c, in_wts, tm=ctx.shape[1])
    c_zero = jnp.zeros((B, N_HD, DH, 2 * DH), F32)
    m_zero = jnp.zeros((B, N_HD, DH), F32)
    _, _, c_state, m_state = _scan(q_c, kt_c, v_c, *gates_c, c_zero, m_zero)

    yan, q, kt, v, o_pre, *gates = _inproj(x, mod_x, in_wts, tm=256)
    hf, hb, _, _ = _scan(q, kt, v, *gates, c_state, m_state)

    wu = w_up[l]
    w_up_b = _interleave_blocks(wu[:, :D_FF], wu[:, D_FF:], 1).astype(BF16)
    cw = _interleave_blocks(ffn_conv_w[l][:, :D_FF], ffn_conv_w[l][:, D_FF:], 1)
    cb = _interleave_blocks(ffn_conv_b[l][None, :D_FF], ffn_conv_b[l][None, D_FF:], 1)
    ffn_wts = (norm_g[:, HALF_W:], w_out[l].astype(BF16), ln1_g[l][None, :], ln1_b[l][None, :],
               w_up_b, cw, cb, w_down[l].astype(BF16), ln2_g[l][None, :], ln2_b[l][None, :])
    return _ffn(x, yan, hf, hb, o_pre, mod_x, ffn_wts, tm=256)
```

```python
import functools

import jax
import jax.numpy as jnp
from jax import lax
from jax.experimental import pallas as pl
from jax.experimental.pallas import tpu as pltpu

D_MODEL = 1024
GRID_W = 64
CHUNK = 128
HEADS = 4
DH = 128
HALF_W = HEADS * DH
N_HD = 2 * HEADS
D_FF = 2688
DEPTH = 1
ALPHA = (2 * DEPTH) ** 0.25
EPS = 1e-5
NSLAB = 8
HALO = 8
FFN_BLOCKS = (768, 768, 768, 384)

F32 = jnp.float32
BF16 = jnp.bfloat16
VMEM_LIMIT = 56 * 1024 * 1024


def _dot(a, b):
    return jnp.dot(a, b, preferred_element_type=F32)


def _norm_lanes(z):
    mu = jnp.mean(z, axis=-1, keepdims=True)
    d = z - mu
    var = jnp.mean(d * d, axis=-1, keepdims=True)
    return d * lax.rsqrt(var + EPS)


def _gelu_tanh(x):
    half = 0.5 * x
    return half + half * jnp.tanh(x * (0.7978845608028654 + 0.035677408136300125 * (x * x)))


def _sigmoid(x):
    return 0.5 * jnp.tanh(0.5 * x) + 0.5


def _log_sigmoid(x):
    return jnp.minimum(x, 0.0) - jnp.log(1.0 + jnp.exp(-jnp.abs(x)))


def _const_spec(shape):
    nd = len(shape)
    return pl.BlockSpec(shape, lambda *_: (0,) * nd, pipeline_mode=pl.Buffered(1))


def _mod_kernel(c_ref, w_ref, b_ref, o_ref):
    cs = c_ref[...]
    a = cs * _sigmoid(cs)
    o_ref[...] = jnp.dot(a, w_ref[...], precision=lax.Precision.HIGHEST,
                         preferred_element_type=F32) + b_ref[...]


def _modulation(c_rows, w_ada, b_ada):
    n_out = w_ada.shape[1]
    bn = 1536
    return pl.pallas_call(
        _mod_kernel,
        grid=(n_out // bn,),
        in_specs=[pl.BlockSpec((8, D_MODEL), lambda j: (0, 0)),
                  pl.BlockSpec((D_MODEL, bn), lambda j: (0, j)),
                  pl.BlockSpec((1, bn), lambda j: (0, j))],
        out_specs=pl.BlockSpec((8, bn), lambda j: (0, j)),
        out_shape=jax.ShapeDtypeStruct((8, n_out), F32),
        compiler_params=pltpu.CompilerParams(vmem_limit_bytes=VMEM_LIMIT),
        name="mod",
    )(c_rows, w_ada, b_ada)


def _inproj_kernel(x_ref, xp_ref, xn_ref, mod_ref, wuv_ref, wqk_ref, wvo_ref, wg_ref,
                   lng_ref, ws_ref, bsb_ref, cw_ref, cb_ref, gbi_ref, gbf_ref, ng_ref,
                   yan_ref, q_ref, kt_ref, vm_ref, o_ref, gb_ref, gcm_ref, gct_ref, gwk_ref, *, tm):
    i = pl.program_id(1)
    n_tiles = pl.num_programs(1)
    shift = mod_ref[0:1, :]
    scale = 1.0 + mod_ref[1:2, :]

    def modulate(xv):
        return (xv * scale + shift).astype(BF16)

    hx = modulate(x_ref[...])
    h_prev = jnp.where(i > 0, modulate(xp_ref[...]), jnp.zeros((), BF16))
    h_next = jnp.where(i < n_tiles - 1, modulate(xn_ref[...]), jnp.zeros((), BF16))
    gp = _dot(hx, wg_ref[...])
    uv = _dot(hx, wuv_ref[...])
    ext = _dot(jnp.concatenate([h_prev, hx, h_next], axis=0), wqk_ref[...])
    vo = _dot(hx, wvo_ref[...])

    u = _gelu_tanh(uv[:, :HALF_W])
    v = _gelu_tanh(uv[:, HALF_W:])
    for g in range(HEADS):
        lanes = slice(g * DH, (g + 1) * DH)
        vh = (_norm_lanes(v[:, lanes]) * lng_ref[:, lanes]).astype(BF16)
        for c in range(tm // CHUNK):
            rows = slice(c * CHUNK, (c + 1) * CHUNK)
            mixed = _dot(ws_ref[g], vh[rows]) + bsb_ref[:, lanes]
            ya = u[rows, lanes] * mixed
            yan_ref[rows, lanes] = (_norm_lanes(ya) * ng_ref[:, lanes]).astype(BF16)

    n_ext = tm + 2 * HALO
    pre = ext[HALO:HALO + tm]
    down = pltpu.roll(ext, 1, 0)[HALO:HALO + tm]
    up = pltpu.roll(ext, n_ext - 1, 0)[HALO:HALO + tm]
    conv = down * cw_ref[0:1, :] + pre * cw_ref[1:2, :] + up * cw_ref[2:3, :] + cb_ref[...]
    qk = conv * _sigmoid(conv)
    q_ref[...] = qk[:, :HALF_W].astype(BF16)
    kt_ref[...] = (qk[:, HALF_W:] * (DH ** -0.5)).T.astype(BF16)

    vm_ref[...] = vo[:, :HALF_W].astype(BF16)
    o_ref[...] = vo[:, HALF_W:]

    pos = lax.broadcasted_iota(jnp.int32, (N_HD, CHUNK), 1)
    is_fwd = lax.broadcasted_iota(jnp.int32, (N_HD, CHUNK), 0) < HEADS
    pad = jnp.zeros((DH - N_HD, CHUNK), F32)

    def scan_lanes(z, op, fill, reverse):
        step = 1
        while step < CHUNK:
            if reverse:
                moved = jnp.where(pos < CHUNK - step, pltpu.roll(z, CHUNK - step, 1), fill)
            else:
                moved = jnp.where(pos >= step, pltpu.roll(z, step, 1), fill)
            z = op(z, moved)
            step *= 2
        return z

    for c in range(tm // CHUNK):
        rows = slice(c * CHUNK, (c + 1) * CHUNK)
        gt = gp[rows].T
        log_i = gt[0:N_HD] + gbi_ref[...]
        lf = _log_sigmoid(gt[N_HD:2 * N_HD] + gbf_ref[...])
        csum = scan_lanes(lf, jnp.add, 0.0, False)
        suffix = csum[:, CHUNK - 1:CHUNK] - csum + lf
        b = jnp.where(is_fwd, csum, suffix)
        cdiff = b - log_i
        cmin = jnp.where(is_fwd, scan_lanes(cdiff, jnp.minimum, jnp.inf, False),
                         scan_lanes(cdiff, jnp.minimum, jnp.inf, True))
        cmin_end = jnp.where(is_fwd[:, 0:1], cmin[:, CHUNK - 1:CHUNK], cmin[:, 0:1])
        gct_ref[:, rows] = cdiff
        gwk_ref[:, rows] = jnp.exp(cmin_end - cdiff)
        gb_ref[rows, :] = jnp.concatenate([b, pad], axis=0).T
        gcm_ref[rows, :] = jnp.concatenate([cmin, pad], axis=0).T


def _inproj(xs, mod6, wts, *, tm):
    B, S, _ = xs.shape
    nt = S // tm
    hb = tm // HALO
    n_halo_blocks = S // HALO
    kern = functools.partial(_inproj_kernel, tm=tm)
    tok = lambda w: pl.BlockSpec((None, tm, w), lambda b, i: (b, i, 0))
    in_specs = [
        tok(D_MODEL),
        pl.BlockSpec((None, HALO, D_MODEL), lambda b, i: (b, jnp.maximum(i * hb - 1, 0), 0)),
        pl.BlockSpec((None, HALO, D_MODEL), lambda b, i: (b, jnp.minimum((i + 1) * hb, n_halo_blocks - 1), 0)),
        pl.BlockSpec((None, 6, D_MODEL), lambda b, i: (b, 0, 0)),
    ] + [_const_spec(w.shape) for w in wts]
    out_shape = [
        jax.ShapeDtypeStruct((B, S, HALF_W), BF16),
        jax.ShapeDtypeStruct((B, S, HALF_W), BF16),
        jax.ShapeDtypeStruct((B, HALF_W, S), BF16),
        jax.ShapeDtypeStruct((B, S, HALF_W), BF16),
        jax.ShapeDtypeStruct((B, S, HALF_W), F32),
        jax.ShapeDtypeStruct((B, S, DH), F32),
        jax.ShapeDtypeStruct((B, S, DH), F32),
        jax.ShapeDtypeStruct((B, N_HD, S), F32),
        jax.ShapeDtypeStruct((B, N_HD, S), F32),
    ]
    out_specs = [
        tok(HALF_W), tok(HALF_W),
        pl.BlockSpec((None, HALF_W, tm), lambda b, i: (b, 0, i)),
        tok(HALF_W), tok(HALF_W), tok(DH), tok(DH),
        pl.BlockSpec((None, N_HD, tm), lambda b, i: (b, 0, i)),
        pl.BlockSpec((None, N_HD, tm), lambda b, i: (b, 0, i)),
    ]
    return pl.pallas_call(
        kern, grid=(B, nt), in_specs=in_specs, out_specs=out_specs, out_shape=out_shape,
        compiler_params=pltpu.CompilerParams(
            dimension_semantics=("arbitrary", "arbitrary"), vmem_limit_bytes=VMEM_LIMIT),
        name="inproj",
    )(xs, xs, xs, mod6, *wts)


def _scan_kernel(qf_ref, ktf_ref, vf_ref, gbf_ref, gcmf_ref, gctf_ref, gwkf_ref,
                 qb_ref, ktb_ref, vb_ref, gbb_ref, gcmb_ref, gctb_ref, gwkb_ref,
                 c0_ref, m0_ref, hf_ref, hb_ref, cst_ref, mst_ref):
    @pl.when(pl.program_id(1) == 0)
    def _():
        cst_ref[...] = c0_ref[...]
        mst_ref[...] = m0_ref[...]

    t_idx = lax.broadcasted_iota(jnp.int32, (CHUNK, CHUNK), 0)
    s_idx = lax.broadcasted_iota(jnp.int32, (CHUNK, CHUNK), 1)
    is_fwd = lax.broadcasted_iota(jnp.int32, (CHUNK, DH), 1) < HEADS
    last = CHUNK - 1

    m_row = mst_ref[0:1, :]
    b = jnp.where(is_fwd, gbf_ref[...], gbb_ref[...])
    cmin = jnp.where(is_fwd, gcmf_ref[...], gcmb_ref[...])
    total = jnp.where(is_fwd[0:1], gbf_ref[last:CHUNK, :], gbb_ref[0:1, :])
    cmin_end = jnp.where(is_fwd[0:1], gcmf_ref[last:CHUNK, :], gcmb_ref[0:1, :])
    rmax = b - cmin
    inter = b + m_row
    m_t = jnp.maximum(inter, rmax)
    w_intra = jnp.exp(rmax - m_t)
    w_state = jnp.exp(inter - m_t)
    floor = jnp.exp(-m_t)
    gmax = total - cmin_end
    m_new = jnp.maximum(total + m_row, gmax)
    w_old = jnp.exp(total + m_row - m_new)
    w_new = jnp.exp(gmax - m_new)
    mst_ref[...] = jnp.broadcast_to(m_new, (N_HD, DH))

    ones = jnp.ones((CHUNK, DH), BF16)
    pairs = []
    for j in range(N_HD):
        fwd = j < HEADS
        lanes = slice((j % HEADS) * DH, (j % HEADS + 1) * DH)
        refs = ((qf_ref, ktf_ref, vf_ref, gctf_ref, gwkf_ref, hf_ref) if fwd else
                (qb_ref, ktb_ref, vb_ref, gctb_ref, gwkb_ref, hb_ref))
        pairs.append((fwd, lanes) + refs)

    scores, from_state, updates, v_augs = [], [], [], []
    for j, (fwd, lanes, q_ref, kt_ref, v_ref, gct_ref, gwk_ref, out_ref) in enumerate(pairs):
        scores.append(_dot(q_ref[:, lanes], kt_ref[lanes, :]))
    for j, (fwd, lanes, q_ref, kt_ref, v_ref, gct_ref, gwk_ref, out_ref) in enumerate(pairs):
        from_state.append(_dot(q_ref[:, lanes], cst_ref[j].astype(BF16)))
    for j, (fwd, lanes, q_ref, kt_ref, v_ref, gct_ref, gwk_ref, out_ref) in enumerate(pairs):
        v_aug = jnp.concatenate([v_ref[:, lanes], ones], axis=1)
        v_augs.append(v_aug)
        kt_w = (kt_ref[lanes, :].astype(F32) * gwk_ref[j:j + 1, :]).astype(BF16)
        updates.append(_dot(kt_w, v_aug))
    for j, (fwd, lanes, q_ref, kt_ref, v_ref, gct_ref, gwk_ref, out_ref) in enumerate(pairs):
        keep = (s_idx <= t_idx) if fwd else (s_idx >= t_idx)
        decay = jnp.exp(jnp.where(keep, cmin[:, j:j + 1] - gct_ref[j:j + 1, :], -jnp.inf))
        sv = _dot((scores[j] * decay).astype(BF16), v_augs[j])
        wi = jnp.broadcast_to(w_intra[:, j:j + 1], (CHUNK, DH))
        ws = jnp.broadcast_to(w_state[:, j:j + 1], (CHUNK, DH))
        num = wi * sv[:, :DH] + ws * from_state[j][:, :DH]
        den = wi * sv[:, DH:] + ws * from_state[j][:, DH:]
        out_ref[:, lanes] = num / jnp.maximum(jnp.abs(den), floor[:, j:j + 1])
        cst_ref[j] = w_old[:, j:j + 1] * cst_ref[j] + w_new[:, j:j + 1] * updates[j]


def _scan(q, kt, v, gb, gcm, gct, gwk, c0, m0):
    B, S, _ = q.shape
    nc = S // CHUNK
    fwd_tok = lambda w: pl.BlockSpec((None, CHUNK, w), lambda b, i: (b, i, 0))
    bwd_tok = lambda w: pl.BlockSpec((None, CHUNK, w), lambda b, i: (b, nc - 1 - i, 0))
    fwd_t = lambda r: pl.BlockSpec((None, r, CHUNK), lambda b, i: (b, 0, i))
    bwd_t = lambda r: pl.BlockSpec((None, r, CHUNK), lambda b, i: (b, 0, nc - 1 - i))
    c_spec = pl.BlockSpec((None, N_HD, DH, 2 * DH), lambda b, i: (b, 0, 0, 0))
    m_spec = pl.BlockSpec((None, N_HD, DH), lambda b, i: (b, 0, 0))
    return pl.pallas_call(
        _scan_kernel, grid=(B, nc),
        in_specs=[fwd_tok(HALF_W), fwd_t(HALF_W), fwd_tok(HALF_W), fwd_tok(DH), fwd_tok(DH), fwd_t(N_HD), fwd_t(N_HD),
                  bwd_tok(HALF_W), bwd_t(HALF_W), bwd_tok(HALF_W), bwd_tok(DH), bwd_tok(DH), bwd_t(N_HD), bwd_t(N_HD),
                  c_spec, m_spec],
        out_specs=[fwd_tok(HALF_W), bwd_tok(HALF_W), c_spec, m_spec],
        out_shape=[jax.ShapeDtypeStruct((B, S, HALF_W), F32),
                   jax.ShapeDtypeStruct((B, S, HALF_W), F32),
                   jax.ShapeDtypeStruct((B, N_HD, DH, 2 * DH), F32),
                   jax.ShapeDtypeStruct((B, N_HD, DH), F32)],
        compiler_params=pltpu.CompilerParams(
            dimension_semantics=("arbitrary", "arbitrary"), vmem_limit_bytes=VMEM_LIMIT),
        name="scan",
    )(q, kt, v, gb, gcm, gct, gwk, q, kt, v, gb, gcm, gct, gwk, c0, m0)


def _layer_norm(z, g_ref, b_ref):
    return _norm_lanes(z) * g_ref[...] + b_ref[...]


def _ffn_kernel(x_ref, yan_ref, hf_ref, hb_ref, o_ref, mod_ref, ng_ref, wout_ref,
                ln1g_ref, ln1b_ref, wup_ref, cw_ref, cb_ref, wdown_ref, ln2g_ref, ln2b_ref,
                out_ref, perm_ref, *, tm):
    h = hf_ref[...] + hb_ref[...]
    gated = []
    for g in range(HEADS):
        lanes = slice(g * DH, (g + 1) * DH)
        gated.append((_norm_lanes(h[:, lanes]) * ng_ref[:, lanes] * _sigmoid(o_ref[:, lanes])).astype(BF16))
    y = jnp.concatenate([yan_ref[...]] + gated, axis=1)
    mix = _dot(y, wout_ref[...])
    x1n = _layer_norm(ALPHA * x_ref[...] + mod_ref[2:3, :] * mix, ln1g_ref, ln1b_ref)
    n_lt = D_MODEL // DH
    for c in range(n_lt):
        perm_ref[c] = x1n[:, c * DH:(c + 1) * DH]

    ns = tm // NSLAB
    x1 = jnp.concatenate(
        [jnp.concatenate([perm_ref[c, pl.ds(j, ns, stride=NSLAB), :] for c in range(n_lt)], axis=1)
         for j in range(NSLAB)], axis=0)
    h2 = (x1 * (1.0 + mod_ref[4:5, :]) + mod_ref[3:4, :]).astype(BF16)
    starts = [sum(FFN_BLOCKS[:k]) for k in range(len(FFN_BLOCKS))]

    def up(k):
        return _dot(h2, wup_ref[:, 2 * starts[k]:2 * (starts[k] + FFN_BLOCKS[k])])

    def hidden(a, k):
        blk = FFN_BLOCKS[k]
        cols = slice(2 * starts[k], 2 * (starts[k] + blk))
        q8 = lax.broadcasted_iota(jnp.int32, (ns, 2 * blk), 0) % NSLAB
        before_first = jnp.where(q8 == 0, 0.0, pltpu.roll(a[tm - ns:], 1, 0))
        after_last = jnp.where(q8 == NSLAB - 1, 0.0, pltpu.roll(a[:ns], ns - 1, 0))
        left = jnp.concatenate([before_first, a[:tm - ns]], axis=0)
        right = jnp.concatenate([a[ns:], after_last], axis=0)
        a = left * cw_ref[0:1, cols] + a * cw_ref[1:2, cols] + right * cw_ref[2:3, cols] + cb_ref[:, cols]
        prod = a[:, blk:] * a[:, :blk]
        return (prod + prod * jnp.tanh(a[:, blk:])).astype(BF16)

    acc = None
    pending = up(0)
    for k, blk in enumerate(FFN_BLOCKS):
        following = up(k + 1) if k + 1 < len(FFN_BLOCKS) else None
        part = _dot(hidden(pending, k), wdown_ref[starts[k]:starts[k] + blk, :])
        acc = part if acc is None else acc + part
        pending = following
    out = _layer_norm(ALPHA * x1 + mod_ref[5:6, :] * acc, ln2g_ref, ln2b_ref)
    for c in range(n_lt):
        for j in range(NSLAB):
            perm_ref[c, pl.ds(j, ns, stride=NSLAB), :] = out[j * ns:(j + 1) * ns, c * DH:(c + 1) * DH]
    for c in range(n_lt):
        out_ref[:, c * DH:(c + 1) * DH] = perm_ref[c]


def _ffn(x, yan, hf, hb, o_pre, mod6, wts, *, tm):
    B, S, _ = x.shape
    kern = functools.partial(_ffn_kernel, tm=tm)
    tok = lambda w: pl.BlockSpec((None, tm, w), lambda b, i: (b, i, 0))
    in_specs = [tok(D_MODEL), tok(HALF_W), tok(HALF_W), tok(HALF_W), tok(HALF_W),
                pl.BlockSpec((None, 6, D_MODEL), lambda b, i: (b, 0, 0))] + [_const_spec(w.shape) for w in wts]
    return pl.pallas_call(
        kern, grid=(B, S // tm), in_specs=in_specs, out_specs=tok(D_MODEL),
        out_shape=jax.ShapeDtypeStruct((B, S, D_MODEL), F32),
        scratch_shapes=[pltpu.VMEM((D_MODEL // DH, tm, DH), F32)],
        compiler_params=pltpu.CompilerParams(
            dimension_semantics=("arbitrary", "arbitrary"), vmem_limit_bytes=VMEM_LIMIT),
        name="ffn",
    )(x, yan, hf, hb, o_pre, mod6, *wts)


def _interleave_blocks(val, gate, axis):
    parts, start = [], 0
    for blk in FFN_BLOCKS:
        parts.append(lax.slice_in_dim(val, start, start + blk, axis=axis))
        parts.append(lax.slice_in_dim(gate, start, start + blk, axis=axis))
        start += blk
    return jnp.concatenate(parts, axis=axis)


def kernel(x, c, ctx, c_ctx, w_ada, b_ada, w_in, gmlp_ln_g, gmlp_ws, gmlp_bs, qk_conv_w, qk_conv_b,
           b_igate, b_fgate, mix_norm_g, w_out, ln1_g, ln1_b, w_up, ffn_conv_w, ffn_conv_b, w_down,
           ln2_g, ln2_b):
    B, S, _ = x.shape
    assert DEPTH == 1 and w_in.shape[0] == 1
    l = 0

    c_rows = jnp.concatenate([c, c_ctx[None, :], jnp.zeros((8 - B - 1, D_MODEL), F32)], axis=0)
    mod = _modulation(c_rows, w_ada[l], b_ada[l][None, :])
    mod_x = mod[:B].reshape(B, 6, D_MODEL)
    mod_c = jnp.broadcast_to(mod[B].reshape(1, 6, D_MODEL), (B, 6, D_MODEL))

    wi = w_in[l]
    w_uv = wi[:, :2 * HALF_W].astype(BF16)
    w_qk = wi[:, 2 * HALF_W:4 * HALF_W].astype(BF16)
    w_vo = wi[:, 4 * HALF_W:6 * HALF_W].astype(BF16)
    w_g = jnp.pad(wi[:, 6 * HALF_W:], ((0, 0), (0, DH - 2 * N_HD))).astype(BF16)
    gbi = jnp.broadcast_to(b_igate[l].reshape(N_HD, 1), (N_HD, CHUNK))
    gbf = jnp.broadcast_to(b_fgate[l].reshape(N_HD, 1), (N_HD, CHUNK))
    ln_g = gmlp_ln_g[l].reshape(1, HALF_W)
    ws = gmlp_ws[l].astype(BF16)
    bsb = jnp.repeat(gmlp_bs[l].T, DH, axis=1)
    norm_g = mix_norm_g[l][None, :]
    in_wts = (w_uv, w_qk, w_vo, w_g, ln_g, ws, bsb, qk_conv_w[l], qk_conv_b[l][None, :], gbi, gbf,
              norm_g[:, :HALF_W])

    _, q_c, kt_c, v_c, _, *gates_c = _inproj(ctx, mod_c, in_wts, tm=ctx.shape[1])
    c_zero = jnp.zeros((B, N_HD, DH, 2 * DH), F32)
    m_zero = jnp.zeros((B, N_HD, DH), F32)
    _, _, c_state, m_state = _scan(q_c, kt_c, v_c, *gates_c, c_zero, m_zero)

    yan, q, kt, v, o_pre, *gates = _inproj(x, mod_x, in_wts, tm=256)
    hf, hb, _, _ = _scan(q, kt, v, *gates, c_state, m_state)

    wu = w_up[l]
    w_up_b = _interleave_blocks(wu[:, :D_FF], wu[:, D_FF:], 1).astype(BF16)
    cw = _interleave_blocks(ffn_conv_w[l][:, :D_FF], 0.5 * ffn_conv_w[l][:, D_FF:], 1)
    cb = _interleave_blocks(ffn_conv_b[l][None, :D_FF], 0.5 * ffn_conv_b[l][None, D_FF:], 1)
    ffn_wts = (norm_g[:, HALF_W:], w_out[l].astype(BF16), ln1_g[l][None, :], ln1_b[l][None, :],
               w_up_b, cw, cb, w_down[l].astype(BF16), ln2_g[l][None, :], ln2_b[l][None, :])
    return _ffn(x, yan, hf, hb, o_pre, mod_x, ffn_wts, tm=256)
```

```python
import functools

import jax
import jax.numpy as jnp
from jax import lax
from jax.experimental import pallas as pl
from jax.experimental.pallas import tpu as pltpu

D_MODEL = 1024
GRID_W = 64
CHUNK = 128
HEADS = 4
DH = 128
HALF_W = HEADS * DH
N_HD = 2 * HEADS
D_FF = 2688
DEPTH = 1
ALPHA = (2 * DEPTH) ** 0.25
EPS = 1e-5
NSLAB = 8
HALO = 8
FFN_BLOCKS = (768, 768, 768, 384)

F32 = jnp.float32
BF16 = jnp.bfloat16
VMEM_LIMIT = 56 * 1024 * 1024


def _dot(a, b):
    return jnp.dot(a, b, preferred_element_type=F32)


def _norm_lanes(z):
    mu = jnp.mean(z, axis=-1, keepdims=True)
    d = z - mu
    var = jnp.mean(d * d, axis=-1, keepdims=True)
    return d * lax.rsqrt(var + EPS)


def _gelu_tanh(x):
    half = 0.5 * x
    return half + half * jnp.tanh(x * (0.7978845608028654 + 0.035677408136300125 * (x * x)))


def _sigmoid(x):
    return 0.5 * jnp.tanh(0.5 * x) + 0.5


def _log_sigmoid(x):
    return jnp.minimum(x, 0.0) - jnp.log(1.0 + jnp.exp(-jnp.abs(x)))


def _const_spec(shape):
    nd = len(shape)
    return pl.BlockSpec(shape, lambda *_: (0,) * nd, pipeline_mode=pl.Buffered(1))


def _mod_kernel(c_ref, w_ref, b_ref, o_ref):
    cs = c_ref[...]
    a = cs * _sigmoid(cs)
    o_ref[...] = jnp.dot(a, w_ref[...], precision=lax.Precision.HIGHEST,
                         preferred_element_type=F32) + b_ref[...]


def _modulation(c_rows, w_ada, b_ada):
    n_out = w_ada.shape[1]
    bn = 768
    return pl.pallas_call(
        _mod_kernel,
        grid=(n_out // bn,),
        in_specs=[pl.BlockSpec((8, D_MODEL), lambda j: (0, 0)),
                  pl.BlockSpec((D_MODEL, bn), lambda j: (0, j)),
                  pl.BlockSpec((1, bn), lambda j: (0, j))],
        out_specs=pl.BlockSpec((8, bn), lambda j: (0, j)),
        out_shape=jax.ShapeDtypeStruct((8, n_out), F32),
        compiler_params=pltpu.CompilerParams(vmem_limit_bytes=VMEM_LIMIT),
        name="mod",
    )(c_rows, w_ada, b_ada)


def _inproj_kernel(x_ref, xp_ref, xn_ref, mod_ref, wuv_ref, wqk_ref, wvo_ref, wg_ref,
                   lng_ref, ws_ref, bsb_ref, cw_ref, cb_ref, gbi_ref, gbf_ref, ng_ref,
                   yan_ref, q_ref, kt_ref, vm_ref, o_ref, gb_ref, gcm_ref, gct_ref, gwk_ref, *, tm):
    i = pl.program_id(1)
    n_tiles = pl.num_programs(1)
    shift = mod_ref[0:1, :]
    scale = 1.0 + mod_ref[1:2, :]

    def modulate(xv):
        return (xv * scale + shift).astype(BF16)

    hx = modulate(x_ref[...])
    h_prev = jnp.where(i > 0, modulate(xp_ref[...]), jnp.zeros((), BF16))
    h_next = jnp.where(i < n_tiles - 1, modulate(xn_ref[...]), jnp.zeros((), BF16))
    gp = _dot(hx, wg_ref[...])
    uv = _dot(hx, wuv_ref[...])
    ext = _dot(jnp.concatenate([h_prev, hx, h_next], axis=0), wqk_ref[...])
    vo = _dot(hx, wvo_ref[...])

    u = _gelu_tanh(uv[:, :HALF_W])
    v = _gelu_tanh(uv[:, HALF_W:])
    for g in range(HEADS):
        lanes = slice(g * DH, (g + 1) * DH)
        vh = (_norm_lanes(v[:, lanes]) * lng_ref[:, lanes]).astype(BF16)
        for c in range(tm // CHUNK):
            rows = slice(c * CHUNK, (c + 1) * CHUNK)
            mixed = _dot(ws_ref[g], vh[rows]) + bsb_ref[:, lanes]
            ya = u[rows, lanes] * mixed
            yan_ref[rows, lanes] = (_norm_lanes(ya) * ng_ref[:, lanes]).astype(BF16)

    n_ext = tm + 2 * HALO
    pre = ext[HALO:HALO + tm]
    down = pltpu.roll(ext, 1, 0)[HALO:HALO + tm]
    up = pltpu.roll(ext, n_ext - 1, 0)[HALO:HALO + tm]
    conv = down * cw_ref[0:1, :] + pre * cw_ref[1:2, :] + up * cw_ref[2:3, :] + cb_ref[...]
    qk = conv * _sigmoid(conv)
    q_ref[...] = qk[:, :HALF_W].astype(BF16)
    kt_ref[...] = (qk[:, HALF_W:] * (DH ** -0.5)).T.astype(BF16)

    vm_ref[...] = vo[:, :HALF_W].astype(BF16)
    o_ref[...] = vo[:, HALF_W:]

    pos = lax.broadcasted_iota(jnp.int32, (N_HD, CHUNK), 1)
    is_fwd = lax.broadcasted_iota(jnp.int32, (N_HD, CHUNK), 0) < HEADS
    pad = jnp.zeros((DH - N_HD, CHUNK), F32)

    def scan_lanes(z, op, fill, reverse):
        step = 1
        while step < CHUNK:
            if reverse:
                moved = jnp.where(pos < CHUNK - step, pltpu.roll(z, CHUNK - step, 1), fill)
            else:
                moved = jnp.where(pos >= step, pltpu.roll(z, step, 1), fill)
            z = op(z, moved)
            step *= 2
        return z

    for c in range(tm // CHUNK):
        rows = slice(c * CHUNK, (c + 1) * CHUNK)
        gt = gp[rows].T
        log_i = gt[0:N_HD] + gbi_ref[...]
        lf = _log_sigmoid(gt[N_HD:2 * N_HD] + gbf_ref[...])
        csum = scan_lanes(lf, jnp.add, 0.0, False)
        suffix = csum[:, CHUNK - 1:CHUNK] - csum + lf
        b = jnp.where(is_fwd, csum, suffix)
        cdiff = b - log_i
        cmin = jnp.where(is_fwd, scan_lanes(cdiff, jnp.minimum, jnp.inf, False),
                         scan_lanes(cdiff, jnp.minimum, jnp.inf, True))
        cmin_end = jnp.where(is_fwd[:, 0:1], cmin[:, CHUNK - 1:CHUNK], cmin[:, 0:1])
        gct_ref[:, rows] = cdiff
        gwk_ref[:, rows] = jnp.exp(cmin_end - cdiff)
        gb_ref[rows, :] = jnp.concatenate([b, pad], axis=0).T
        gcm_ref[rows, :] = jnp.concatenate([cmin, pad], axis=0).T


def _inproj(xs, mod6, wts, *, tm):
    B, S, _ = xs.shape
    nt = S // tm
    hb = tm // HALO
    n_halo_blocks = S // HALO
    kern = functools.partial(_inproj_kernel, tm=tm)
    tok = lambda w: pl.BlockSpec((None, tm, w), lambda b, i: (b, i, 0))
    in_specs = [
        tok(D_MODEL),
        pl.BlockSpec((None, HALO, D_MODEL), lambda b, i: (b, jnp.maximum(i * hb - 1, 0), 0)),
        pl.BlockSpec((None, HALO, D_MODEL), lambda b, i: (b, jnp.minimum((i + 1) * hb, n_halo_blocks - 1), 0)),
        pl.BlockSpec((None, 6, D_MODEL), lambda b, i: (b, 0, 0)),
    ] + [_const_spec(w.shape) for w in wts]
    out_shape = [
        jax.ShapeDtypeStruct((B, S, HALF_W), BF16),
        jax.ShapeDtypeStruct((B, S, HALF_W), BF16),
        jax.ShapeDtypeStruct((B, HALF_W, S), BF16),
        jax.ShapeDtypeStruct((B, S, HALF_W), BF16),
        jax.ShapeDtypeStruct((B, S, HALF_W), F32),
        jax.ShapeDtypeStruct((B, S, DH), F32),
        jax.ShapeDtypeStruct((B, S, DH), F32),
        jax.ShapeDtypeStruct((B, N_HD, S), F32),
        jax.ShapeDtypeStruct((B, N_HD, S), F32),
    ]
    out_specs = [
        tok(HALF_W), tok(HALF_W),
        pl.BlockSpec((None, HALF_W, tm), lambda b, i: (b, 0, i)),
        tok(HALF_W), tok(HALF_W), tok(DH), tok(DH),
        pl.BlockSpec((None, N_HD, tm), lambda b, i: (b, 0, i)),
        pl.BlockSpec((None, N_HD, tm), lambda b, i: (b, 0, i)),
    ]
    return pl.pallas_call(
        kern, grid=(B, nt), in_specs=in_specs, out_specs=out_specs, out_shape=out_shape,
        compiler_params=pltpu.CompilerParams(
            dimension_semantics=("arbitrary", "arbitrary"), vmem_limit_bytes=VMEM_LIMIT),
        name="inproj",
    )(xs, xs, xs, mod6, *wts)


def _scan_kernel(qf_ref, ktf_ref, vf_ref, gbf_ref, gcmf_ref, gctf_ref, gwkf_ref,
                 qb_ref, ktb_ref, vb_ref, gbb_ref, gcmb_ref, gctb_ref, gwkb_ref,
                 *rest, zero_init):
    if zero_init:
        hf_ref, hb_ref, cst_ref, mst_ref = rest
    else:
        c0_ref, m0_ref, hf_ref, hb_ref, cst_ref, mst_ref = rest

    @pl.when(pl.program_id(1) == 0)
    def _():
        if zero_init:
            cst_ref[...] = jnp.zeros(cst_ref.shape, F32)
            mst_ref[...] = jnp.zeros(mst_ref.shape, F32)
        else:
            cst_ref[...] = c0_ref[...]
            mst_ref[...] = m0_ref[...]

    t_idx = lax.broadcasted_iota(jnp.int32, (CHUNK, CHUNK), 0)
    s_idx = lax.broadcasted_iota(jnp.int32, (CHUNK, CHUNK), 1)
    is_fwd = lax.broadcasted_iota(jnp.int32, (CHUNK, DH), 1) < HEADS
    last = CHUNK - 1

    m_row = mst_ref[0:1, :]
    b = jnp.where(is_fwd, gbf_ref[...], gbb_ref[...])
    cmin = jnp.where(is_fwd, gcmf_ref[...], gcmb_ref[...])
    total = jnp.where(is_fwd[0:1], gbf_ref[last:CHUNK, :], gbb_ref[0:1, :])
    cmin_end = jnp.where(is_fwd[0:1], gcmf_ref[last:CHUNK, :], gcmb_ref[0:1, :])
    rmax = b - cmin
    inter = b + m_row
    m_t = jnp.maximum(inter, rmax)
    w_intra = jnp.exp(rmax - m_t)
    w_state = jnp.exp(inter - m_t)
    floor = jnp.exp(-m_t)
    gmax = total - cmin_end
    m_new = jnp.maximum(total + m_row, gmax)
    w_old = jnp.exp(total + m_row - m_new)
    w_new = jnp.exp(gmax - m_new)
    mst_ref[...] = jnp.broadcast_to(m_new, (N_HD, DH))

    ones = jnp.ones((CHUNK, DH), BF16)
    pairs = []
    for j in range(N_HD):
        fwd = j < HEADS
        lanes = slice((j % HEADS) * DH, (j % HEADS + 1) * DH)
        refs = ((qf_ref, ktf_ref, vf_ref, gctf_ref, gwkf_ref, hf_ref) if fwd else
                (qb_ref, ktb_ref, vb_ref, gctb_ref, gwkb_ref, hb_ref))
        pairs.append((fwd, lanes) + refs)

    scores, from_state, updates, v_augs = [], [], [], []
    for j, (fwd, lanes, q_ref, kt_ref, v_ref, gct_ref, gwk_ref, out_ref) in enumerate(pairs):
        scores.append(_dot(q_ref[:, lanes], kt_ref[lanes, :]))
    for j, (fwd, lanes, q_ref, kt_ref, v_ref, gct_ref, gwk_ref, out_ref) in enumerate(pairs):
        from_state.append(_dot(q_ref[:, lanes], cst_ref[j].astype(BF16)))
    for j, (fwd, lanes, q_ref, kt_ref, v_ref, gct_ref, gwk_ref, out_ref) in enumerate(pairs):
        v_aug = jnp.concatenate([v_ref[:, lanes], ones], axis=1)
        v_augs.append(v_aug)
        kt_w = (kt_ref[lanes, :].astype(F32) * gwk_ref[j:j + 1, :]).astype(BF16)
        updates.append(_dot(kt_w, v_aug))
    for j, (fwd, lanes, q_ref, kt_ref, v_ref, gct_ref, gwk_ref, out_ref) in enumerate(pairs):
        keep = (s_idx <= t_idx) if fwd else (s_idx >= t_idx)
        decay = jnp.exp(jnp.where(keep, cmin[:, j:j + 1] - gct_ref[j:j + 1, :], -jnp.inf))
        sv = _dot((scores[j] * decay).astype(BF16), v_augs[j])
        wi = jnp.broadcast_to(w_intra[:, j:j + 1], (CHUNK, DH))
        ws = jnp.broadcast_to(w_state[:, j:j + 1], (CHUNK, DH))
        num = wi * sv[:, :DH] + ws * from_state[j][:, :DH]
        den = wi * sv[:, DH:] + ws * from_state[j][:, DH:]
        out_ref[:, lanes] = num / jnp.maximum(jnp.abs(den), floor[:, j:j + 1])
        cst_ref[j] = w_old[:, j:j + 1] * cst_ref[j] + w_new[:, j:j + 1] * updates[j]


def _scan(q, kt, v, gb, gcm, gct, gwk, init=None):
    B, S, _ = q.shape
    nc = S // CHUNK
    fwd_tok = lambda w: pl.BlockSpec((None, CHUNK, w), lambda b, i: (b, i, 0))
    bwd_tok = lambda w: pl.BlockSpec((None, CHUNK, w), lambda b, i: (b, nc - 1 - i, 0))
    fwd_t = lambda r: pl.BlockSpec((None, r, CHUNK), lambda b, i: (b, 0, i))
    bwd_t = lambda r: pl.BlockSpec((None, r, CHUNK), lambda b, i: (b, 0, nc - 1 - i))
    c_spec = pl.BlockSpec((None, N_HD, DH, 2 * DH), lambda b, i: (b, 0, 0, 0))
    m_spec = pl.BlockSpec((None, N_HD, DH), lambda b, i: (b, 0, 0))
    init = () if init is None else tuple(init)
    return pl.pallas_call(
        functools.partial(_scan_kernel, zero_init=not init), grid=(B, nc),
        in_specs=[fwd_tok(HALF_W), fwd_t(HALF_W), fwd_tok(HALF_W), fwd_tok(DH), fwd_tok(DH), fwd_t(N_HD), fwd_t(N_HD),
                  bwd_tok(HALF_W), bwd_t(HALF_W), bwd_tok(HALF_W), bwd_tok(DH), bwd_tok(DH), bwd_t(N_HD), bwd_t(N_HD),
                  ] + [c_spec, m_spec][:len(init)],
        out_specs=[fwd_tok(HALF_W), bwd_tok(HALF_W), c_spec, m_spec],
        out_shape=[jax.ShapeDtypeStruct((B, S, HALF_W), F32),
                   jax.ShapeDtypeStruct((B, S, HALF_W), F32),
                   jax.ShapeDtypeStruct((B, N_HD, DH, 2 * DH), F32),
                   jax.ShapeDtypeStruct((B, N_HD, DH), F32)],
        compiler_params=pltpu.CompilerParams(
            dimension_semantics=("arbitrary", "arbitrary"), vmem_limit_bytes=VMEM_LIMIT),
        name="scan",
    )(q, kt, v, gb, gcm, gct, gwk, q, kt, v, gb, gcm, gct, gwk, *init)


def _layer_norm(z, g_ref, b_ref):
    return _norm_lanes(z) * g_ref[...] + b_ref[...]


def _ffn_kernel(x_ref, yan_ref, hf_ref, hb_ref, o_ref, mod_ref, ng_ref, wout_ref,
                ln1g_ref, ln1b_ref, wup_ref, cw_ref, cb_ref, wdown_ref, ln2g_ref, ln2b_ref,
                out_ref, perm_ref, *, tm, n_sub):
    ts = tm // n_sub
    ns = ts // NSLAB
    n_lt = D_MODEL // DH
    starts = [sum(FFN_BLOCKS[:k]) for k in range(len(FFN_BLOCKS))]
    subs = [slice(t * ts, (t + 1) * ts) for t in range(n_sub)]

    def mixer_in(rows):
        h = hf_ref[rows, :] + hb_ref[rows, :]
        gated = []
        for g in range(HEADS):
            lanes = slice(g * DH, (g + 1) * DH)
            gated.append((_norm_lanes(h[:, lanes]) * ng_ref[:, lanes] * _sigmoid(o_ref[rows, lanes])).astype(BF16))
        return jnp.concatenate([yan_ref[rows, :]] + gated, axis=1)

    def permuted_x1(t, mix):
        x1n = _layer_norm(ALPHA * x_ref[subs[t], :] + mod_ref[2:3, :] * mix, ln1g_ref, ln1b_ref)
        for c in range(n_lt):
            perm_ref[t * n_lt + c] = x1n[:, c * DH:(c + 1) * DH]
        return jnp.concatenate(
            [jnp.concatenate([perm_ref[t * n_lt + c, pl.ds(j, ns, stride=NSLAB), :] for c in range(n_lt)], axis=1)
             for j in range(NSLAB)], axis=0)

    def up(h2, k):
        return _dot(h2, wup_ref[:, 2 * starts[k]:2 * (starts[k] + FFN_BLOCKS[k])])

    def hidden(a, k):
        blk = FFN_BLOCKS[k]
        cols = slice(2 * starts[k], 2 * (starts[k] + blk))
        q8 = lax.broadcasted_iota(jnp.int32, (ns, 2 * blk), 0) % NSLAB
        before_first = jnp.where(q8 == 0, 0.0, pltpu.roll(a[ts - ns:], 1, 0))
        after_last = jnp.where(q8 == NSLAB - 1, 0.0, pltpu.roll(a[:ns], ns - 1, 0))
        left = jnp.concatenate([before_first, a[:ts - ns]], axis=0)
        right = jnp.concatenate([a[ns:], after_last], axis=0)
        a = left * cw_ref[0:1, cols] + a * cw_ref[1:2, cols] + right * cw_ref[2:3, cols] + cb_ref[:, cols]
        prod = a[:, blk:] * a[:, :blk]
        return (prod + prod * jnp.tanh(a[:, blk:])).astype(BF16)

    def finish(t, x1, acc):
        out = _layer_norm(ALPHA * x1 + mod_ref[5:6, :] * acc, ln2g_ref, ln2b_ref)
        for c in range(n_lt):
            for j in range(NSLAB):
                perm_ref[t * n_lt + c, pl.ds(j, ns, stride=NSLAB), :] = out[j * ns:(j + 1) * ns, c * DH:(c + 1) * DH]
        for c in range(n_lt):
            out_ref[subs[t], c * DH:(c + 1) * DH] = perm_ref[t * n_lt + c]

    mixes = [_dot(mixer_in(rows), wout_ref[...]) for rows in subs]
    x1s = [permuted_x1(t, mixes[t]) for t in range(n_sub)]
    h2s = [(x1 * (1.0 + mod_ref[4:5, :]) + mod_ref[3:4, :]).astype(BF16) for x1 in x1s]

    steps = [(t, k) for t in range(n_sub) for k in range(len(FFN_BLOCKS))]
    accs = [None] * n_sub
    pending = up(h2s[0], 0)
    for idx, (t, k) in enumerate(steps):
        following = up(h2s[steps[idx + 1][0]], steps[idx + 1][1]) if idx + 1 < len(steps) else None
        part = _dot(hidden(pending, k), wdown_ref[starts[k]:starts[k] + FFN_BLOCKS[k], :])
        accs[t] = part if accs[t] is None else accs[t] + part
        pending = following
        if k == len(FFN_BLOCKS) - 1:
            finish(t, x1s[t], accs[t])


def _ffn(x, yan, hf, hb, o_pre, mod6, wts, *, tm, n_sub):
    B, S, _ = x.shape
    kern = functools.partial(_ffn_kernel, tm=tm, n_sub=n_sub)
    tok = lambda w: pl.BlockSpec((None, tm, w), lambda b, i: (b, i, 0))
    in_specs = [tok(D_MODEL), tok(HALF_W), tok(HALF_W), tok(HALF_W), tok(HALF_W),
                pl.BlockSpec((None, 6, D_MODEL), lambda b, i: (b, 0, 0))] + [_const_spec(w.shape) for w in wts]
    return pl.pallas_call(
        kern, grid=(B, S // tm), in_specs=in_specs, out_specs=tok(D_MODEL),
        out_shape=jax.ShapeDtypeStruct((B, S, D_MODEL), F32),
        scratch_shapes=[pltpu.VMEM((n_sub * (D_MODEL // DH), tm // n_sub, DH), F32)],
        compiler_params=pltpu.CompilerParams(
            dimension_semantics=("arbitrary", "arbitrary"), vmem_limit_bytes=VMEM_LIMIT),
        name="ffn",
    )(x, yan, hf, hb, o_pre, mod6, *wts)


def _interleave_blocks(val, gate, axis):
    parts, start = [], 0
    for blk in FFN_BLOCKS:
        parts.append(lax.slice_in_dim(val, start, start + blk, axis=axis))
        parts.append(lax.slice_in_dim(gate, start, start + blk, axis=axis))
        start += blk
    return jnp.concatenate(parts, axis=axis)


def kernel(x, c, ctx, c_ctx, w_ada, b_ada, w_in, gmlp_ln_g, gmlp_ws, gmlp_bs, qk_conv_w, qk_conv_b,
           b_igate, b_fgate, mix_norm_g, w_out, ln1_g, ln1_b, w_up, ffn_conv_w, ffn_conv_b, w_down,
           ln2_g, ln2_b):
    B, S, _ = x.shape
    assert DEPTH == 1 and w_in.shape[0] == 1
    l = 0

    c_rows = jnp.concatenate([c, c_ctx[None, :], jnp.zeros((8 - B - 1, D_MODEL), F32)], axis=0)
    mod = _modulation(c_rows, w_ada[l], b_ada[l][None, :])
    mod_x = mod[:B].reshape(B, 6, D_MODEL)
    mod_c = jnp.broadcast_to(mod[B].reshape(1, 6, D_MODEL), (B, 6, D_MODEL))

    wi = w_in[l]
    w_uv = wi[:, :2 * HALF_W].astype(BF16)
    w_qk = wi[:, 2 * HALF_W:4 * HALF_W].astype(BF16)
    w_vo = wi[:, 4 * HALF_W:6 * HALF_W].astype(BF16)
    w_g = jnp.pad(wi[:, 6 * HALF_W:], ((0, 0), (0, DH - 2 * N_HD))).astype(BF16)
    gbi = jnp.broadcast_to(b_igate[l].reshape(N_HD, 1), (N_HD, CHUNK))
    gbf = jnp.broadcast_to(b_fgate[l].reshape(N_HD, 1), (N_HD, CHUNK))
    ln_g = gmlp_ln_g[l].reshape(1, HALF_W)
    ws = gmlp_ws[l].astype(BF16)
    bsb = jnp.repeat(gmlp_bs[l].T, DH, axis=1)
    norm_g = mix_norm_g[l][None, :]
    in_wts = (w_uv, w_qk, w_vo, w_g, ln_g, ws, bsb, qk_conv_w[l], qk_conv_b[l][None, :], gbi, gbf,
              norm_g[:, :HALF_W])

    _, q_c, kt_c, v_c, _, *gates_c = _inproj(ctx, mod_c, in_wts, tm=ctx.shape[1])
    _, _, c_state, m_state = _scan(q_c, kt_c, v_c, *gates_c)

    yan, q, kt, v, o_pre, *gates = _inproj(x, mod_x, in_wts, tm=512)
    hf, hb, _, _ = _scan(q, kt, v, *gates, init=(c_state, m_state))

    wu = w_up[l]
    w_up_b = _interleave_blocks(wu[:, :D_FF], wu[:, D_FF:], 1).astype(BF16)
    cw = _interleave_blocks(ffn_conv_w[l][:, :D_FF], 0.5 * ffn_conv_w[l][:, D_FF:], 1)
    cb = _interleave_blocks(ffn_conv_b[l][None, :D_FF], 0.5 * ffn_conv_b[l][None, D_FF:], 1)
    ffn_wts = (norm_g[:, HALF_W:], w_out[l].astype(BF16), ln1_g[l][None, :], ln1_b[l][None, :],
               w_up_b, cw, cb, w_down[l].astype(BF16), ln2_g[l][None, :], ln2_b[l][None, :])
    return _ffn(x, yan, hf, hb, o_pre, mod_x, ffn_wts, tm=512, n_sub=2)
```

```python
import functools

import jax
import jax.numpy as jnp
from jax import lax
from jax.experimental import pallas as pl
from jax.experimental.pallas import tpu as pltpu

D_MODEL = 1024
GRID_W = 64
CHUNK = 128
HEADS = 4
DH = 128
HALF_W = HEADS * DH
N_HD = 2 * HEADS
D_FF = 2688
DEPTH = 1
ALPHA = (2 * DEPTH) ** 0.25
EPS = 1e-5
NSLAB = 8
HALO = 8
FFN_BLOCKS = (768, 768, 768, 384)

F32 = jnp.float32
BF16 = jnp.bfloat16
VMEM_LIMIT = 56 * 1024 * 1024


def _dot(a, b):
    return jnp.dot(a, b, preferred_element_type=F32)


def _norm_lanes(z):
    mu = jnp.mean(z, axis=-1, keepdims=True)
    d = z - mu
    var = jnp.mean(d * d, axis=-1, keepdims=True)
    return d * lax.rsqrt(var + EPS)


def _gelu_tanh(x):
    half = 0.5 * x
    return half + half * jnp.tanh(x * (0.7978845608028654 + 0.035677408136300125 * (x * x)))


def _sigmoid(x):
    return 0.5 * jnp.tanh(0.5 * x) + 0.5


def _log_sigmoid(x):
    return jnp.minimum(x, 0.0) - jnp.log(1.0 + jnp.exp(-jnp.abs(x)))


def _const_spec(shape):
    nd = len(shape)
    return pl.BlockSpec(shape, lambda *_: (0,) * nd, pipeline_mode=pl.Buffered(1))


def _mod_kernel(c_ref, w_ref, b_ref, o_ref):
    cs = c_ref[...]
    a = cs * _sigmoid(cs)
    a_hi = a.astype(BF16)
    a_lo = (a - a_hi.astype(F32)).astype(BF16)
    w = w_ref[...]
    w_hi = w.astype(BF16)
    w_lo = (w - w_hi.astype(F32)).astype(BF16)
    by_hi = _dot(jnp.concatenate([a_hi, a_lo], axis=0), w_hi)
    rows = a.shape[0]
    o_ref[...] = by_hi[:rows] + by_hi[rows:] + _dot(a_hi, w_lo) + b_ref[...]


def _modulation(c_rows, w_ada, b_ada):
    n_out = w_ada.shape[1]
    bn = 768
    return pl.pallas_call(
        _mod_kernel,
        grid=(n_out // bn,),
        in_specs=[pl.BlockSpec((8, D_MODEL), lambda j: (0, 0)),
                  pl.BlockSpec((D_MODEL, bn), lambda j: (0, j)),
                  pl.BlockSpec((1, bn), lambda j: (0, j))],
        out_specs=pl.BlockSpec((8, bn), lambda j: (0, j)),
        out_shape=jax.ShapeDtypeStruct((8, n_out), F32),
        compiler_params=pltpu.CompilerParams(vmem_limit_bytes=VMEM_LIMIT),
        name="mod",
    )(c_rows, w_ada, b_ada)


def _inproj_kernel(x_ref, xp_ref, xn_ref, mod_ref, win_ref,
                   lng_ref, ws_ref, bsb_ref, cw_ref, cb_ref, gbi_ref, gbf_ref, ng_ref,
                   yan_ref, q_ref, kt_ref, vm_ref, o_ref, gb_ref, gcm_ref, gct_ref, gwk_ref, *, tm):
    i = pl.program_id(1)
    n_tiles = pl.num_programs(1)
    shift = mod_ref[0:1, :]
    scale = 1.0 + mod_ref[1:2, :]

    def modulate(xv):
        return (xv * scale + shift).astype(BF16)

    hx = modulate(x_ref[...])
    h_prev = jnp.where(i > 0, modulate(xp_ref[...]), jnp.zeros((), BF16))
    h_next = jnp.where(i < n_tiles - 1, modulate(xn_ref[...]), jnp.zeros((), BF16))
    gp = _dot(hx, win_ref[:, 6 * HALF_W:])
    uv = _dot(hx, win_ref[:, 0:2 * HALF_W])
    ext = _dot(jnp.concatenate([h_prev, hx, h_next], axis=0), win_ref[:, 2 * HALF_W:4 * HALF_W])
    vo = _dot(hx, win_ref[:, 4 * HALF_W:6 * HALF_W])

    u = _gelu_tanh(uv[:, :HALF_W])
    v = _gelu_tanh(uv[:, HALF_W:])
    for g in range(HEADS):
        lanes = slice(g * DH, (g + 1) * DH)
        vh = (_norm_lanes(v[:, lanes]) * lng_ref[:, lanes]).astype(BF16)
        for c in range(tm // CHUNK):
            rows = slice(c * CHUNK, (c + 1) * CHUNK)
            mixed = _dot(ws_ref[g], vh[rows]) + bsb_ref[:, lanes]
            ya = u[rows, lanes] * mixed
            yan_ref[rows, lanes] = (_norm_lanes(ya) * ng_ref[:, lanes]).astype(BF16)

    n_ext = tm + 2 * HALO
    pre = ext[HALO:HALO + tm]
    down = pltpu.roll(ext, 1, 0)[HALO:HALO + tm]
    up = pltpu.roll(ext, n_ext - 1, 0)[HALO:HALO + tm]
    conv = down * cw_ref[0:1, :] + pre * cw_ref[1:2, :] + up * cw_ref[2:3, :] + cb_ref[...]
    qk = conv * _sigmoid(conv)
    q_ref[...] = qk[:, :HALF_W].astype(BF16)
    kt_ref[...] = (qk[:, HALF_W:] * (DH ** -0.5)).T.astype(BF16)

    vm_ref[...] = vo[:, :HALF_W].astype(BF16)
    o_ref[...] = vo[:, HALF_W:]

    pos = lax.broadcasted_iota(jnp.int32, (N_HD, CHUNK), 1)
    is_fwd = lax.broadcasted_iota(jnp.int32, (N_HD, CHUNK), 0) < HEADS
    pad = jnp.zeros((DH - N_HD, CHUNK), F32)

    def scan_lanes(z, op, fill, reverse):
        step = 1
        while step < CHUNK:
            if reverse:
                moved = jnp.where(pos < CHUNK - step, pltpu.roll(z, CHUNK - step, 1), fill)
            else:
                moved = jnp.where(pos >= step, pltpu.roll(z, step, 1), fill)
            z = op(z, moved)
            step *= 2
        return z

    for c in range(tm // CHUNK):
        rows = slice(c * CHUNK, (c + 1) * CHUNK)
        gt = gp[rows].T
        log_i = gt[0:N_HD] + gbi_ref[...]
        lf = _log_sigmoid(gt[N_HD:2 * N_HD] + gbf_ref[...])
        csum = scan_lanes(lf, jnp.add, 0.0, False)
        suffix = csum[:, CHUNK - 1:CHUNK] - csum + lf
        b = jnp.where(is_fwd, csum, suffix)
        cdiff = b - log_i
        cmin = jnp.where(is_fwd, scan_lanes(cdiff, jnp.minimum, jnp.inf, False),
                         scan_lanes(cdiff, jnp.minimum, jnp.inf, True))
        cmin_end = jnp.where(is_fwd[:, 0:1], cmin[:, CHUNK - 1:CHUNK], cmin[:, 0:1])
        gct_ref[:, rows] = cdiff
        gwk_ref[:, rows] = jnp.exp(cmin_end - cdiff)
        gb_ref[rows, :] = jnp.concatenate([b, pad], axis=0).T
        gcm_ref[rows, :] = jnp.concatenate([cmin, pad], axis=0).T


def _inproj(xs, mod6, wts, *, tm):
    B, S, _ = xs.shape
    nt = S // tm
    hb = tm // HALO
    n_halo_blocks = S // HALO
    kern = functools.partial(_inproj_kernel, tm=tm)
    tok = lambda w: pl.BlockSpec((None, tm, w), lambda b, i: (b, i, 0))
    in_specs = [
        tok(D_MODEL),
        pl.BlockSpec((None, HALO, D_MODEL), lambda b, i: (b, jnp.maximum(i * hb - 1, 0), 0)),
        pl.BlockSpec((None, HALO, D_MODEL), lambda b, i: (b, jnp.minimum((i + 1) * hb, n_halo_blocks - 1), 0)),
        pl.BlockSpec((None, 6, D_MODEL), lambda b, i: (b, 0, 0)),
    ] + [_const_spec(w.shape) for w in wts]
    out_shape = [
        jax.ShapeDtypeStruct((B, S, HALF_W), BF16),
        jax.ShapeDtypeStruct((B, S, HALF_W), BF16),
        jax.ShapeDtypeStruct((B, HALF_W, S), BF16),
        jax.ShapeDtypeStruct((B, S, HALF_W), BF16),
        jax.ShapeDtypeStruct((B, S, HALF_W), F32),
        jax.ShapeDtypeStruct((B, S, DH), F32),
        jax.ShapeDtypeStruct((B, S, DH), F32),
        jax.ShapeDtypeStruct((B, N_HD, S), F32),
        jax.ShapeDtypeStruct((B, N_HD, S), F32),
    ]
    out_specs = [
        tok(HALF_W), tok(HALF_W),
        pl.BlockSpec((None, HALF_W, tm), lambda b, i: (b, 0, i)),
        tok(HALF_W), tok(HALF_W), tok(DH), tok(DH),
        pl.BlockSpec((None, N_HD, tm), lambda b, i: (b, 0, i)),
        pl.BlockSpec((None, N_HD, tm), lambda b, i: (b, 0, i)),
    ]
    return pl.pallas_call(
        kern, grid=(B, nt), in_specs=in_specs, out_specs=out_specs, out_shape=out_shape,
        compiler_params=pltpu.CompilerParams(
            dimension_semantics=("arbitrary", "arbitrary"), vmem_limit_bytes=VMEM_LIMIT),
        name="inproj",
    )(xs, xs, xs, mod6, *wts)


def _scan_kernel(qf_ref, ktf_ref, vf_ref, gbf_ref, gcmf_ref, gctf_ref, gwkf_ref,
                 qb_ref, ktb_ref, vb_ref, gbb_ref, gcmb_ref, gctb_ref, gwkb_ref,
                 *rest, zero_init, cps):
    if zero_init:
        hf_ref, hb_ref, cst_ref, mst_ref = rest
    else:
        c0_ref, m0_ref, hf_ref, hb_ref, cst_ref, mst_ref = rest

    @pl.when(pl.program_id(1) == 0)
    def _():
        if zero_init:
            cst_ref[...] = jnp.zeros(cst_ref.shape, F32)
            mst_ref[...] = jnp.zeros(mst_ref.shape, F32)
        else:
            cst_ref[...] = c0_ref[...]
            mst_ref[...] = m0_ref[...]

    t_idx = lax.broadcasted_iota(jnp.int32, (CHUNK, CHUNK), 0)
    s_idx = lax.broadcasted_iota(jnp.int32, (CHUNK, CHUNK), 1)
    is_fwd = lax.broadcasted_iota(jnp.int32, (CHUNK, DH), 1) < HEADS
    ones = jnp.ones((CHUNK, DH), BF16)
    f_rows = [slice(k * CHUNK, (k + 1) * CHUNK) for k in range(cps)]
    b_rows = f_rows[::-1]

    terms = []
    m_row = mst_ref[0:1, :]
    for fr, br in zip(f_rows, b_rows):
        b = jnp.where(is_fwd, gbf_ref[fr, :], gbb_ref[br, :])
        cmin = jnp.where(is_fwd, gcmf_ref[fr, :], gcmb_ref[br, :])
        total = jnp.where(is_fwd[0:1], gbf_ref[fr.stop - 1:fr.stop, :], gbb_ref[br.start:br.start + 1, :])
        cmin_end = jnp.where(is_fwd[0:1], gcmf_ref[fr.stop - 1:fr.stop, :], gcmb_ref[br.start:br.start + 1, :])
        rmax = b - cmin
        inter = b + m_row
        m_t = jnp.maximum(inter, rmax)
        gmax = total - cmin_end
        m_new = jnp.maximum(total + m_row, gmax)
        terms.append(dict(cmin=cmin, w_intra=jnp.exp(rmax - m_t), w_state=jnp.exp(inter - m_t),
                          floor=jnp.exp(-m_t), w_old=jnp.exp(total + m_row - m_new), w_new=jnp.exp(gmax - m_new)))
        m_row = m_new
    mst_ref[...] = jnp.broadcast_to(m_row, (N_HD, DH))

    pairs = []
    for j in range(N_HD):
        fwd = j < HEADS
        lanes = slice((j % HEADS) * DH, (j % HEADS + 1) * DH)
        refs = ((qf_ref, ktf_ref, vf_ref, gctf_ref, gwkf_ref, hf_ref) if fwd else
                (qb_ref, ktb_ref, vb_ref, gctb_ref, gwkb_ref, hb_ref))
        pairs.append((fwd, lanes, f_rows if fwd else b_rows) + refs)

    scores, updates, v_augs = [], [], []
    for k in range(cps):
        scores.append([_dot(q_ref[rows[k], lanes], kt_ref[lanes, rows[k]])
                       for (fwd, lanes, rows, q_ref, kt_ref, v_ref, gct_ref, gwk_ref, out_ref) in pairs])
    for k in range(cps):
        v_augs.append([jnp.concatenate([v_ref[rows[k], lanes], ones], axis=1)
                       for (fwd, lanes, rows, q_ref, kt_ref, v_ref, gct_ref, gwk_ref, out_ref) in pairs])
        updates.append([_dot((kt_ref[lanes, rows[k]].astype(F32) * gwk_ref[j:j + 1, rows[k]]).astype(BF16), v_augs[k][j])
                        for j, (fwd, lanes, rows, q_ref, kt_ref, v_ref, gct_ref, gwk_ref, out_ref) in enumerate(pairs)])
    states = [cst_ref[j] for j in range(N_HD)]
    for k in range(cps):
        tk = terms[k]
        from_state = [_dot(q_ref[rows[k], lanes], states[j].astype(BF16))
                      for j, (fwd, lanes, rows, q_ref, kt_ref, v_ref, gct_ref, gwk_ref, out_ref) in enumerate(pairs)]
        states = [tk["w_old"][:, j:j + 1] * states[j] + tk["w_new"][:, j:j + 1] * updates[k][j] for j in range(N_HD)]
        for j, (fwd, lanes, rows, q_ref, kt_ref, v_ref, gct_ref, gwk_ref, out_ref) in enumerate(pairs):
            keep = (s_idx <= t_idx) if fwd else (s_idx >= t_idx)
            decay = jnp.exp(jnp.where(keep, tk["cmin"][:, j:j + 1] - gct_ref[j:j + 1, rows[k]], -jnp.inf))
            sv = _dot((scores[k][j] * decay).astype(BF16), v_augs[k][j])
            wi = jnp.broadcast_to(tk["w_intra"][:, j:j + 1], (CHUNK, DH))
            ws = jnp.broadcast_to(tk["w_state"][:, j:j + 1], (CHUNK, DH))
            num = wi * sv[:, :DH] + ws * from_state[j][:, :DH]
            den = wi * sv[:, DH:] + ws * from_state[j][:, DH:]
            out_ref[rows[k], lanes] = num / jnp.maximum(jnp.abs(den), tk["floor"][:, j:j + 1])
    for j in range(N_HD):
        cst_ref[j] = states[j]


def _scan(q, kt, v, gb, gcm, gct, gwk, init=None):
    B, S, _ = q.shape
    cps = 2
    rows = cps * CHUNK
    nb = S // rows
    fwd_tok = lambda w: pl.BlockSpec((None, rows, w), lambda b, i: (b, i, 0))
    bwd_tok = lambda w: pl.BlockSpec((None, rows, w), lambda b, i: (b, nb - 1 - i, 0))
    fwd_t = lambda r: pl.BlockSpec((None, r, rows), lambda b, i: (b, 0, i))
    bwd_t = lambda r: pl.BlockSpec((None, r, rows), lambda b, i: (b, 0, nb - 1 - i))
    c_spec = pl.BlockSpec((None, N_HD, DH, 2 * DH), lambda b, i: (b, 0, 0, 0))
    m_spec = pl.BlockSpec((None, N_HD, DH), lambda b, i: (b, 0, 0))
    init = () if init is None else tuple(init)
    return pl.pallas_call(
        functools.partial(_scan_kernel, zero_init=not init, cps=cps), grid=(B, nb),
        in_specs=[fwd_tok(HALF_W), fwd_t(HALF_W), fwd_tok(HALF_W), fwd_tok(DH), fwd_tok(DH), fwd_t(N_HD), fwd_t(N_HD),
                  bwd_tok(HALF_W), bwd_t(HALF_W), bwd_tok(HALF_W), bwd_tok(DH), bwd_tok(DH), bwd_t(N_HD), bwd_t(N_HD),
                  ] + [c_spec, m_spec][:len(init)],
        out_specs=[fwd_tok(HALF_W), bwd_tok(HALF_W), c_spec, m_spec],
        out_shape=[jax.ShapeDtypeStruct((B, S, HALF_W), F32),
                   jax.ShapeDtypeStruct((B, S, HALF_W), F32),
                   jax.ShapeDtypeStruct((B, N_HD, DH, 2 * DH), F32),
                   jax.ShapeDtypeStruct((B, N_HD, DH), F32)],
        compiler_params=pltpu.CompilerParams(
            dimension_semantics=("arbitrary", "arbitrary"), vmem_limit_bytes=VMEM_LIMIT),
        name="scan",
    )(q, kt, v, gb, gcm, gct, gwk, q, kt, v, gb, gcm, gct, gwk, *init)


def _layer_norm(z, g_ref, b_ref):
    return _norm_lanes(z) * g_ref[...] + b_ref[...]


def _ffn_kernel(x_ref, yan_ref, hf_ref, hb_ref, o_ref, mod_ref, ng_ref, wout_ref,
                ln1g_ref, ln1b_ref, wup_ref, cw_ref, cb_ref, wdown_ref, ln2g_ref, ln2b_ref,
                out_ref, perm_ref, *, tm, n_sub):
    ts = tm // n_sub
    ns = ts // NSLAB
    n_lt = D_MODEL // DH
    starts = [sum(FFN_BLOCKS[:k]) for k in range(len(FFN_BLOCKS))]
    subs = [slice(t * ts, (t + 1) * ts) for t in range(n_sub)]

    def mixer_in(rows):
        h = hf_ref[rows, :] + hb_ref[rows, :]
        gated = []
        for g in range(HEADS):
            lanes = slice(g * DH, (g + 1) * DH)
            gated.append((_norm_lanes(h[:, lanes]) * ng_ref[:, lanes] * _sigmoid(o_ref[rows, lanes])).astype(BF16))
        return jnp.concatenate([yan_ref[rows, :]] + gated, axis=1)

    def permuted_x1(t, mix):
        x1n = _layer_norm(ALPHA * x_ref[subs[t], :] + mod_ref[2:3, :] * mix, ln1g_ref, ln1b_ref)
        for c in range(n_lt):
            perm_ref[t * n_lt + c] = x1n[:, c * DH:(c + 1) * DH]
        return jnp.concatenate(
            [jnp.concatenate([perm_ref[t * n_lt + c, pl.ds(j, ns, stride=NSLAB), :] for c in range(n_lt)], axis=1)
             for j in range(NSLAB)], axis=0)

    def up(h2, k):
        return tuple(_dot(h2, wup_ref[:, off + starts[k]:off + starts[k] + FFN_BLOCKS[k]]) for off in (0, D_FF))

    def conv(a, cols):
        q8 = lax.broadcasted_iota(jnp.int32, (ns, a.shape[1]), 0) % NSLAB
        before_first = jnp.where(q8 == 0, 0.0, pltpu.roll(a[ts - ns:], 1, 0))
        after_last = jnp.where(q8 == NSLAB - 1, 0.0, pltpu.roll(a[:ns], ns - 1, 0))
        left = jnp.concatenate([before_first, a[:ts - ns]], axis=0)
        right = jnp.concatenate([a[ns:], after_last], axis=0)
        return left * cw_ref[0:1, cols] + a * cw_ref[1:2, cols] + right * cw_ref[2:3, cols] + cb_ref[:, cols]

    def hidden(a, k):
        val = conv(a[0], slice(starts[k], starts[k] + FFN_BLOCKS[k]))
        gate = conv(a[1], slice(D_FF + starts[k], D_FF + starts[k] + FFN_BLOCKS[k]))
        prod = gate * val
        return (prod + prod * jnp.tanh(gate)).astype(BF16)

    def finish(t, x1, acc):
        out = _layer_norm(ALPHA * x1 + mod_ref[5:6, :] * acc, ln2g_ref, ln2b_ref)
        for c in range(n_lt):
            for j in range(NSLAB):
                perm_ref[t * n_lt + c, pl.ds(j, ns, stride=NSLAB), :] = out[j * ns:(j + 1) * ns, c * DH:(c + 1) * DH]
        for c in range(n_lt):
            out_ref[subs[t], c * DH:(c + 1) * DH] = perm_ref[t * n_lt + c]

    mixes = [_dot(mixer_in(rows), wout_ref[...]) for rows in subs]
    x1s = [permuted_x1(t, mixes[t]) for t in range(n_sub)]
    h2s = [(x1 * (1.0 + mod_ref[4:5, :]) + mod_ref[3:4, :]).astype(BF16) for x1 in x1s]

    steps = [(t, k) for t in range(n_sub) for k in range(len(FFN_BLOCKS))]
    accs = [None] * n_sub
    pending = up(h2s[0], 0)
    for idx, (t, k) in enumerate(steps):
        following = up(h2s[steps[idx + 1][0]], steps[idx + 1][1]) if idx + 1 < len(steps) else None
        part = _dot(hidden(pending, k), wdown_ref[starts[k]:starts[k] + FFN_BLOCKS[k], :])
        accs[t] = part if accs[t] is None else accs[t] + part
        pending = following
        if k == len(FFN_BLOCKS) - 1:
            finish(t, x1s[t], accs[t])


def _ffn(x, yan, hf, hb, o_pre, mod6, wts, *, tm, n_sub):
    B, S, _ = x.shape
    kern = functools.partial(_ffn_kernel, tm=tm, n_sub=n_sub)
    tok = lambda w: pl.BlockSpec((None, tm, w), lambda b, i: (b, i, 0))
    in_specs = [tok(D_MODEL), tok(HALF_W), tok(HALF_W), tok(HALF_W), tok(HALF_W),
                pl.BlockSpec((None, 6, D_MODEL), lambda b, i: (b, 0, 0))] + [_const_spec(w.shape) for w in wts]
    return pl.pallas_call(
        kern, grid=(B, S // tm), in_specs=in_specs, out_specs=tok(D_MODEL),
        out_shape=jax.ShapeDtypeStruct((B, S, D_MODEL), F32),
        scratch_shapes=[pltpu.VMEM((n_sub * (D_MODEL // DH), tm // n_sub, DH), F32)],
        compiler_params=pltpu.CompilerParams(
            dimension_semantics=("arbitrary", "arbitrary"), vmem_limit_bytes=VMEM_LIMIT),
        name="ffn",
    )(x, yan, hf, hb, o_pre, mod6, *wts)


def kernel(x, c, ctx, c_ctx, w_ada, b_ada, w_in, gmlp_ln_g, gmlp_ws, gmlp_bs, qk_conv_w, qk_conv_b,
           b_igate, b_fgate, mix_norm_g, w_out, ln1_g, ln1_b, w_up, ffn_conv_w, ffn_conv_b, w_down,
           ln2_g, ln2_b):
    B, S, _ = x.shape
    assert DEPTH == 1 and w_in.shape[0] == 1
    l = 0

    c_rows = jnp.concatenate([c, c_ctx[None, :], jnp.zeros((8 - B - 1, D_MODEL), F32)], axis=0)
    mod = _modulation(c_rows, w_ada[l], b_ada[l][None, :])
    mod_x = mod[:B].reshape(B, 6, D_MODEL)
    mod_c = jnp.broadcast_to(mod[B].reshape(1, 6, D_MODEL), (B, 6, D_MODEL))

    w_in_b = jnp.pad(w_in[l], ((0, 0), (0, DH - 2 * N_HD))).astype(BF16)
    gbi = jnp.broadcast_to(b_igate[l].reshape(N_HD, 1), (N_HD, CHUNK))
    gbf = jnp.broadcast_to(b_fgate[l].reshape(N_HD, 1), (N_HD, CHUNK))
    ln_g = gmlp_ln_g[l].reshape(1, HALF_W)
    ws = gmlp_ws[l].astype(BF16)
    bsb = jnp.repeat(gmlp_bs[l].T, DH, axis=1)
    norm_g = mix_norm_g[l][None, :]
    in_wts = (w_in_b, ln_g, ws, bsb, qk_conv_w[l], qk_conv_b[l][None, :], gbi, gbf,
              norm_g[:, :HALF_W])

    _, q_c, kt_c, v_c, _, *gates_c = _inproj(ctx, mod_c, in_wts, tm=ctx.shape[1])
    _, _, c_state, m_state = _scan(q_c, kt_c, v_c, *gates_c)

    yan, q, kt, v, o_pre, *gates = _inproj(x, mod_x, in_wts, tm=512)
    hf, hb, _, _ = _scan(q, kt, v, *gates, init=(c_state, m_state))

    gate_half = jnp.where(jnp.arange(2 * D_FF) < D_FF, 1.0, 0.5)[None, :]
    cw = ffn_conv_w[l] * gate_half
    cb = ffn_conv_b[l][None, :] * gate_half
    ffn_wts = (norm_g[:, HALF_W:], w_out[l].astype(BF16), ln1_g[l][None, :], ln1_b[l][None, :],
               w_up[l].astype(BF16), cw, cb, w_down[l].astype(BF16), ln2_g[l][None, :], ln2_b[l][None, :])
    return _ffn(x, yan, hf, hb, o_pre, mod_x, ffn_wts, tm=512, n_sub=2)
```

```python
import functools

import jax
import jax.numpy as jnp
from jax import lax
from jax.experimental import pallas as pl
from jax.experimental.pallas import tpu as pltpu

D_MODEL = 1024
GRID_W = 64
CHUNK = 128
HEADS = 4
DH = 128
HALF_W = HEADS * DH
N_HD = 2 * HEADS
D_FF = 2688
DEPTH = 1
ALPHA = (2 * DEPTH) ** 0.25
EPS = 1e-5
NSLAB = 8
HALO = 8
FFN_BLOCKS = (768, 768, 768, 384)

F32 = jnp.float32
BF16 = jnp.bfloat16
VMEM_LIMIT = 56 * 1024 * 1024


def _dot(a, b):
    return jnp.dot(a, b, preferred_element_type=F32)


def _norm_lanes(z):
    mu = jnp.mean(z, axis=-1, keepdims=True)
    d = z - mu
    var = jnp.mean(d * d, axis=-1, keepdims=True)
    return d * lax.rsqrt(var + EPS)


def _gelu_tanh(x):
    half = 0.5 * x
    return half + half * jnp.tanh(x * (0.7978845608028654 + 0.035677408136300125 * (x * x)))


def _sigmoid(x):
    return 0.5 * jnp.tanh(0.5 * x) + 0.5


def _log_sigmoid(x):
    return jnp.minimum(x, 0.0) - jnp.log(1.0 + jnp.exp(-jnp.abs(x)))


def _const_spec(shape):
    nd = len(shape)
    return pl.BlockSpec(shape, lambda *_: (0,) * nd, pipeline_mode=pl.Buffered(1))


def _mod_kernel(c_ref, w_ref, b_ref, o_ref):
    cs = c_ref[...]
    a = cs * _sigmoid(cs)
    a_hi = a.astype(BF16)
    a_lo = (a - a_hi.astype(F32)).astype(BF16)
    w = w_ref[...]
    w_hi = w.astype(BF16)
    w_lo = (w - w_hi.astype(F32)).astype(BF16)
    by_hi = _dot(jnp.concatenate([a_hi, a_lo], axis=0), w_hi)
    rows = a.shape[0]
    o_ref[...] = by_hi[:rows] + by_hi[rows:] + _dot(a_hi, w_lo) + b_ref[...]


def _modulation(c_rows, w_ada, b_ada):
    n_out = w_ada.shape[1]
    bn = 768
    return pl.pallas_call(
        _mod_kernel,
        grid=(n_out // bn,),
        in_specs=[pl.BlockSpec((8, D_MODEL), lambda j: (0, 0)),
                  pl.BlockSpec((D_MODEL, bn), lambda j: (0, j)),
                  pl.BlockSpec((1, bn), lambda j: (0, j))],
        out_specs=pl.BlockSpec((8, bn), lambda j: (0, j)),
        out_shape=jax.ShapeDtypeStruct((8, n_out), F32),
        compiler_params=pltpu.CompilerParams(vmem_limit_bytes=VMEM_LIMIT),
        name="mod",
    )(c_rows, w_ada, b_ada)


def _inproj_kernel(x_ref, xp_ref, xn_ref, mod_ref, win_ref, wg_ref,
                   lng_ref, ws_ref, bsb_ref, cw_ref, cb_ref, gbi_ref, gbf_ref, ng_ref,
                   yan_ref, q_ref, kt_ref, vm_ref, o_ref, gb_ref, gcm_ref, gct_ref, gwk_ref, *, tm):
    i = pl.program_id(1)
    n_tiles = pl.num_programs(1)
    shift = mod_ref[0:1, :]
    scale = 1.0 + mod_ref[1:2, :]

    def modulate(xv):
        return (xv * scale + shift).astype(BF16)

    hx = modulate(x_ref[...])
    h_prev = jnp.where(i > 0, modulate(xp_ref[...]), jnp.zeros((), BF16))
    h_next = jnp.where(i < n_tiles - 1, modulate(xn_ref[...]), jnp.zeros((), BF16))
    gp = _dot(hx, wg_ref[...])
    uv = _dot(hx, win_ref[:, 0:2 * HALF_W])
    ext = _dot(jnp.concatenate([h_prev, hx, h_next], axis=0), win_ref[:, 2 * HALF_W:4 * HALF_W])
    vo = _dot(hx, win_ref[:, 4 * HALF_W:6 * HALF_W])

    u = _gelu_tanh(uv[:, :HALF_W])
    v = _gelu_tanh(uv[:, HALF_W:])
    for g in range(HEADS):
        lanes = slice(g * DH, (g + 1) * DH)
        vh = (_norm_lanes(v[:, lanes]) * lng_ref[:, lanes]).astype(BF16)
        for c in range(tm // CHUNK):
            rows = slice(c * CHUNK, (c + 1) * CHUNK)
            mixed = _dot(ws_ref[g], vh[rows]) + bsb_ref[:, lanes]
            ya = u[rows, lanes] * mixed
            yan_ref[rows, lanes] = (_norm_lanes(ya) * ng_ref[:, lanes]).astype(BF16)

    n_ext = tm + 2 * HALO
    pre = ext[HALO:HALO + tm]
    down = pltpu.roll(ext, 1, 0)[HALO:HALO + tm]
    up = pltpu.roll(ext, n_ext - 1, 0)[HALO:HALO + tm]
    conv = down * cw_ref[0:1, :] + pre * cw_ref[1:2, :] + up * cw_ref[2:3, :] + cb_ref[...]
    qk = conv * _sigmoid(conv)
    q_ref[...] = qk[:, :HALF_W].astype(BF16)
    kt_ref[...] = (qk[:, HALF_W:] * (DH ** -0.5)).T.astype(BF16)

    vm_ref[...] = vo[:, :HALF_W].astype(BF16)
    o_ref[...] = vo[:, HALF_W:]

    pos = lax.broadcasted_iota(jnp.int32, (N_HD, CHUNK), 1)
    is_fwd = lax.broadcasted_iota(jnp.int32, (N_HD, CHUNK), 0) < HEADS
    pad = jnp.zeros((DH - N_HD, CHUNK), F32)

    def scan_lanes(z, op, fill, reverse):
        step = 1
        while step < CHUNK:
            if reverse:
                moved = jnp.where(pos < CHUNK - step, pltpu.roll(z, CHUNK - step, 1), fill)
            else:
                moved = jnp.where(pos >= step, pltpu.roll(z, step, 1), fill)
            z = op(z, moved)
            step *= 2
        return z

    for c in range(tm // CHUNK):
        rows = slice(c * CHUNK, (c + 1) * CHUNK)
        gt = gp[rows].T
        log_i = gt[0:N_HD] + gbi_ref[...]
        lf = _log_sigmoid(gt[N_HD:2 * N_HD] + gbf_ref[...])
        csum = scan_lanes(lf, jnp.add, 0.0, False)
        suffix = csum[:, CHUNK - 1:CHUNK] - csum + lf
        b = jnp.where(is_fwd, csum, suffix)
        cdiff = b - log_i
        cmin = jnp.where(is_fwd, scan_lanes(cdiff, jnp.minimum, jnp.inf, False),
                         scan_lanes(cdiff, jnp.minimum, jnp.inf, True))
        cmin_end = jnp.where(is_fwd[:, 0:1], cmin[:, CHUNK - 1:CHUNK], cmin[:, 0:1])
        gct_ref[:, rows] = cdiff
        gwk_ref[:, rows] = jnp.exp(cmin_end - cdiff)
        gb_ref[rows, :] = jnp.concatenate([b, pad], axis=0).T
        gcm_ref[rows, :] = jnp.concatenate([cmin, pad], axis=0).T


def _inproj(xs, mod6, wts, *, tm):
    B, S, _ = xs.shape
    nt = S // tm
    hb = tm // HALO
    n_halo_blocks = S // HALO
    kern = functools.partial(_inproj_kernel, tm=tm)
    tok = lambda w: pl.BlockSpec((None, tm, w), lambda b, i: (b, i, 0))
    in_specs = [
        tok(D_MODEL),
        pl.BlockSpec((None, HALO, D_MODEL), lambda b, i: (b, jnp.maximum(i * hb - 1, 0), 0)),
        pl.BlockSpec((None, HALO, D_MODEL), lambda b, i: (b, jnp.minimum((i + 1) * hb, n_halo_blocks - 1), 0)),
        pl.BlockSpec((None, 6, D_MODEL), lambda b, i: (b, 0, 0)),
    ] + [_const_spec(w.shape) for w in wts]
    out_shape = [
        jax.ShapeDtypeStruct((B, S, HALF_W), BF16),
        jax.ShapeDtypeStruct((B, S, HALF_W), BF16),
        jax.ShapeDtypeStruct((B, HALF_W, S), BF16),
        jax.ShapeDtypeStruct((B, S, HALF_W), BF16),
        jax.ShapeDtypeStruct((B, S, HALF_W), F32),
        jax.ShapeDtypeStruct((B, S, DH), F32),
        jax.ShapeDtypeStruct((B, S, DH), F32),
        jax.ShapeDtypeStruct((B, N_HD, S), F32),
        jax.ShapeDtypeStruct((B, N_HD, S), F32),
    ]
    out_specs = [
        tok(HALF_W), tok(HALF_W),
        pl.BlockSpec((None, HALF_W, tm), lambda b, i: (b, 0, i)),
        tok(HALF_W), tok(HALF_W), tok(DH), tok(DH),
        pl.BlockSpec((None, N_HD, tm), lambda b, i: (b, 0, i)),
        pl.BlockSpec((None, N_HD, tm), lambda b, i: (b, 0, i)),
    ]
    return pl.pallas_call(
        kern, grid=(B, nt), in_specs=in_specs, out_specs=out_specs, out_shape=out_shape,
        compiler_params=pltpu.CompilerParams(
            dimension_semantics=("arbitrary", "arbitrary"), vmem_limit_bytes=VMEM_LIMIT),
        name="inproj",
    )(xs, xs, xs, mod6, *wts)


def _scan_kernel(qf_ref, ktf_ref, vf_ref, gbf_ref, gcmf_ref, gctf_ref, gwkf_ref,
                 qb_ref, ktb_ref, vb_ref, gbb_ref, gcmb_ref, gctb_ref, gwkb_ref,
                 *rest, zero_init, cps):
    if zero_init:
        hf_ref, hb_ref, cst_ref, mst_ref = rest
    else:
        c0_ref, m0_ref, hf_ref, hb_ref, cst_ref, mst_ref = rest

    @pl.when(pl.program_id(1) == 0)
    def _():
        if zero_init:
            cst_ref[...] = jnp.zeros(cst_ref.shape, F32)
            mst_ref[...] = jnp.zeros(mst_ref.shape, F32)
        else:
            cst_ref[...] = c0_ref[...]
            mst_ref[...] = m0_ref[...]

    t_idx = lax.broadcasted_iota(jnp.int32, (CHUNK, CHUNK), 0)
    s_idx = lax.broadcasted_iota(jnp.int32, (CHUNK, CHUNK), 1)
    is_fwd = lax.broadcasted_iota(jnp.int32, (CHUNK, DH), 1) < HEADS
    ones = jnp.ones((CHUNK, DH), BF16)
    f_rows = [slice(k * CHUNK, (k + 1) * CHUNK) for k in range(cps)]
    b_rows = f_rows[::-1]

    terms = []
    m_row = mst_ref[0:1, :]
    for fr, br in zip(f_rows, b_rows):
        b = jnp.where(is_fwd, gbf_ref[fr, :], gbb_ref[br, :])
        cmin = jnp.where(is_fwd, gcmf_ref[fr, :], gcmb_ref[br, :])
        total = jnp.where(is_fwd[0:1], gbf_ref[fr.stop - 1:fr.stop, :], gbb_ref[br.start:br.start + 1, :])
        cmin_end = jnp.where(is_fwd[0:1], gcmf_ref[fr.stop - 1:fr.stop, :], gcmb_ref[br.start:br.start + 1, :])
        rmax = b - cmin
        inter = b + m_row
        m_t = jnp.maximum(inter, rmax)
        gmax = total - cmin_end
        m_new = jnp.maximum(total + m_row, gmax)
        terms.append(dict(cmin=cmin, w_intra=jnp.exp(rmax - m_t), w_state=jnp.exp(inter - m_t),
                          floor=jnp.exp(-m_t), w_old=jnp.exp(total + m_row - m_new), w_new=jnp.exp(gmax - m_new)))
        m_row = m_new
    mst_ref[...] = jnp.broadcast_to(m_row, (N_HD, DH))

    pairs = []
    for j in range(N_HD):
        fwd = j < HEADS
        lanes = slice((j % HEADS) * DH, (j % HEADS + 1) * DH)
        refs = ((qf_ref, ktf_ref, vf_ref, gctf_ref, gwkf_ref, hf_ref) if fwd else
                (qb_ref, ktb_ref, vb_ref, gctb_ref, gwkb_ref, hb_ref))
        pairs.append((fwd, lanes, f_rows if fwd else b_rows) + refs)

    scores, updates, v_augs = [], [], []
    for k in range(cps):
        scores.append([_dot(q_ref[rows[k], lanes], kt_ref[lanes, rows[k]])
                       for (fwd, lanes, rows, q_ref, kt_ref, v_ref, gct_ref, gwk_ref, out_ref) in pairs])
    for k in range(cps):
        v_augs.append([jnp.concatenate([v_ref[rows[k], lanes], ones], axis=1)
                       for (fwd, lanes, rows, q_ref, kt_ref, v_ref, gct_ref, gwk_ref, out_ref) in pairs])
        updates.append([_dot((kt_ref[lanes, rows[k]].astype(F32) * gwk_ref[j:j + 1, rows[k]]).astype(BF16), v_augs[k][j])
                        for j, (fwd, lanes, rows, q_ref, kt_ref, v_ref, gct_ref, gwk_ref, out_ref) in enumerate(pairs)])
    states = [cst_ref[j] for j in range(N_HD)]
    for k in range(cps):
        tk = terms[k]
        from_state = [_dot(q_ref[rows[k], lanes], states[j].astype(BF16))
                      for j, (fwd, lanes, rows, q_ref, kt_ref, v_ref, gct_ref, gwk_ref, out_ref) in enumerate(pairs)]
        states = [tk["w_old"][:, j:j + 1] * states[j] + tk["w_new"][:, j:j + 1] * updates[k][j] for j in range(N_HD)]
        for j, (fwd, lanes, rows, q_ref, kt_ref, v_ref, gct_ref, gwk_ref, out_ref) in enumerate(pairs):
            keep = (s_idx <= t_idx) if fwd else (s_idx >= t_idx)
            decay = jnp.exp(jnp.where(keep, tk["cmin"][:, j:j + 1] - gct_ref[j:j + 1, rows[k]], -jnp.inf))
            sv = _dot((scores[k][j] * decay).astype(BF16), v_augs[k][j])
            wi = jnp.broadcast_to(tk["w_intra"][:, j:j + 1], (CHUNK, DH))
            ws = jnp.broadcast_to(tk["w_state"][:, j:j + 1], (CHUNK, DH))
            num = wi * sv[:, :DH] + ws * from_state[j][:, :DH]
            den = wi * sv[:, DH:] + ws * from_state[j][:, DH:]
            out_ref[rows[k], lanes] = num / jnp.maximum(jnp.abs(den), tk["floor"][:, j:j + 1])
    for j in range(N_HD):
        cst_ref[j] = states[j]


def _scan(q, kt, v, gb, gcm, gct, gwk, init=None):
    B, S, _ = q.shape
    cps = 2
    rows = cps * CHUNK
    nb = S // rows
    fwd_tok = lambda w: pl.BlockSpec((None, rows, w), lambda b, i: (b, i, 0))
    bwd_tok = lambda w: pl.BlockSpec((None, rows, w), lambda b, i: (b, nb - 1 - i, 0))
    fwd_t = lambda r: pl.BlockSpec((None, r, rows), lambda b, i: (b, 0, i))
    bwd_t = lambda r: pl.BlockSpec((None, r, rows), lambda b, i: (b, 0, nb - 1 - i))
    c_spec = pl.BlockSpec((None, N_HD, DH, 2 * DH), lambda b, i: (b, 0, 0, 0))
    m_spec = pl.BlockSpec((None, N_HD, DH), lambda b, i: (b, 0, 0))
    init = () if init is None else tuple(init)
    return pl.pallas_call(
        functools.partial(_scan_kernel, zero_init=not init, cps=cps), grid=(B, nb),
        in_specs=[fwd_tok(HALF_W), fwd_t(HALF_W), fwd_tok(HALF_W), fwd_tok(DH), fwd_tok(DH), fwd_t(N_HD), fwd_t(N_HD),
                  bwd_tok(HALF_W), bwd_t(HALF_W), bwd_tok(HALF_W), bwd_tok(DH), bwd_tok(DH), bwd_t(N_HD), bwd_t(N_HD),
                  ] + [c_spec, m_spec][:len(init)],
        out_specs=[fwd_tok(HALF_W), bwd_tok(HALF_W), c_spec, m_spec],
        out_shape=[jax.ShapeDtypeStruct((B, S, HALF_W), F32),
                   jax.ShapeDtypeStruct((B, S, HALF_W), F32),
                   jax.ShapeDtypeStruct((B, N_HD, DH, 2 * DH), F32),
                   jax.ShapeDtypeStruct((B, N_HD, DH), F32)],
        compiler_params=pltpu.CompilerParams(
            dimension_semantics=("arbitrary", "arbitrary"), vmem_limit_bytes=VMEM_LIMIT),
        name="scan",
    )(q, kt, v, gb, gcm, gct, gwk, q, kt, v, gb, gcm, gct, gwk, *init)


def _layer_norm(z, g_ref, b_ref):
    return _norm_lanes(z) * g_ref[...] + b_ref[...]


def _ffn_kernel(x_ref, yan_ref, hf_ref, hb_ref, o_ref, mod_ref, ng_ref, wout_ref,
                ln1g_ref, ln1b_ref, wup_ref, cw_ref, cb_ref, wdown_ref, ln2g_ref, ln2b_ref,
                out_ref, perm_ref, *, tm, n_sub):
    ts = tm // n_sub
    ns = ts // NSLAB
    n_lt = D_MODEL // DH
    starts = [sum(FFN_BLOCKS[:k]) for k in range(len(FFN_BLOCKS))]
    subs = [slice(t * ts, (t + 1) * ts) for t in range(n_sub)]

    def mixer_in(rows):
        h = hf_ref[rows, :] + hb_ref[rows, :]
        gated = []
        for g in range(HEADS):
            lanes = slice(g * DH, (g + 1) * DH)
            gated.append((_norm_lanes(h[:, lanes]) * ng_ref[:, lanes] * _sigmoid(o_ref[rows, lanes])).astype(BF16))
        return jnp.concatenate([yan_ref[rows, :]] + gated, axis=1)

    def permuted_x1(t, mix):
        x1n = _layer_norm(ALPHA * x_ref[subs[t], :] + mod_ref[2:3, :] * mix, ln1g_ref, ln1b_ref)
        for c in range(n_lt):
            perm_ref[t * n_lt + c] = x1n[:, c * DH:(c + 1) * DH]
        return jnp.concatenate(
            [jnp.concatenate([perm_ref[t * n_lt + c, pl.ds(j, ns, stride=NSLAB), :] for c in range(n_lt)], axis=1)
             for j in range(NSLAB)], axis=0)

    def up(h2, k):
        return _dot(h2, wup_ref[:, 2 * starts[k]:2 * (starts[k] + FFN_BLOCKS[k])])

    def hidden(a, k):
        blk = FFN_BLOCKS[k]
        cols = slice(2 * starts[k], 2 * (starts[k] + blk))
        q8 = lax.broadcasted_iota(jnp.int32, (ns, 2 * blk), 0) % NSLAB
        before_first = jnp.where(q8 == 0, 0.0, pltpu.roll(a[ts - ns:], 1, 0))
        after_last = jnp.where(q8 == NSLAB - 1, 0.0, pltpu.roll(a[:ns], ns - 1, 0))
        left = jnp.concatenate([before_first, a[:ts - ns]], axis=0)
        right = jnp.concatenate([a[ns:], after_last], axis=0)
        a = left * cw_ref[0:1, cols] + a * cw_ref[1:2, cols] + right * cw_ref[2:3, cols] + cb_ref[:, cols]
        prod = a[:, blk:] * a[:, :blk]
        return (prod + prod * jnp.tanh(a[:, blk:])).astype(BF16)

    def finish(t, x1, acc):
        out = _layer_norm(ALPHA * x1 + mod_ref[5:6, :] * acc, ln2g_ref, ln2b_ref)
        for c in range(n_lt):
            for j in range(NSLAB):
                perm_ref[t * n_lt + c, pl.ds(j, ns, stride=NSLAB), :] = out[j * ns:(j + 1) * ns, c * DH:(c + 1) * DH]
        for c in range(n_lt):
            out_ref[subs[t], c * DH:(c + 1) * DH] = perm_ref[t * n_lt + c]

    mixes = [_dot(mixer_in(rows), wout_ref[...]) for rows in subs]
    x1s = [permuted_x1(t, mixes[t]) for t in range(n_sub)]
    h2s = [(x1 * (1.0 + mod_ref[4:5, :]) + mod_ref[3:4, :]).astype(BF16) for x1 in x1s]

    steps = [(t, k) for t in range(n_sub) for k in range(len(FFN_BLOCKS))]
    accs = [None] * n_sub
    pending = up(h2s[0], 0)
    for idx, (t, k) in enumerate(steps):
        following = up(h2s[steps[idx + 1][0]], steps[idx + 1][1]) if idx + 1 < len(steps) else None
        part = _dot(hidden(pending, k), wdown_ref[starts[k]:starts[k] + FFN_BLOCKS[k], :])
        accs[t] = part if accs[t] is None else accs[t] + part
        pending = following
        if k == len(FFN_BLOCKS) - 1:
            finish(t, x1s[t], accs[t])


def _ffn(x, yan, hf, hb, o_pre, mod6, wts, *, tm, n_sub):
    B, S, _ = x.shape
    kern = functools.partial(_ffn_kernel, tm=tm, n_sub=n_sub)
    tok = lambda w: pl.BlockSpec((None, tm, w), lambda b, i: (b, i, 0))
    in_specs = [tok(D_MODEL), tok(HALF_W), tok(HALF_W), tok(HALF_W), tok(HALF_W),
                pl.BlockSpec((None, 6, D_MODEL), lambda b, i: (b, 0, 0))] + [_const_spec(w.shape) for w in wts]
    return pl.pallas_call(
        kern, grid=(B, S // tm), in_specs=in_specs, out_specs=tok(D_MODEL),
        out_shape=jax.ShapeDtypeStruct((B, S, D_MODEL), F32),
        scratch_shapes=[pltpu.VMEM((n_sub * (D_MODEL // DH), tm // n_sub, DH), F32)],
        compiler_params=pltpu.CompilerParams(
            dimension_semantics=("arbitrary", "arbitrary"), vmem_limit_bytes=VMEM_LIMIT),
        name="ffn",
    )(x, yan, hf, hb, o_pre, mod6, *wts)


def _interleave_blocks(val, gate, axis):
    parts, start = [], 0
    for blk in FFN_BLOCKS:
        parts.append(lax.slice_in_dim(val, start, start + blk, axis=axis))
        parts.append(lax.slice_in_dim(gate, start, start + blk, axis=axis))
        start += blk
    return jnp.concatenate(parts, axis=axis)


def kernel(x, c, ctx, c_ctx, w_ada, b_ada, w_in, gmlp_ln_g, gmlp_ws, gmlp_bs, qk_conv_w, qk_conv_b,
           b_igate, b_fgate, mix_norm_g, w_out, ln1_g, ln1_b, w_up, ffn_conv_w, ffn_conv_b, w_down,
           ln2_g, ln2_b):
    B, S, _ = x.shape
    assert DEPTH == 1 and w_in.shape[0] == 1
    l = 0

    c_rows = jnp.concatenate([c, c_ctx[None, :], jnp.zeros((8 - B - 1, D_MODEL), F32)], axis=0)
    mod = _modulation(c_rows, w_ada[l], b_ada[l][None, :])
    mod_x = mod[:B].reshape(B, 6, D_MODEL)
    mod_c = jnp.broadcast_to(mod[B].reshape(1, 6, D_MODEL), (B, 6, D_MODEL))

    w_in_b = w_in[l][:, :6 * HALF_W].astype(BF16)
    w_g = jnp.pad(w_in[l][:, 6 * HALF_W:], ((0, 0), (0, DH - 2 * N_HD))).astype(BF16)
    gbi = jnp.broadcast_to(b_igate[l].reshape(N_HD, 1), (N_HD, CHUNK))
    gbf = jnp.broadcast_to(b_fgate[l].reshape(N_HD, 1), (N_HD, CHUNK))
    ln_g = gmlp_ln_g[l].reshape(1, HALF_W)
    ws = gmlp_ws[l].astype(BF16)
    bsb = jnp.repeat(gmlp_bs[l].T, DH, axis=1)
    norm_g = mix_norm_g[l][None, :]
    in_wts = (w_in_b, w_g, ln_g, ws, bsb, qk_conv_w[l], qk_conv_b[l][None, :], gbi, gbf,
              norm_g[:, :HALF_W])

    _, q_c, kt_c, v_c, _, *gates_c = _inproj(ctx, mod_c, in_wts, tm=ctx.shape[1])
    _, _, c_state, m_state = _scan(q_c, kt_c, v_c, *gates_c)

    yan, q, kt, v, o_pre, *gates = _inproj(x, mod_x, in_wts, tm=512)
    hf, hb, _, _ = _scan(q, kt, v, *gates, init=(c_state, m_state))

    wu = w_up[l]
    w_up_b = _interleave_blocks(wu[:, :D_FF], wu[:, D_FF:], 1).astype(BF16)
    cw = _interleave_blocks(ffn_conv_w[l][:, :D_FF], 0.5 * ffn_conv_w[l][:, D_FF:], 1)
    cb = _interleave_blocks(ffn_conv_b[l][None, :D_FF], 0.5 * ffn_conv_b[l][None, D_FF:], 1)
    ffn_wts = (norm_g[:, HALF_W:], w_out[l].astype(BF16), ln1_g[l][None, :], ln1_b[l][None, :],
               w_up_b, cw, cb, w_down[l].astype(BF16), ln2_g[l][None, :], ln2_b[l][None, :])
    return _ffn(x, yan, hf, hb, o_pre, mod_x, ffn_wts, tm=512, n_sub=2)
```

```python
import functools

import jax
import jax.numpy as jnp
from jax import lax
from jax.experimental import pallas as pl
from jax.experimental.pallas import tpu as pltpu

D_MODEL = 1024
GRID_W = 64
CHUNK = 128
HEADS = 4
DH = 128
HALF_W = HEADS * DH
N_HD = 2 * HEADS
D_FF = 2688
DEPTH = 1
ALPHA = (2 * DEPTH) ** 0.25
EPS = 1e-5
NSLAB = 8
HALO = 8
FFN_BLOCKS = (768, 768, 768, 384)

F32 = jnp.float32
BF16 = jnp.bfloat16
VMEM_LIMIT = 56 * 1024 * 1024


def _dot(a, b):
    return jnp.dot(a, b, preferred_element_type=F32)


def _norm_lanes(z):
    mu = jnp.mean(z, axis=-1, keepdims=True)
    d = z - mu
    var = jnp.mean(d * d, axis=-1, keepdims=True)
    return d * lax.rsqrt(var + EPS)


def _gelu_tanh(x):
    half = 0.5 * x
    return half + half * jnp.tanh(x * (0.7978845608028654 + 0.035677408136300125 * (x * x)))


def _sigmoid(x):
    return 0.5 * jnp.tanh(0.5 * x) + 0.5


def _log_sigmoid(x):
    return jnp.minimum(x, 0.0) - jnp.log(1.0 + jnp.exp(-jnp.abs(x)))


def _const_spec(shape):
    nd = len(shape)
    return pl.BlockSpec(shape, lambda *_: (0,) * nd, pipeline_mode=pl.Buffered(1))


def _mod_kernel(c_ref, w_ref, b_ref, win_ref, o_ref, winb_ref, wgb_ref):
    n_main = winb_ref.shape[1]
    winb_ref[...] = win_ref[:, :n_main].astype(BF16)
    wgb_ref[...] = jnp.zeros(wgb_ref.shape, BF16)
    wgb_ref[:, :2 * N_HD] = win_ref[:, n_main:].astype(BF16)

    cs = c_ref[...]
    a = cs * _sigmoid(cs)
    a_hi = a.astype(BF16)
    a_lo = (a - a_hi.astype(F32)).astype(BF16)
    w = w_ref[...]
    w_hi = w.astype(BF16)
    w_lo = (w - w_hi.astype(F32)).astype(BF16)
    by_hi = _dot(jnp.concatenate([a_hi, a_lo], axis=0), w_hi)
    rows = a.shape[0]
    o_ref[...] = by_hi[:rows] + by_hi[rows:] + _dot(a_hi, w_lo) + b_ref[...]


def _modulation(c_rows, w_ada, b_ada, w_in):
    n_out = w_ada.shape[1]
    n_steps = 8
    bn = n_out // n_steps
    rows = D_MODEL // n_steps
    n_main = 6 * HALF_W
    return pl.pallas_call(
        _mod_kernel,
        grid=(n_steps,),
        in_specs=[pl.BlockSpec((8, D_MODEL), lambda j: (0, 0)),
                  pl.BlockSpec((D_MODEL, bn), lambda j: (0, j)),
                  pl.BlockSpec((1, bn), lambda j: (0, j)),
                  pl.BlockSpec((rows, w_in.shape[1]), lambda j: (j, 0))],
        out_specs=[pl.BlockSpec((8, bn), lambda j: (0, j)),
                   pl.BlockSpec((rows, n_main), lambda j: (j, 0)),
                   pl.BlockSpec((rows, DH), lambda j: (j, 0))],
        out_shape=[jax.ShapeDtypeStruct((8, n_out), F32),
                   jax.ShapeDtypeStruct((D_MODEL, n_main), BF16),
                   jax.ShapeDtypeStruct((D_MODEL, DH), BF16)],
        compiler_params=pltpu.CompilerParams(vmem_limit_bytes=VMEM_LIMIT),
        name="mod",
    )(c_rows, w_ada, b_ada, w_in)


def _inproj_kernel(x_ref, xp_ref, xn_ref, mod_ref, win_ref, wg_ref,
                   lng_ref, ws_ref, bsb_ref, cw_ref, cb_ref, gbi_ref, gbf_ref, ng_ref,
                   *rest, tm, convert):
    if convert:
        wup_ref, wdown_ref, wout_ref = rest[:3]
        wupb_ref, wdownb_ref, woutb_ref = rest[-3:]
        rest = rest[3:-3]
        start = 0
        for blk in FFN_BLOCKS:
            wupb_ref[:, 2 * start:2 * start + blk] = wup_ref[:, start:start + blk].astype(BF16)
            wupb_ref[:, 2 * start + blk:2 * (start + blk)] = wup_ref[:, D_FF + start:D_FF + start + blk].astype(BF16)
            start += blk
        wdownb_ref[...] = wdown_ref[...].astype(BF16)
        woutb_ref[...] = wout_ref[...].astype(BF16)
    yan_ref, q_ref, kt_ref, vm_ref, o_ref, gb_ref, gcm_ref, gct_ref, gwk_ref = rest
    i = pl.program_id(1)
    n_tiles = pl.num_programs(1)
    shift = mod_ref[0:1, :]
    scale = 1.0 + mod_ref[1:2, :]

    def modulate(xv):
        return (xv * scale + shift).astype(BF16)

    hx = modulate(x_ref[...])
    h_prev = jnp.where(i > 0, modulate(xp_ref[...]), jnp.zeros((), BF16))
    h_next = jnp.where(i < n_tiles - 1, modulate(xn_ref[...]), jnp.zeros((), BF16))
    gp = _dot(hx, wg_ref[...])
    uv = _dot(hx, win_ref[:, 0:2 * HALF_W])
    ext = _dot(jnp.concatenate([h_prev, hx, h_next], axis=0), win_ref[:, 2 * HALF_W:4 * HALF_W])
    vo = _dot(hx, win_ref[:, 4 * HALF_W:6 * HALF_W])

    u = _gelu_tanh(uv[:, :HALF_W])
    v = _gelu_tanh(uv[:, HALF_W:])
    for g in range(HEADS):
        lanes = slice(g * DH, (g + 1) * DH)
        vh = (_norm_lanes(v[:, lanes]) * lng_ref[:, lanes]).astype(BF16)
        for c in range(tm // CHUNK):
            rows = slice(c * CHUNK, (c + 1) * CHUNK)
            mixed = _dot(ws_ref[g], vh[rows]) + bsb_ref[:, lanes]
            ya = u[rows, lanes] * mixed
            yan_ref[rows, lanes] = (_norm_lanes(ya) * ng_ref[:, lanes]).astype(BF16)

    n_ext = tm + 2 * HALO
    pre = ext[HALO:HALO + tm]
    down = pltpu.roll(ext, 1, 0)[HALO:HALO + tm]
    up = pltpu.roll(ext, n_ext - 1, 0)[HALO:HALO + tm]
    conv = down * cw_ref[0:1, :] + pre * cw_ref[1:2, :] + up * cw_ref[2:3, :] + cb_ref[...]
    qk = conv * _sigmoid(conv)
    q_ref[...] = qk[:, :HALF_W].astype(BF16)
    kt_ref[...] = (qk[:, HALF_W:] * (DH ** -0.5)).T.astype(BF16)

    vm_ref[...] = vo[:, :HALF_W].astype(BF16)
    o_ref[...] = vo[:, HALF_W:]

    pos = lax.broadcasted_iota(jnp.int32, (N_HD, CHUNK), 1)
    is_fwd = lax.broadcasted_iota(jnp.int32, (N_HD, CHUNK), 0) < HEADS
    pad = jnp.zeros((DH - N_HD, CHUNK), F32)

    def scan_lanes(z, op, fill, reverse):
        step = 1
        while step < CHUNK:
            if reverse:
                moved = jnp.where(pos < CHUNK - step, pltpu.roll(z, CHUNK - step, 1), fill)
            else:
                moved = jnp.where(pos >= step, pltpu.roll(z, step, 1), fill)
            z = op(z, moved)
            step *= 2
        return z

    for c in range(tm // CHUNK):
        rows = slice(c * CHUNK, (c + 1) * CHUNK)
        gt = gp[rows].T
        log_i = gt[0:N_HD] + gbi_ref[...]
        lf = _log_sigmoid(gt[N_HD:2 * N_HD] + gbf_ref[...])
        csum = scan_lanes(lf, jnp.add, 0.0, False)
        suffix = csum[:, CHUNK - 1:CHUNK] - csum + lf
        b = jnp.where(is_fwd, csum, suffix)
        cdiff = b - log_i
        cmin = jnp.where(is_fwd, scan_lanes(cdiff, jnp.minimum, jnp.inf, False),
                         scan_lanes(cdiff, jnp.minimum, jnp.inf, True))
        cmin_end = jnp.where(is_fwd[:, 0:1], cmin[:, CHUNK - 1:CHUNK], cmin[:, 0:1])
        gct_ref[:, rows] = cdiff
        gwk_ref[:, rows] = jnp.exp(cmin_end - cdiff)
        gb_ref[rows, :] = jnp.concatenate([b, pad], axis=0).T
        gcm_ref[rows, :] = jnp.concatenate([cmin, pad], axis=0).T


def _inproj(xs, mod6, wts, *, tm, ffn_f32=()):
    B, S, _ = xs.shape
    nt = S // tm
    hb = tm // HALO
    n_halo_blocks = S // HALO
    kern = functools.partial(_inproj_kernel, tm=tm, convert=bool(ffn_f32))
    tok = lambda w: pl.BlockSpec((None, tm, w), lambda b, i: (b, i, 0))
    in_specs = [
        tok(D_MODEL),
        pl.BlockSpec((None, HALO, D_MODEL), lambda b, i: (b, jnp.maximum(i * hb - 1, 0), 0)),
        pl.BlockSpec((None, HALO, D_MODEL), lambda b, i: (b, jnp.minimum((i + 1) * hb, n_halo_blocks - 1), 0)),
        pl.BlockSpec((None, 6, D_MODEL), lambda b, i: (b, 0, 0)),
    ] + [_const_spec(w.shape) for w in wts]
    out_shape = [
        jax.ShapeDtypeStruct((B, S, HALF_W), BF16),
        jax.ShapeDtypeStruct((B, S, HALF_W), BF16),
        jax.ShapeDtypeStruct((B, HALF_W, S), BF16),
        jax.ShapeDtypeStruct((B, S, HALF_W), BF16),
        jax.ShapeDtypeStruct((B, S, HALF_W), F32),
        jax.ShapeDtypeStruct((B, S, DH), F32),
        jax.ShapeDtypeStruct((B, S, DH), F32),
        jax.ShapeDtypeStruct((B, N_HD, S), F32),
        jax.ShapeDtypeStruct((B, N_HD, S), F32),
    ]
    out_specs = [
        tok(HALF_W), tok(HALF_W),
        pl.BlockSpec((None, HALF_W, tm), lambda b, i: (b, 0, i)),
        tok(HALF_W), tok(HALF_W), tok(DH), tok(DH),
        pl.BlockSpec((None, N_HD, tm), lambda b, i: (b, 0, i)),
        pl.BlockSpec((None, N_HD, tm), lambda b, i: (b, 0, i)),
    ]
    if ffn_f32:
        for w, n_blocks in zip(ffn_f32, (B * nt, 8, B * nt)):
            rows = w.shape[0] // n_blocks
            assert rows * n_blocks == w.shape[0] and rows % 16 == 0 and n_blocks <= B * nt
            spec = pl.BlockSpec((rows, w.shape[1]), lambda b, i, n=n_blocks: (jnp.minimum(b * nt + i, n - 1), 0))
            in_specs.append(spec)
            out_specs.append(spec)
            out_shape.append(jax.ShapeDtypeStruct(w.shape, BF16))
    return pl.pallas_call(
        kern, grid=(B, nt), in_specs=in_specs, out_specs=out_specs, out_shape=out_shape,
        compiler_params=pltpu.CompilerParams(
            dimension_semantics=("arbitrary", "arbitrary"), vmem_limit_bytes=VMEM_LIMIT),
        name="inproj",
    )(xs, xs, xs, mod6, *wts, *ffn_f32)


def _scan_kernel(qf_ref, ktf_ref, vf_ref, gbf_ref, gcmf_ref, gctf_ref, gwkf_ref,
                 qb_ref, ktb_ref, vb_ref, gbb_ref, gcmb_ref, gctb_ref, gwkb_ref,
                 *rest, zero_init, cps):
    if zero_init:
        hf_ref, hb_ref, cst_ref, mst_ref = rest
    else:
        c0_ref, m0_ref, hf_ref, hb_ref, cst_ref, mst_ref = rest

    @pl.when(pl.program_id(1) == 0)
    def _():
        if zero_init:
            cst_ref[...] = jnp.zeros(cst_ref.shape, F32)
            mst_ref[...] = jnp.zeros(mst_ref.shape, F32)
        else:
            cst_ref[...] = c0_ref[...]
            mst_ref[...] = m0_ref[...]

    t_idx = lax.broadcasted_iota(jnp.int32, (CHUNK, CHUNK), 0)
    s_idx = lax.broadcasted_iota(jnp.int32, (CHUNK, CHUNK), 1)
    is_fwd = lax.broadcasted_iota(jnp.int32, (CHUNK, DH), 1) < HEADS
    ones = jnp.ones((CHUNK, DH), BF16)
    f_rows = [slice(k * CHUNK, (k + 1) * CHUNK) for k in range(cps)]
    b_rows = f_rows[::-1]

    terms = []
    m_row = mst_ref[0:1, :]
    for fr, br in zip(f_rows, b_rows):
        b = jnp.where(is_fwd, gbf_ref[fr, :], gbb_ref[br, :])
        cmin = jnp.where(is_fwd, gcmf_ref[fr, :], gcmb_ref[br, :])
        total = jnp.where(is_fwd[0:1], gbf_ref[fr.stop - 1:fr.stop, :], gbb_ref[br.start:br.start + 1, :])
        cmin_end = jnp.where(is_fwd[0:1], gcmf_ref[fr.stop - 1:fr.stop, :], gcmb_ref[br.start:br.start + 1, :])
        rmax = b - cmin
        inter = b + m_row
        m_t = jnp.maximum(inter, rmax)
        gmax = total - cmin_end
        m_new = jnp.maximum(total + m_row, gmax)
        terms.append(dict(cmin=cmin, w_intra=jnp.exp(rmax - m_t), w_state=jnp.exp(inter - m_t),
                          floor=jnp.exp(-m_t), w_old=jnp.exp(total + m_row - m_new), w_new=jnp.exp(gmax - m_new)))
        m_row = m_new
    mst_ref[...] = jnp.broadcast_to(m_row, (N_HD, DH))

    pairs = []
    for j in range(N_HD):
        fwd = j < HEADS
        lanes = slice((j % HEADS) * DH, (j % HEADS + 1) * DH)
        refs = ((qf_ref, ktf_ref, vf_ref, gctf_ref, gwkf_ref, hf_ref) if fwd else
                (qb_ref, ktb_ref, vb_ref, gctb_ref, gwkb_ref, hb_ref))
        pairs.append((fwd, lanes, f_rows if fwd else b_rows) + refs)

    scores, updates, v_augs = [], [], []
    for k in range(cps):
        scores.append([_dot(q_ref[rows[k], lanes], kt_ref[lanes, rows[k]])
                       for (fwd, lanes, rows, q_ref, kt_ref, v_ref, gct_ref, gwk_ref, out_ref) in pairs])
    for k in range(cps):
        v_augs.append([jnp.concatenate([v_ref[rows[k], lanes], ones], axis=1)
                       for (fwd, lanes, rows, q_ref, kt_ref, v_ref, gct_ref, gwk_ref, out_ref) in pairs])
        updates.append([_dot((kt_ref[lanes, rows[k]].astype(F32) * gwk_ref[j:j + 1, rows[k]]).astype(BF16), v_augs[k][j])
                        for j, (fwd, lanes, rows, q_ref, kt_ref, v_ref, gct_ref, gwk_ref, out_ref) in enumerate(pairs)])
    states = [cst_ref[j] for j in range(N_HD)]
    for k in range(cps):
        tk = terms[k]
        from_state = [_dot(q_ref[rows[k], lanes], states[j].astype(BF16))
                      for j, (fwd, lanes, rows, q_ref, kt_ref, v_ref, gct_ref, gwk_ref, out_ref) in enumerate(pairs)]
        states = [tk["w_old"][:, j:j + 1] * states[j] + tk["w_new"][:, j:j + 1] * updates[k][j] for j in range(N_HD)]
        for j, (fwd, lanes, rows, q_ref, kt_ref, v_ref, gct_ref, gwk_ref, out_ref) in enumerate(pairs):
            keep = (s_idx <= t_idx) if fwd else (s_idx >= t_idx)
            decay = jnp.exp(jnp.where(keep, tk["cmin"][:, j:j + 1] - gct_ref[j:j + 1, rows[k]], -jnp.inf))
            sv = _dot((scores[k][j] * decay).astype(BF16), v_augs[k][j])
            wi = jnp.broadcast_to(tk["w_intra"][:, j:j + 1], (CHUNK, DH))
            ws = jnp.broadcast_to(tk["w_state"][:, j:j + 1], (CHUNK, DH))
            num = wi * sv[:, :DH] + ws * from_state[j][:, :DH]
            den = wi * sv[:, DH:] + ws * from_state[j][:, DH:]
            out_ref[rows[k], lanes] = num / jnp.maximum(jnp.abs(den), tk["floor"][:, j:j + 1])
    for j in range(N_HD):
        cst_ref[j] = states[j]


def _scan(q, kt, v, gb, gcm, gct, gwk, init=None):
    B, S, _ = q.shape
    cps = 2
    rows = cps * CHUNK
    nb = S // rows
    fwd_tok = lambda w: pl.BlockSpec((None, rows, w), lambda b, i: (b, i, 0))
    bwd_tok = lambda w: pl.BlockSpec((None, rows, w), lambda b, i: (b, nb - 1 - i, 0))
    fwd_t = lambda r: pl.BlockSpec((None, r, rows), lambda b, i: (b, 0, i))
    bwd_t = lambda r: pl.BlockSpec((None, r, rows), lambda b, i: (b, 0, nb - 1 - i))
    c_spec = pl.BlockSpec((None, N_HD, DH, 2 * DH), lambda b, i: (b, 0, 0, 0))
    m_spec = pl.BlockSpec((None, N_HD, DH), lambda b, i: (b, 0, 0))
    init = () if init is None else tuple(init)
    return pl.pallas_call(
        functools.partial(_scan_kernel, zero_init=not init, cps=cps), grid=(B, nb),
        in_specs=[fwd_tok(HALF_W), fwd_t(HALF_W), fwd_tok(HALF_W), fwd_tok(DH), fwd_tok(DH), fwd_t(N_HD), fwd_t(N_HD),
                  bwd_tok(HALF_W), bwd_t(HALF_W), bwd_tok(HALF_W), bwd_tok(DH), bwd_tok(DH), bwd_t(N_HD), bwd_t(N_HD),
                  ] + [c_spec, m_spec][:len(init)],
        out_specs=[fwd_tok(HALF_W), bwd_tok(HALF_W), c_spec, m_spec],
        out_shape=[jax.ShapeDtypeStruct((B, S, HALF_W), F32),
                   jax.ShapeDtypeStruct((B, S, HALF_W), F32),
                   jax.ShapeDtypeStruct((B, N_HD, DH, 2 * DH), F32),
                   jax.ShapeDtypeStruct((B, N_HD, DH), F32)],
        compiler_params=pltpu.CompilerParams(
            dimension_semantics=("arbitrary", "arbitrary"), vmem_limit_bytes=VMEM_LIMIT),
        name="scan",
    )(q, kt, v, gb, gcm, gct, gwk, q, kt, v, gb, gcm, gct, gwk, *init)


def _layer_norm(z, g_ref, b_ref):
    return _norm_lanes(z) * g_ref[...] + b_ref[...]


def _ffn_kernel(x_ref, yan_ref, hf_ref, hb_ref, o_ref, mod_ref, ng_ref, wout_ref,
                ln1g_ref, ln1b_ref, wup_ref, cw_ref, cb_ref, wdown_ref, ln2g_ref, ln2b_ref,
                out_ref, perm_ref, *, tm, n_sub):
    ts = tm // n_sub
    ns = ts // NSLAB
    n_lt = D_MODEL // DH
    starts = [sum(FFN_BLOCKS[:k]) for k in range(len(FFN_BLOCKS))]
    subs = [slice(t * ts, (t + 1) * ts) for t in range(n_sub)]

    def mixer_in(rows):
        h = hf_ref[rows, :] + hb_ref[rows, :]
        gated = []
        for g in range(HEADS):
            lanes = slice(g * DH, (g + 1) * DH)
            gated.append((_norm_lanes(h[:, lanes]) * ng_ref[:, lanes] * _sigmoid(o_ref[rows, lanes])).astype(BF16))
        return jnp.concatenate([yan_ref[rows, :]] + gated, axis=1)

    def permuted_x1(t, mix):
        x1n = _layer_norm(ALPHA * x_ref[subs[t], :] + mod_ref[2:3, :] * mix, ln1g_ref, ln1b_ref)
        for c in range(n_lt):
            perm_ref[t * n_lt + c] = x1n[:, c * DH:(c + 1) * DH]
        return jnp.concatenate(
            [jnp.concatenate([perm_ref[t * n_lt + c, pl.ds(j, ns, stride=NSLAB), :] for c in range(n_lt)], axis=1)
             for j in range(NSLAB)], axis=0)

    def up(h2, k):
        return _dot(h2, wup_ref[:, 2 * starts[k]:2 * (starts[k] + FFN_BLOCKS[k])])

    def hidden(a, k):
        blk = FFN_BLOCKS[k]
        cols = slice(2 * starts[k], 2 * (starts[k] + blk))
        q8 = lax.broadcasted_iota(jnp.int32, (ns, 2 * blk), 0) % NSLAB
        before_first = jnp.where(q8 == 0, 0.0, pltpu.roll(a[ts - ns:], 1, 0))
        after_last = jnp.where(q8 == NSLAB - 1, 0.0, pltpu.roll(a[:ns], ns - 1, 0))
        left = jnp.concatenate([before_first, a[:ts - ns]], axis=0)
        right = jnp.concatenate([a[ns:], after_last], axis=0)
        a = left * cw_ref[0:1, cols] + a * cw_ref[1:2, cols] + right * cw_ref[2:3, cols] + cb_ref[:, cols]
        prod = a[:, blk:] * a[:, :blk]
        return (prod + prod * jnp.tanh(a[:, blk:])).astype(BF16)

    def finish(t, x1, acc):
        out = _layer_norm(ALPHA * x1 + mod_ref[5:6, :] * acc, ln2g_ref, ln2b_ref)
        for c in range(n_lt):
            for j in range(NSLAB):
                perm_ref[t * n_lt + c, pl.ds(j, ns, stride=NSLAB), :] = out[j * ns:(j + 1) * ns, c * DH:(c + 1) * DH]
        for c in range(n_lt):
            out_ref[subs[t], c * DH:(c + 1) * DH] = perm_ref[t * n_lt + c]

    mixes = [_dot(mixer_in(rows), wout_ref[...]) for rows in subs]
    x1s = [permuted_x1(t, mixes[t]) for t in range(n_sub)]
    h2s = [(x1 * (1.0 + mod_ref[4:5, :]) + mod_ref[3:4, :]).astype(BF16) for x1 in x1s]

    steps = [(t, k) for t in range(n_sub) for k in range(len(FFN_BLOCKS))]
    accs = [None] * n_sub
    pending = up(h2s[0], 0)
    for idx, (t, k) in enumerate(steps):
        following = up(h2s[steps[idx + 1][0]], steps[idx + 1][1]) if idx + 1 < len(steps) else None
        part = _dot(hidden(pending, k), wdown_ref[starts[k]:starts[k] + FFN_BLOCKS[k], :])
        accs[t] = part if accs[t] is None else accs[t] + part
        pending = following
        if k == len(FFN_BLOCKS) - 1:
            finish(t, x1s[t], accs[t])


def _ffn(x, yan, hf, hb, o_pre, mod6, wts, *, tm, n_sub):
    B, S, _ = x.shape
    kern = functools.partial(_ffn_kernel, tm=tm, n_sub=n_sub)
    tok = lambda w: pl.BlockSpec((None, tm, w), lambda b, i: (b, i, 0))
    in_specs = [tok(D_MODEL), tok(HALF_W), tok(HALF_W), tok(HALF_W), tok(HALF_W),
                pl.BlockSpec((None, 6, D_MODEL), lambda b, i: (b, 0, 0))] + [_const_spec(w.shape) for w in wts]
    return pl.pallas_call(
        kern, grid=(B, S // tm), in_specs=in_specs, out_specs=tok(D_MODEL),
        out_shape=jax.ShapeDtypeStruct((B, S, D_MODEL), F32),
        scratch_shapes=[pltpu.VMEM((n_sub * (D_MODEL // DH), tm // n_sub, DH), F32)],
        compiler_params=pltpu.CompilerParams(
            dimension_semantics=("arbitrary", "arbitrary"), vmem_limit_bytes=VMEM_LIMIT),
        name="ffn",
    )(x, yan, hf, hb, o_pre, mod6, *wts)


def _interleave_blocks(val, gate, axis):
    parts, start = [], 0
    for blk in FFN_BLOCKS:
        parts.append(lax.slice_in_dim(val, start, start + blk, axis=axis))
        parts.append(lax.slice_in_dim(gate, start, start + blk, axis=axis))
        start += blk
    return jnp.concatenate(parts, axis=axis)


def kernel(x, c, ctx, c_ctx, w_ada, b_ada, w_in, gmlp_ln_g, gmlp_ws, gmlp_bs, qk_conv_w, qk_conv_b,
           b_igate, b_fgate, mix_norm_g, w_out, ln1_g, ln1_b, w_up, ffn_conv_w, ffn_conv_b, w_down,
           ln2_g, ln2_b):
    B, S, _ = x.shape
    assert DEPTH == 1 and w_in.shape[0] == 1
    l = 0

    c_rows = jnp.concatenate([c, c_ctx[None, :], jnp.zeros((8 - B - 1, D_MODEL), F32)], axis=0)
    mod, w_in_b, w_g = _modulation(c_rows, w_ada[l], b_ada[l][None, :], w_in[l])
    mod_x = mod[:B].reshape(B, 6, D_MODEL)
    mod_c = jnp.broadcast_to(mod[B].reshape(1, 6, D_MODEL), (B, 6, D_MODEL))

    gbi = jnp.broadcast_to(b_igate[l].reshape(N_HD, 1), (N_HD, CHUNK))
    gbf = jnp.broadcast_to(b_fgate[l].reshape(N_HD, 1), (N_HD, CHUNK))
    ln_g = gmlp_ln_g[l].reshape(1, HALF_W)
    ws = gmlp_ws[l].astype(BF16)
    bsb = jnp.repeat(gmlp_bs[l].T, DH, axis=1)
    norm_g = mix_norm_g[l][None, :]
    in_wts = (w_in_b, w_g, ln_g, ws, bsb, qk_conv_w[l], qk_conv_b[l][None, :], gbi, gbf,
              norm_g[:, :HALF_W])

    _, q_c, kt_c, v_c, _, *gates_c = _inproj(ctx, mod_c, in_wts, tm=ctx.shape[1])
    _, _, c_state, m_state = _scan(q_c, kt_c, v_c, *gates_c)

    yan, q, kt, v, o_pre, *rest = _inproj(x, mod_x, in_wts, tm=512, ffn_f32=(w_up[l], w_down[l], w_out[l]))
    gates, (w_up_b, w_down_b, w_out_b) = rest[:4], rest[4:]
    hf, hb, _, _ = _scan(q, kt, v, *gates, init=(c_state, m_state))

    cw = _interleave_blocks(ffn_conv_w[l][:, :D_FF], 0.5 * ffn_conv_w[l][:, D_FF:], 1)
    cb = _interleave_blocks(ffn_conv_b[l][None, :D_FF], 0.5 * ffn_conv_b[l][None, D_FF:], 1)
    ffn_wts = (norm_g[:, HALF_W:], w_out_b, ln1_g[l][None, :], ln1_b[l][None, :],
               w_up_b, cw, cb, w_down_b, ln2_g[l][None, :], ln2_b[l][None, :])
    return _ffn(x, yan, hf, hb, o_pre, mod_x, ffn_wts, tm=512, n_sub=2)
```

```python
import functools

import jax
import jax.numpy as jnp
from jax import lax
from jax.experimental import pallas as pl
from jax.experimental.pallas import tpu as pltpu

D_MODEL = 1024
GRID_W = 64
CHUNK = 128
HEADS = 4
DH = 128
HALF_W = HEADS * DH
N_HD = 2 * HEADS
D_FF = 2688
DEPTH = 1
ALPHA = (2 * DEPTH) ** 0.25
EPS = 1e-5
NSLAB = 8
HALO = 8
FFN_BLOCKS = (768, 768, 768, 384)

F32 = jnp.float32
BF16 = jnp.bfloat16
VMEM_LIMIT = 56 * 1024 * 1024


def _dot(a, b):
    return jnp.dot(a, b, preferred_element_type=F32)


def _norm_lanes(z):
    mu = jnp.mean(z, axis=-1, keepdims=True)
    d = z - mu
    var = jnp.mean(d * d, axis=-1, keepdims=True)
    return d * lax.rsqrt(var + EPS)


def _gelu_tanh(x):
    half = 0.5 * x
    return half + half * jnp.tanh(x * (0.7978845608028654 + 0.035677408136300125 * (x * x)))


def _sigmoid(x):
    return 0.5 * jnp.tanh(0.5 * x) + 0.5


def _log_sigmoid(x):
    return jnp.minimum(x, 0.0) - jnp.log(1.0 + jnp.exp(-jnp.abs(x)))


def _const_spec(shape):
    nd = len(shape)
    return pl.BlockSpec(shape, lambda *_: (0,) * nd, pipeline_mode=pl.Buffered(1))


def _mod_kernel(c_ref, w_ref, b_ref, win_ref, wg_ref, o_ref, winb_ref, wgb_ref):
    winb_ref[...] = win_ref[...].T.astype(BF16)
    wg = wg_ref[...]
    wg_rows = jnp.concatenate([wg, jnp.zeros((DH - wg.shape[0], wg.shape[1]), F32)], axis=0)
    wgb_ref[...] = wg_rows.T.astype(BF16)

    cs = c_ref[...]
    a = cs * _sigmoid(cs)
    a_hi = a.astype(BF16)
    a_lo = (a - a_hi.astype(F32)).astype(BF16)
    w = w_ref[...]
    w_hi = w.astype(BF16)
    w_lo = (w - w_hi.astype(F32)).astype(BF16)
    by_hi = _dot(jnp.concatenate([a_hi, a_lo], axis=0), w_hi)
    rows = a.shape[0]
    o_ref[...] = by_hi[:rows] + by_hi[rows:] + _dot(a_hi, w_lo) + b_ref[...]


def _modulation(c_rows, w_ada, b_ada, w_in_t):
    n_out = w_ada.shape[1]
    n_steps = 8
    bn = n_out // n_steps
    n_main = 6 * HALF_W
    n_gate = 2 * N_HD
    rows = n_main // n_steps
    assert w_in_t.shape == (n_main + n_gate, D_MODEL) and n_main % n_gate == 0
    return pl.pallas_call(
        _mod_kernel,
        grid=(n_steps,),
        in_specs=[pl.BlockSpec((8, D_MODEL), lambda j: (0, 0)),
                  pl.BlockSpec((D_MODEL, bn), lambda j: (0, j)),
                  pl.BlockSpec((1, bn), lambda j: (0, j)),
                  pl.BlockSpec((rows, D_MODEL), lambda j: (j, 0)),
                  pl.BlockSpec((n_gate, D_MODEL), lambda j: (n_main // n_gate, 0))],
        out_specs=[pl.BlockSpec((8, bn), lambda j: (0, j)),
                   pl.BlockSpec((D_MODEL, rows), lambda j: (0, j)),
                   pl.BlockSpec((D_MODEL, DH), lambda j: (0, 0))],
        out_shape=[jax.ShapeDtypeStruct((8, n_out), F32),
                   jax.ShapeDtypeStruct((D_MODEL, n_main), BF16),
                   jax.ShapeDtypeStruct((D_MODEL, DH), BF16)],
        compiler_params=pltpu.CompilerParams(vmem_limit_bytes=VMEM_LIMIT),
        name="mod",
    )(c_rows, w_ada, b_ada, w_in_t, w_in_t)


def _inproj_kernel(x_ref, xp_ref, xn_ref, mod_ref, win_ref, wg_ref,
                   lng_ref, ws_ref, bsb_ref, cw_ref, cb_ref, gbi_ref, gbf_ref, ng_ref,
                   *rest, tm, convert):
    if convert:
        wup_ref, wdown_ref, wout_ref = rest[:3]
        wupb_ref, wdownb_ref, woutb_ref = rest[-3:]
        rest = rest[3:-3]
        start = 0
        for blk in FFN_BLOCKS:
            wupb_ref[:, 2 * start:2 * start + blk] = wup_ref[:, start:start + blk].astype(BF16)
            wupb_ref[:, 2 * start + blk:2 * (start + blk)] = wup_ref[:, D_FF + start:D_FF + start + blk].astype(BF16)
            start += blk
        wdownb_ref[...] = wdown_ref[...].astype(BF16)
        woutb_ref[...] = wout_ref[...].astype(BF16)
    yan_ref, q_ref, kt_ref, vm_ref, o_ref, gb_ref, gcm_ref, gct_ref, gwk_ref = rest
    i = pl.program_id(1)
    n_tiles = pl.num_programs(1)
    shift = mod_ref[0:1, :]
    scale = 1.0 + mod_ref[1:2, :]

    def modulate(xv):
        return (xv * scale + shift).astype(BF16)

    hx = modulate(x_ref[...])
    h_prev = jnp.where(i > 0, modulate(xp_ref[...]), jnp.zeros((), BF16))
    h_next = jnp.where(i < n_tiles - 1, modulate(xn_ref[...]), jnp.zeros((), BF16))
    gp = _dot(hx, wg_ref[...])
    uv = _dot(hx, win_ref[:, 0:2 * HALF_W])
    ext = _dot(jnp.concatenate([h_prev, hx, h_next], axis=0), win_ref[:, 2 * HALF_W:4 * HALF_W])
    vo = _dot(hx, win_ref[:, 4 * HALF_W:6 * HALF_W])

    u = _gelu_tanh(uv[:, :HALF_W])
    v = _gelu_tanh(uv[:, HALF_W:])
    for g in range(HEADS):
        lanes = slice(g * DH, (g + 1) * DH)
        vh = (_norm_lanes(v[:, lanes]) * lng_ref[:, lanes]).astype(BF16)
        for c in range(tm // CHUNK):
            rows = slice(c * CHUNK, (c + 1) * CHUNK)
            mixed = _dot(ws_ref[g], vh[rows]) + bsb_ref[:, lanes]
            ya = u[rows, lanes] * mixed
            yan_ref[rows, lanes] = (_norm_lanes(ya) * ng_ref[:, lanes]).astype(BF16)

    n_ext = tm + 2 * HALO
    pre = ext[HALO:HALO + tm]
    down = pltpu.roll(ext, 1, 0)[HALO:HALO + tm]
    up = pltpu.roll(ext, n_ext - 1, 0)[HALO:HALO + tm]
    conv = down * cw_ref[0:1, :] + pre * cw_ref[1:2, :] + up * cw_ref[2:3, :] + cb_ref[...]
    qk = conv * _sigmoid(conv)
    q_ref[...] = qk[:, :HALF_W].astype(BF16)
    kt_ref[...] = (qk[:, HALF_W:] * (DH ** -0.5)).T.astype(BF16)

    vm_ref[...] = vo[:, :HALF_W].astype(BF16)
    o_ref[...] = vo[:, HALF_W:]

    pos = lax.broadcasted_iota(jnp.int32, (N_HD, CHUNK), 1)
    is_fwd = lax.broadcasted_iota(jnp.int32, (N_HD, CHUNK), 0) < HEADS
    pad = jnp.zeros((DH - N_HD, CHUNK), F32)

    def scan_lanes(z, op, fill, reverse):
        step = 1
        while step < CHUNK:
            if reverse:
                moved = jnp.where(pos < CHUNK - step, pltpu.roll(z, CHUNK - step, 1), fill)
            else:
                moved = jnp.where(pos >= step, pltpu.roll(z, step, 1), fill)
            z = op(z, moved)
            step *= 2
        return z

    for c in range(tm // CHUNK):
        rows = slice(c * CHUNK, (c + 1) * CHUNK)
        gt = gp[rows].T
        log_i = gt[0:N_HD] + gbi_ref[...]
        lf = _log_sigmoid(gt[N_HD:2 * N_HD] + gbf_ref[...])
        csum = scan_lanes(lf, jnp.add, 0.0, False)
        suffix = csum[:, CHUNK - 1:CHUNK] - csum + lf
        b = jnp.where(is_fwd, csum, suffix)
        cdiff = b - log_i
        cmin = jnp.where(is_fwd, scan_lanes(cdiff, jnp.minimum, jnp.inf, False),
                         scan_lanes(cdiff, jnp.minimum, jnp.inf, True))
        cmin_end = jnp.where(is_fwd[:, 0:1], cmin[:, CHUNK - 1:CHUNK], cmin[:, 0:1])
        gct_ref[:, rows] = cdiff
        gwk_ref[:, rows] = jnp.exp(cmin_end - cdiff)
        gb_ref[rows, :] = jnp.concatenate([b, pad], axis=0).T
        gcm_ref[rows, :] = jnp.concatenate([cmin, pad], axis=0).T


def _inproj(xs, mod6, wts, *, tm, ffn_f32=()):
    B, S, _ = xs.shape
    nt = S // tm
    hb = tm // HALO
    n_halo_blocks = S // HALO
    kern = functools.partial(_inproj_kernel, tm=tm, convert=bool(ffn_f32))
    tok = lambda w: pl.BlockSpec((None, tm, w), lambda b, i: (b, i, 0))
    in_specs = [
        tok(D_MODEL),
        pl.BlockSpec((None, HALO, D_MODEL), lambda b, i: (b, jnp.maximum(i * hb - 1, 0), 0)),
        pl.BlockSpec((None, HALO, D_MODEL), lambda b, i: (b, jnp.minimum((i + 1) * hb, n_halo_blocks - 1), 0)),
        pl.BlockSpec((None, 6, D_MODEL), lambda b, i: (b, 0, 0)),
    ] + [_const_spec(w.shape) for w in wts]
    out_shape = [
        jax.ShapeDtypeStruct((B, S, HALF_W), BF16),
        jax.ShapeDtypeStruct((B, S, HALF_W), BF16),
        jax.ShapeDtypeStruct((B, HALF_W, S), BF16),
        jax.ShapeDtypeStruct((B, S, HALF_W), BF16),
        jax.ShapeDtypeStruct((B, S, HALF_W), F32),
        jax.ShapeDtypeStruct((B, S, DH), F32),
        jax.ShapeDtypeStruct((B, S, DH), F32),
        jax.ShapeDtypeStruct((B, N_HD, S), F32),
        jax.ShapeDtypeStruct((B, N_HD, S), F32),
    ]
    out_specs = [
        tok(HALF_W), tok(HALF_W),
        pl.BlockSpec((None, HALF_W, tm), lambda b, i: (b, 0, i)),
        tok(HALF_W), tok(HALF_W), tok(DH), tok(DH),
        pl.BlockSpec((None, N_HD, tm), lambda b, i: (b, 0, i)),
        pl.BlockSpec((None, N_HD, tm), lambda b, i: (b, 0, i)),
    ]
    if ffn_f32:
        for w, n_blocks in zip(ffn_f32, (B * nt, 8, B * nt)):
            rows = w.shape[0] // n_blocks
            assert rows * n_blocks == w.shape[0] and rows % 16 == 0 and n_blocks <= B * nt
            spec = pl.BlockSpec((rows, w.shape[1]), lambda b, i, n=n_blocks: (jnp.minimum(b * nt + i, n - 1), 0))
            in_specs.append(spec)
            out_specs.append(spec)
            out_shape.append(jax.ShapeDtypeStruct(w.shape, BF16))
    return pl.pallas_call(
        kern, grid=(B, nt), in_specs=in_specs, out_specs=out_specs, out_shape=out_shape,
        compiler_params=pltpu.CompilerParams(
            dimension_semantics=("arbitrary", "arbitrary"), vmem_limit_bytes=VMEM_LIMIT),
        name="inproj",
    )(xs, xs, xs, mod6, *wts, *ffn_f32)


def _scan_kernel(qf_ref, ktf_ref, vf_ref, gbf_ref, gcmf_ref, gctf_ref, gwkf_ref,
                 qb_ref, ktb_ref, vb_ref, gbb_ref, gcmb_ref, gctb_ref, gwkb_ref,
                 *rest, zero_init, cps):
    if zero_init:
        hf_ref, hb_ref, cst_ref, mst_ref = rest
    else:
        c0_ref, m0_ref, hf_ref, hb_ref, cst_ref, mst_ref = rest

    @pl.when(pl.program_id(1) == 0)
    def _():
        if zero_init:
            cst_ref[...] = jnp.zeros(cst_ref.shape, F32)
            mst_ref[...] = jnp.zeros(mst_ref.shape, F32)
        else:
            cst_ref[...] = c0_ref[...]
            mst_ref[...] = m0_ref[...]

    t_idx = lax.broadcasted_iota(jnp.int32, (CHUNK, CHUNK), 0)
    s_idx = lax.broadcasted_iota(jnp.int32, (CHUNK, CHUNK), 1)
    is_fwd = lax.broadcasted_iota(jnp.int32, (CHUNK, DH), 1) < HEADS
    ones = jnp.ones((CHUNK, DH), BF16)
    f_rows = [slice(k * CHUNK, (k + 1) * CHUNK) for k in range(cps)]
    b_rows = f_rows[::-1]

    terms = []
    m_row = mst_ref[0:1, :]
    for fr, br in zip(f_rows, b_rows):
        b = jnp.where(is_fwd, gbf_ref[fr, :], gbb_ref[br, :])
        cmin = jnp.where(is_fwd, gcmf_ref[fr, :], gcmb_ref[br, :])
        total = jnp.where(is_fwd[0:1], gbf_ref[fr.stop - 1:fr.stop, :], gbb_ref[br.start:br.start + 1, :])
        cmin_end = jnp.where(is_fwd[0:1], gcmf_ref[fr.stop - 1:fr.stop, :], gcmb_ref[br.start:br.start + 1, :])
        rmax = b - cmin
        inter = b + m_row
        m_t = jnp.maximum(inter, rmax)
        gmax = total - cmin_end
        m_new = jnp.maximum(total + m_row, gmax)
        terms.append(dict(cmin=cmin, w_intra=jnp.exp(rmax - m_t), w_state=jnp.exp(inter - m_t),
                          floor=jnp.exp(-m_t), w_old=jnp.exp(total + m_row - m_new), w_new=jnp.exp(gmax - m_new)))
        m_row = m_new
    mst_ref[...] = jnp.broadcast_to(m_row, (N_HD, DH))

    pairs = []
    for j in range(N_HD):
        fwd = j < HEADS
        lanes = slice((j % HEADS) * DH, (j % HEADS + 1) * DH)
        refs = ((qf_ref, ktf_ref, vf_ref, gctf_ref, gwkf_ref, hf_ref) if fwd else
                (qb_ref, ktb_ref, vb_ref, gctb_ref, gwkb_ref, hb_ref))
        pairs.append((fwd, lanes, f_rows if fwd else b_rows) + refs)

    scores, updates, v_augs = [], [], []
    for k in range(cps):
        scores.append([_dot(q_ref[rows[k], lanes], kt_ref[lanes, rows[k]])
                       for (fwd, lanes, rows, q_ref, kt_ref, v_ref, gct_ref, gwk_ref, out_ref) in pairs])
    for k in range(cps):
        v_augs.append([jnp.concatenate([v_ref[rows[k], lanes], ones], axis=1)
                       for (fwd, lanes, rows, q_ref, kt_ref, v_ref, gct_ref, gwk_ref, out_ref) in pairs])
        updates.append([_dot((kt_ref[lanes, rows[k]].astype(F32) * gwk_ref[j:j + 1, rows[k]]).astype(BF16), v_augs[k][j])
                        for j, (fwd, lanes, rows, q_ref, kt_ref, v_ref, gct_ref, gwk_ref, out_ref) in enumerate(pairs)])
    states = [cst_ref[j] for j in range(N_HD)]
    for k in range(cps):
        tk = terms[k]
        from_state = [_dot(q_ref[rows[k], lanes], states[j].astype(BF16))
                      for j, (fwd, lanes, rows, q_ref, kt_ref, v_ref, gct_ref, gwk_ref, out_ref) in enumerate(pairs)]
        states = [tk["w_old"][:, j:j + 1] * states[j] + tk["w_new"][:, j:j + 1] * updates[k][j] for j in range(N_HD)]
        for j, (fwd, lanes, rows, q_ref, kt_ref, v_ref, gct_ref, gwk_ref, out_ref) in enumerate(pairs):
            keep = (s_idx <= t_idx) if fwd else (s_idx >= t_idx)
            decay = jnp.exp(jnp.where(keep, tk["cmin"][:, j:j + 1] - gct_ref[j:j + 1, rows[k]], -jnp.inf))
            sv = _dot((scores[k][j] * decay).astype(BF16), v_augs[k][j])
            wi = jnp.broadcast_to(tk["w_intra"][:, j:j + 1], (CHUNK, DH))
            ws = jnp.broadcast_to(tk["w_state"][:, j:j + 1], (CHUNK, DH))
            num = wi * sv[:, :DH] + ws * from_state[j][:, :DH]
            den = wi * sv[:, DH:] + ws * from_state[j][:, DH:]
            out_ref[rows[k], lanes] = num / jnp.maximum(jnp.abs(den), tk["floor"][:, j:j + 1])
    for j in range(N_HD):
        cst_ref[j] = states[j]


def _scan(q, kt, v, gb, gcm, gct, gwk, init=None):
    B, S, _ = q.shape
    cps = 2
    rows = cps * CHUNK
    nb = S // rows
    fwd_tok = lambda w: pl.BlockSpec((None, rows, w), lambda b, i: (b, i, 0))
    bwd_tok = lambda w: pl.BlockSpec((None, rows, w), lambda b, i: (b, nb - 1 - i, 0))
    fwd_t = lambda r: pl.BlockSpec((None, r, rows), lambda b, i: (b, 0, i))
    bwd_t = lambda r: pl.BlockSpec((None, r, rows), lambda b, i: (b, 0, nb - 1 - i))
    c_spec = pl.BlockSpec((None, N_HD, DH, 2 * DH), lambda b, i: (b, 0, 0, 0))
    m_spec = pl.BlockSpec((None, N_HD, DH), lambda b, i: (b, 0, 0))
    init = () if init is None else tuple(init)
    return pl.pallas_call(
        functools.partial(_scan_kernel, zero_init=not init, cps=cps), grid=(B, nb),
        in_specs=[fwd_tok(HALF_W), fwd_t(HALF_W), fwd_tok(HALF_W), fwd_tok(DH), fwd_tok(DH), fwd_t(N_HD), fwd_t(N_HD),
                  bwd_tok(HALF_W), bwd_t(HALF_W), bwd_tok(HALF_W), bwd_tok(DH), bwd_tok(DH), bwd_t(N_HD), bwd_t(N_HD),
                  ] + [c_spec, m_spec][:len(init)],
        out_specs=[fwd_tok(HALF_W), bwd_tok(HALF_W), c_spec, m_spec],
        out_shape=[jax.ShapeDtypeStruct((B, S, HALF_W), F32),
                   jax.ShapeDtypeStruct((B, S, HALF_W), F32),
                   jax.ShapeDtypeStruct((B, N_HD, DH, 2 * DH), F32),
                   jax.ShapeDtypeStruct((B, N_HD, DH), F32)],
        compiler_params=pltpu.CompilerParams(
            dimension_semantics=("arbitrary", "arbitrary"), vmem_limit_bytes=VMEM_LIMIT),
        name="scan",
    )(q, kt, v, gb, gcm, gct, gwk, q, kt, v, gb, gcm, gct, gwk, *init)


def _layer_norm(z, g_ref, b_ref):
    return _norm_lanes(z) * g_ref[...] + b_ref[...]


def _ffn_kernel(x_ref, yan_ref, hf_ref, hb_ref, o_ref, mod_ref, ng_ref, wout_ref,
                ln1g_ref, ln1b_ref, wup_ref, cw_ref, cb_ref, wdown_ref, ln2g_ref, ln2b_ref,
                out_ref, perm_ref, *, tm, n_sub):
    ts = tm // n_sub
    ns = ts // NSLAB
    n_lt = D_MODEL // DH
    starts = [sum(FFN_BLOCKS[:k]) for k in range(len(FFN_BLOCKS))]
    subs = [slice(t * ts, (t + 1) * ts) for t in range(n_sub)]

    def mixer_in(rows):
        h = hf_ref[rows, :] + hb_ref[rows, :]
        gated = []
        for g in range(HEADS):
            lanes = slice(g * DH, (g + 1) * DH)
            gated.append((_norm_lanes(h[:, lanes]) * ng_ref[:, lanes] * _sigmoid(o_ref[rows, lanes])).astype(BF16))
        return jnp.concatenate([yan_ref[rows, :]] + gated, axis=1)

    def permuted_x1(t, mix):
        x1n = _layer_norm(ALPHA * x_ref[subs[t], :] + mod_ref[2:3, :] * mix, ln1g_ref, ln1b_ref)
        for c in range(n_lt):
            perm_ref[t * n_lt + c] = x1n[:, c * DH:(c + 1) * DH]
        return jnp.concatenate(
            [jnp.concatenate([perm_ref[t * n_lt + c, pl.ds(j, ns, stride=NSLAB), :] for c in range(n_lt)], axis=1)
             for j in range(NSLAB)], axis=0)

    def up(h2, k):
        return _dot(h2, wup_ref[:, 2 * starts[k]:2 * (starts[k] + FFN_BLOCKS[k])])

    def hidden(a, k):
        blk = FFN_BLOCKS[k]
        cols = slice(2 * starts[k], 2 * (starts[k] + blk))
        q8 = lax.broadcasted_iota(jnp.int32, (ns, 2 * blk), 0) % NSLAB
        before_first = jnp.where(q8 == 0, 0.0, pltpu.roll(a[ts - ns:], 1, 0))
        after_last = jnp.where(q8 == NSLAB - 1, 0.0, pltpu.roll(a[:ns], ns - 1, 0))
        left = jnp.concatenate([before_first, a[:ts - ns]], axis=0)
        right = jnp.concatenate([a[ns:], after_last], axis=0)
        a = left * cw_ref[0:1, cols] + a * cw_ref[1:2, cols] + right * cw_ref[2:3, cols] + cb_ref[:, cols]
        prod = a[:, blk:] * a[:, :blk]
        return (prod + prod * jnp.tanh(a[:, blk:])).astype(BF16)

    def finish(t, x1, acc):
        out = _layer_norm(ALPHA * x1 + mod_ref[5:6, :] * acc, ln2g_ref, ln2b_ref)
        for c in range(n_lt):
            for j in range(NSLAB):
                perm_ref[t * n_lt + c, pl.ds(j, ns, stride=NSLAB), :] = out[j * ns:(j + 1) * ns, c * DH:(c + 1) * DH]
        for c in range(n_lt):
            out_ref[subs[t], c * DH:(c + 1) * DH] = perm_ref[t * n_lt + c]

    mixes = [_dot(mixer_in(rows), wout_ref[...]) for rows in subs]
    x1s = [permuted_x1(t, mixes[t]) for t in range(n_sub)]
    h2s = [(x1 * (1.0 + mod_ref[4:5, :]) + mod_ref[3:4, :]).astype(BF16) for x1 in x1s]

    steps = [(t, k) for t in range(n_sub) for k in range(len(FFN_BLOCKS))]
    accs = [None] * n_sub
    pending = up(h2s[0], 0)
    for idx, (t, k) in enumerate(steps):
        following = up(h2s[steps[idx + 1][0]], steps[idx + 1][1]) if idx + 1 < len(steps) else None
        part = _dot(hidden(pending, k), wdown_ref[starts[k]:starts[k] + FFN_BLOCKS[k], :])
        accs[t] = part if accs[t] is None else accs[t] + part
        pending = following
        if k == len(FFN_BLOCKS) - 1:
            finish(t, x1s[t], accs[t])


def _ffn(x, yan, hf, hb, o_pre, mod6, wts, *, tm, n_sub):
    B, S, _ = x.shape
    kern = functools.partial(_ffn_kernel, tm=tm, n_sub=n_sub)
    tok = lambda w: pl.BlockSpec((None, tm, w), lambda b, i: (b, i, 0))
    in_specs = [tok(D_MODEL), tok(HALF_W), tok(HALF_W), tok(HALF_W), tok(HALF_W),
                pl.BlockSpec((None, 6, D_MODEL), lambda b, i: (b, 0, 0))] + [_const_spec(w.shape) for w in wts]
    return pl.pallas_call(
        kern, grid=(B, S // tm), in_specs=in_specs, out_specs=tok(D_MODEL),
        out_shape=jax.ShapeDtypeStruct((B, S, D_MODEL), F32),
        scratch_shapes=[pltpu.VMEM((n_sub * (D_MODEL // DH), tm // n_sub, DH), F32)],
        compiler_params=pltpu.CompilerParams(
            dimension_semantics=("arbitrary", "arbitrary"), vmem_limit_bytes=VMEM_LIMIT),
        name="ffn",
    )(x, yan, hf, hb, o_pre, mod6, *wts)


def _interleave_blocks(val, gate, axis):
    parts, start = [], 0
    for blk in FFN_BLOCKS:
        parts.append(lax.slice_in_dim(val, start, start + blk, axis=axis))
        parts.append(lax.slice_in_dim(gate, start, start + blk, axis=axis))
        start += blk
    return jnp.concatenate(parts, axis=axis)


def kernel(x, c, ctx, c_ctx, w_ada, b_ada, w_in, gmlp_ln_g, gmlp_ws, gmlp_bs, qk_conv_w, qk_conv_b,
           b_igate, b_fgate, mix_norm_g, w_out, ln1_g, ln1_b, w_up, ffn_conv_w, ffn_conv_b, w_down,
           ln2_g, ln2_b):
    B, S, _ = x.shape
    assert DEPTH == 1 and w_in.shape[0] == 1
    l = 0

    c_rows = jnp.concatenate([c, c_ctx[None, :], jnp.zeros((8 - B - 1, D_MODEL), F32)], axis=0)
    mod, w_in_b, w_g = _modulation(c_rows, w_ada[l], b_ada[l][None, :], jnp.swapaxes(w_in[l], 0, 1))
    mod_x = mod[:B].reshape(B, 6, D_MODEL)
    mod_c = jnp.broadcast_to(mod[B].reshape(1, 6, D_MODEL), (B, 6, D_MODEL))

    gbi = jnp.broadcast_to(b_igate[l].reshape(N_HD, 1), (N_HD, CHUNK))
    gbf = jnp.broadcast_to(b_fgate[l].reshape(N_HD, 1), (N_HD, CHUNK))
    ln_g = gmlp_ln_g[l].reshape(1, HALF_W)
    ws = gmlp_ws[l].astype(BF16)
    bsb = jnp.repeat(gmlp_bs[l].T, DH, axis=1)
    norm_g = mix_norm_g[l][None, :]
    in_wts = (w_in_b, w_g, ln_g, ws, bsb, qk_conv_w[l], qk_conv_b[l][None, :], gbi, gbf,
              norm_g[:, :HALF_W])

    _, q_c, kt_c, v_c, _, *gates_c = _inproj(ctx, mod_c, in_wts, tm=ctx.shape[1])
    _, _, c_state, m_state = _scan(q_c, kt_c, v_c, *gates_c)

    yan, q, kt, v, o_pre, *rest = _inproj(x, mod_x, in_wts, tm=512, ffn_f32=(w_up[l], w_down[l], w_out[l]))
    gates, (w_up_b, w_down_b, w_out_b) = rest[:4], rest[4:]
    hf, hb, _, _ = _scan(q, kt, v, *gates, init=(c_state, m_state))

    cw = _interleave_blocks(ffn_conv_w[l][:, :D_FF], 0.5 * ffn_conv_w[l][:, D_FF:], 1)
    cb = _interleave_blocks(ffn_conv_b[l][None, :D_FF], 0.5 * ffn_conv_b[l][None, D_FF:], 1)
    ffn_wts = (norm_g[:, HALF_W:], w_out_b, ln1_g[l][None, :], ln1_b[l][None, :],
               w_up_b, cw, cb, w_down_b, ln2_g[l][None, :], ln2_b[l][None, :])
    return _ffn(x, yan, hf, hb, o_pre, mod_x, ffn_wts, tm=512, n_sub=2)
```

```python
import functools

import jax
import jax.numpy as jnp
from jax import lax
from jax.experimental import pallas as pl
from jax.experimental.pallas import tpu as pltpu

D_MODEL = 1024
GRID_W = 64
CHUNK = 128
HEADS = 4
DH = 128
HALF_W = HEADS * DH
N_HD = 2 * HEADS
D_FF = 2688
DEPTH = 1
ALPHA = (2 * DEPTH) ** 0.25
EPS = 1e-5
NSLAB = 8
HALO = 16
FFN_BLOCKS = (768, 768, 768, 384)

F32 = jnp.float32
BF16 = jnp.bfloat16
VMEM_LIMIT = 56 * 1024 * 1024


def _dot(a, b):
    return jnp.dot(a, b, preferred_element_type=F32)


def _norm_lanes(z):
    mu = jnp.mean(z, axis=-1, keepdims=True)
    d = z - mu
    var = jnp.mean(d * d, axis=-1, keepdims=True)
    return d * lax.rsqrt(var + EPS)


def _gelu_tanh(x):
    half = 0.5 * x
    return half + half * jnp.tanh(x * (0.7978845608028654 + 0.035677408136300125 * (x * x)))


def _sigmoid(x):
    return 0.5 * jnp.tanh(0.5 * x) + 0.5


def _log_sigmoid(x):
    return jnp.minimum(x, 0.0) - jnp.log(1.0 + jnp.exp(-jnp.abs(x)))


def _const_spec(shape):
    nd = len(shape)
    return pl.BlockSpec(shape, lambda *_: (0,) * nd, pipeline_mode=pl.Buffered(1))


def _mod_kernel(c_ref, w_ref, b_ref, win_ref, wg_ref, o_ref, winb_ref, wgb_ref):
    winb_ref[...] = win_ref[...].T.astype(BF16)
    wg = wg_ref[...]
    wg_rows = jnp.concatenate([wg, jnp.zeros((DH - wg.shape[0], wg.shape[1]), F32)], axis=0)
    wgb_ref[...] = wg_rows.T.astype(BF16)

    cs = c_ref[...]
    a = cs * _sigmoid(cs)
    a_hi = a.astype(BF16)
    a_lo = (a - a_hi.astype(F32)).astype(BF16)
    w = w_ref[...]
    w_hi = w.astype(BF16)
    w_lo = (w - w_hi.astype(F32)).astype(BF16)
    by_hi = _dot(jnp.concatenate([a_hi, a_lo], axis=0), w_hi)
    rows = a.shape[0]
    o_ref[...] = by_hi[:rows] + by_hi[rows:] + _dot(a_hi, w_lo) + b_ref[...]


def _modulation(c_rows, w_ada, b_ada, w_in_t):
    n_out = w_ada.shape[1]
    n_steps = 8
    bn = n_out // n_steps
    n_main = 6 * HALF_W
    n_gate = 2 * N_HD
    rows = n_main // n_steps
    assert w_in_t.shape == (n_main + n_gate, D_MODEL) and n_main % n_gate == 0
    return pl.pallas_call(
        _mod_kernel,
        grid=(n_steps,),
        in_specs=[pl.BlockSpec((8, D_MODEL), lambda j: (0, 0)),
                  pl.BlockSpec((D_MODEL, bn), lambda j: (0, j)),
                  pl.BlockSpec((1, bn), lambda j: (0, j)),
                  pl.BlockSpec((rows, D_MODEL), lambda j: (j, 0)),
                  pl.BlockSpec((n_gate, D_MODEL), lambda j: (n_main // n_gate, 0))],
        out_specs=[pl.BlockSpec((8, bn), lambda j: (0, j)),
                   pl.BlockSpec((D_MODEL, rows), lambda j: (0, j)),
                   pl.BlockSpec((D_MODEL, DH), lambda j: (0, 0))],
        out_shape=[jax.ShapeDtypeStruct((8, n_out), F32),
                   jax.ShapeDtypeStruct((D_MODEL, n_main), BF16),
                   jax.ShapeDtypeStruct((D_MODEL, DH), BF16)],
        compiler_params=pltpu.CompilerParams(vmem_limit_bytes=VMEM_LIMIT),
        name="mod",
    )(c_rows, w_ada, b_ada, w_in_t, w_in_t)


def _inproj_kernel(x_ref, xp_ref, xn_ref, mod_ref, win_ref, wg_ref,
                   lng_ref, ws_ref, bsb_ref, cw_ref, cb_ref, gbi_ref, gbf_ref, ng_ref,
                   *rest, tm, convert, mixer):
    if convert:
        wup_ref, wdown_ref, wout_ref = rest[:3]
        wupb_ref, wdownb_ref, woutb_ref = rest[-3:]
        rest = rest[3:-3]
        start = 0
        for blk in FFN_BLOCKS:
            wupb_ref[:, 2 * start:2 * start + blk] = wup_ref[:, start:start + blk].astype(BF16)
            wupb_ref[:, 2 * start + blk:2 * (start + blk)] = wup_ref[:, D_FF + start:D_FF + start + blk].astype(BF16)
            start += blk
        wdownb_ref[...] = wdown_ref[...].astype(BF16)
        woutb_ref[...] = wout_ref[...].astype(BF16)
    yan_ref, q_ref, kt_ref, vm_ref, o_ref, gb_ref, gcm_ref, gct_ref, gwk_ref = rest
    i = pl.program_id(1)
    n_tiles = pl.num_programs(1)
    shift = mod_ref[0:1, :]
    scale = mod_ref[1:2, :]

    def modulate(xv):
        return (xv * scale + shift).astype(BF16)

    hx = modulate(x_ref[...])
    h_prev = jnp.where(i > 0, modulate(xp_ref[...]), jnp.zeros((), BF16))
    h_next = jnp.where(i < n_tiles - 1, modulate(xn_ref[...]), jnp.zeros((), BF16))
    gp = _dot(hx, wg_ref[...])
    uv = _dot(hx, win_ref[:, 0:2 * HALF_W]) if mixer else None
    ext = _dot(jnp.concatenate([h_prev, hx, h_next], axis=0), win_ref[:, 2 * HALF_W:4 * HALF_W])
    vo = _dot(hx, win_ref[:, 4 * HALF_W:(6 if mixer else 5) * HALF_W])

    if mixer:
        u = _gelu_tanh(uv[:, :HALF_W])
        v = _gelu_tanh(uv[:, HALF_W:])
    else:
        yan_ref[...] = jnp.zeros(yan_ref.shape, yan_ref.dtype)
        o_ref[...] = jnp.zeros(o_ref.shape, o_ref.dtype)
    for g in range(HEADS if mixer else 0):
        lanes = slice(g * DH, (g + 1) * DH)
        vh = (_norm_lanes(v[:, lanes]) * lng_ref[:, lanes]).astype(BF16)
        for c in range(tm // CHUNK):
            rows = slice(c * CHUNK, (c + 1) * CHUNK)
            mixed = _dot(ws_ref[g], vh[rows]) + bsb_ref[:, lanes]
            ya = u[rows, lanes] * mixed
            yan_ref[rows, lanes] = (_norm_lanes(ya) * ng_ref[:, lanes]).astype(BF16)

    n_ext = tm + 2 * HALO
    pre = ext[HALO:HALO + tm]
    down = pltpu.roll(ext, 1, 0)[HALO:HALO + tm]
    up = pltpu.roll(ext, n_ext - 1, 0)[HALO:HALO + tm]
    conv = down * cw_ref[0:1, :] + pre * cw_ref[1:2, :] + up * cw_ref[2:3, :] + cb_ref[...]
    qk = conv * _sigmoid(conv)
    q_ref[...] = qk[:, :HALF_W].astype(BF16)
    kt_ref[...] = (qk[:, HALF_W:] * (DH ** -0.5)).T.astype(BF16)

    vm_ref[...] = vo[:, :HALF_W].astype(BF16)
    if mixer:
        o_ref[...] = vo[:, HALF_W:]

    pos = lax.broadcasted_iota(jnp.int32, (N_HD, CHUNK), 1)
    is_fwd = lax.broadcasted_iota(jnp.int32, (N_HD, CHUNK), 0) < HEADS
    pad = jnp.zeros((DH - N_HD, CHUNK), F32)

    def scan_lanes(z, op, fill, reverse):
        step = 1
        while step < CHUNK:
            if reverse:
                moved = jnp.where(pos < CHUNK - step, pltpu.roll(z, CHUNK - step, 1), fill)
            else:
                moved = jnp.where(pos >= step, pltpu.roll(z, step, 1), fill)
            z = op(z, moved)
            step *= 2
        return z

    for c in range(tm // CHUNK):
        rows = slice(c * CHUNK, (c + 1) * CHUNK)
        gt = gp[rows].T
        log_i = gt[0:N_HD] + gbi_ref[...]
        lf = _log_sigmoid(gt[N_HD:2 * N_HD] + gbf_ref[...])
        csum = scan_lanes(lf, jnp.add, 0.0, False)
        suffix = csum[:, CHUNK - 1:CHUNK] - csum + lf
        b = jnp.where(is_fwd, csum, suffix)
        cdiff = b - log_i
        cmin = jnp.where(is_fwd, scan_lanes(cdiff, jnp.minimum, jnp.inf, False),
                         scan_lanes(cdiff, jnp.minimum, jnp.inf, True))
        cmin_end = jnp.where(is_fwd[:, 0:1], cmin[:, CHUNK - 1:CHUNK], cmin[:, 0:1])
        gct_ref[:, rows] = cdiff
        gwk_ref[:, rows] = jnp.exp(cmin_end - cdiff)
        gb_ref[rows, :] = jnp.concatenate([b, pad], axis=0).T
        gcm_ref[rows, :] = jnp.concatenate([cmin, pad], axis=0).T


def _inproj(xs, mod6, wts, *, tm, ffn_f32=(), mixer=True):
    B, S, _ = xs.shape
    nt = S // tm
    hb = tm // HALO
    n_halo_blocks = S // HALO
    kern = functools.partial(_inproj_kernel, tm=tm, convert=bool(ffn_f32), mixer=mixer)
    tok = lambda w: pl.BlockSpec((None, tm, w), lambda b, i: (b, i, 0))
    in_specs = [
        tok(D_MODEL),
        pl.BlockSpec((None, HALO, D_MODEL), lambda b, i: (b, jnp.maximum(i * hb - 1, 0), 0)),
        pl.BlockSpec((None, HALO, D_MODEL), lambda b, i: (b, jnp.minimum((i + 1) * hb, n_halo_blocks - 1), 0)),
        pl.BlockSpec((None, 6, D_MODEL), lambda b, i: (b, 0, 0)),
    ] + [_const_spec(w.shape) for w in wts]
    out_shape = [
        jax.ShapeDtypeStruct((B, S, HALF_W), BF16),
        jax.ShapeDtypeStruct((B, S, HALF_W), BF16),
        jax.ShapeDtypeStruct((B, HALF_W, S), BF16),
        jax.ShapeDtypeStruct((B, S, HALF_W), BF16),
        jax.ShapeDtypeStruct((B, S, HALF_W), F32),
        jax.ShapeDtypeStruct((B, S, DH), F32),
        jax.ShapeDtypeStruct((B, S, DH), F32),
        jax.ShapeDtypeStruct((B, N_HD, S), F32),
        jax.ShapeDtypeStruct((B, N_HD, S), F32),
    ]
    out_specs = [
        tok(HALF_W), tok(HALF_W),
        pl.BlockSpec((None, HALF_W, tm), lambda b, i: (b, 0, i)),
        tok(HALF_W), tok(HALF_W), tok(DH), tok(DH),
        pl.BlockSpec((None, N_HD, tm), lambda b, i: (b, 0, i)),
        pl.BlockSpec((None, N_HD, tm), lambda b, i: (b, 0, i)),
    ]
    if ffn_f32:
        for w, n_blocks in zip(ffn_f32, (B * nt, 8, B * nt)):
            rows = w.shape[0] // n_blocks
            assert rows * n_blocks == w.shape[0] and rows % 16 == 0 and n_blocks <= B * nt
            spec = pl.BlockSpec((rows, w.shape[1]), lambda b, i, n=n_blocks: (jnp.minimum(b * nt + i, n - 1), 0))
            in_specs.append(spec)
            out_specs.append(spec)
            out_shape.append(jax.ShapeDtypeStruct(w.shape, BF16))
    return pl.pallas_call(
        kern, grid=(B, nt), in_specs=in_specs, out_specs=out_specs, out_shape=out_shape,
        compiler_params=pltpu.CompilerParams(
            dimension_semantics=("arbitrary", "arbitrary"), vmem_limit_bytes=VMEM_LIMIT),
        name="inproj",
    )(xs, xs, xs, mod6, *wts, *ffn_f32)


def _scan_kernel(qf_ref, ktf_ref, vf_ref, gbf_ref, gcmf_ref, gctf_ref, gwkf_ref,
                 qb_ref, ktb_ref, vb_ref, gbb_ref, gcmb_ref, gctb_ref, gwkb_ref,
                 *rest, zero_init, cps):
    if zero_init:
        hf_ref, hb_ref, cst_ref, mst_ref = rest
    else:
        c0_ref, m0_ref, hf_ref, hb_ref, cst_ref, mst_ref = rest

    @pl.when(pl.program_id(1) == 0)
    def _():
        if zero_init:
            cst_ref[...] = jnp.zeros(cst_ref.shape, F32)
            mst_ref[...] = jnp.zeros(mst_ref.shape, F32)
        else:
            cst_ref[...] = c0_ref[...]
            mst_ref[...] = m0_ref[...]

    t_idx = lax.broadcasted_iota(jnp.int32, (CHUNK, CHUNK), 0)
    s_idx = lax.broadcasted_iota(jnp.int32, (CHUNK, CHUNK), 1)
    is_fwd = lax.broadcasted_iota(jnp.int32, (CHUNK, DH), 1) < HEADS
    ones = jnp.ones((CHUNK, DH), BF16)
    f_rows = [slice(k * CHUNK, (k + 1) * CHUNK) for k in range(cps)]
    b_rows = f_rows[::-1]

    terms = []
    m_row = mst_ref[0:1, :]
    for fr, br in zip(f_rows, b_rows):
        b = jnp.where(is_fwd, gbf_ref[fr, :], gbb_ref[br, :])
        cmin = jnp.where(is_fwd, gcmf_ref[fr, :], gcmb_ref[br, :])
        total = jnp.where(is_fwd[0:1], gbf_ref[fr.stop - 1:fr.stop, :], gbb_ref[br.start:br.start + 1, :])
        cmin_end = jnp.where(is_fwd[0:1], gcmf_ref[fr.stop - 1:fr.stop, :], gcmb_ref[br.start:br.start + 1, :])
        rmax = b - cmin
        inter = b + m_row
        m_t = jnp.maximum(inter, rmax)
        gmax = total - cmin_end
        m_new = jnp.maximum(total + m_row, gmax)
        terms.append(dict(lead=cmin + (rmax - m_t), w_state=jnp.exp(inter - m_t),
                          floor=jnp.exp(-m_t), w_old=jnp.exp(total + m_row - m_new), w_new=jnp.exp(gmax - m_new)))
        m_row = m_new
    mst_ref[...] = jnp.broadcast_to(m_row, (N_HD, DH))

    pairs = []
    for j in range(N_HD):
        fwd = j < HEADS
        lanes = slice((j % HEADS) * DH, (j % HEADS + 1) * DH)
        refs = ((qf_ref, ktf_ref, vf_ref, gctf_ref, gwkf_ref, hf_ref) if fwd else
                (qb_ref, ktb_ref, vb_ref, gctb_ref, gwkb_ref, hb_ref))
        pairs.append((fwd, lanes, f_rows if fwd else b_rows) + refs)

    scores, updates, v_augs = [], [], []
    for k in range(cps):
        scores.append([_dot(q_ref[rows[k], lanes], kt_ref[lanes, rows[k]])
                       for (fwd, lanes, rows, q_ref, kt_ref, v_ref, gct_ref, gwk_ref, out_ref) in pairs])
    for k in range(cps):
        v_augs.append([jnp.concatenate([v_ref[rows[k], lanes], ones], axis=1)
                       for (fwd, lanes, rows, q_ref, kt_ref, v_ref, gct_ref, gwk_ref, out_ref) in pairs])
        updates.append([_dot((kt_ref[lanes, rows[k]].astype(F32) * gwk_ref[j:j + 1, rows[k]]).astype(BF16), v_augs[k][j])
                        for j, (fwd, lanes, rows, q_ref, kt_ref, v_ref, gct_ref, gwk_ref, out_ref) in enumerate(pairs)])
    states = [cst_ref[j] for j in range(N_HD)]
    for k in range(cps):
        tk = terms[k]
        from_state = [_dot(q_ref[rows[k], lanes], states[j].astype(BF16))
                      for j, (fwd, lanes, rows, q_ref, kt_ref, v_ref, gct_ref, gwk_ref, out_ref) in enumerate(pairs)]
        states = [tk["w_old"][:, j:j + 1] * states[j] + tk["w_new"][:, j:j + 1] * updates[k][j] for j in range(N_HD)]
        for j, (fwd, lanes, rows, q_ref, kt_ref, v_ref, gct_ref, gwk_ref, out_ref) in enumerate(pairs):
            keep = (s_idx <= t_idx) if fwd else (s_idx >= t_idx)
            decay = jnp.exp(jnp.where(keep, tk["lead"][:, j:j + 1] - gct_ref[j:j + 1, rows[k]], -jnp.inf))
            sv = _dot((scores[k][j] * decay).astype(BF16), v_augs[k][j])
            ws = jnp.broadcast_to(tk["w_state"][:, j:j + 1], (CHUNK, DH))
            num = sv[:, :DH] + ws * from_state[j][:, :DH]
            den = sv[:, DH:] + ws * from_state[j][:, DH:]
            out_ref[rows[k], lanes] = num / jnp.maximum(jnp.abs(den), tk["floor"][:, j:j + 1])
    for j in range(N_HD):
        cst_ref[j] = states[j]


def _scan(q, kt, v, gb, gcm, gct, gwk, init=None):
    B, S, _ = q.shape
    cps = 2
    rows = cps * CHUNK
    nb = S // rows
    fwd_tok = lambda w: pl.BlockSpec((None, rows, w), lambda b, i: (b, i, 0))
    bwd_tok = lambda w: pl.BlockSpec((None, rows, w), lambda b, i: (b, nb - 1 - i, 0))
    fwd_t = lambda r: pl.BlockSpec((None, r, rows), lambda b, i: (b, 0, i))
    bwd_t = lambda r: pl.BlockSpec((None, r, rows), lambda b, i: (b, 0, nb - 1 - i))
    c_spec = pl.BlockSpec((None, N_HD, DH, 2 * DH), lambda b, i: (b, 0, 0, 0))
    m_spec = pl.BlockSpec((None, N_HD, DH), lambda b, i: (b, 0, 0))
    init = () if init is None else tuple(init)
    return pl.pallas_call(
        functools.partial(_scan_kernel, zero_init=not init, cps=cps), grid=(B, nb),
        in_specs=[fwd_tok(HALF_W), fwd_t(HALF_W), fwd_tok(HALF_W), fwd_tok(DH), fwd_tok(DH), fwd_t(N_HD), fwd_t(N_HD),
                  bwd_tok(HALF_W), bwd_t(HALF_W), bwd_tok(HALF_W), bwd_tok(DH), bwd_tok(DH), bwd_t(N_HD), bwd_t(N_HD),
                  ] + [c_spec, m_spec][:len(init)],
        out_specs=[fwd_tok(HALF_W), bwd_tok(HALF_W), c_spec, m_spec],
        out_shape=[jax.ShapeDtypeStruct((B, S, HALF_W), F32),
                   jax.ShapeDtypeStruct((B, S, HALF_W), F32),
                   jax.ShapeDtypeStruct((B, N_HD, DH, 2 * DH), F32),
                   jax.ShapeDtypeStruct((B, N_HD, DH), F32)],
        compiler_params=pltpu.CompilerParams(
            dimension_semantics=("arbitrary", "arbitrary"), vmem_limit_bytes=VMEM_LIMIT),
        name="scan",
    )(q, kt, v, gb, gcm, gct, gwk, q, kt, v, gb, gcm, gct, gwk, *init)


def _layer_norm(z, g_ref, b_ref):
    return _norm_lanes(z) * g_ref[...] + b_ref[...]


def _ffn_kernel(x_ref, yan_ref, hf_ref, hb_ref, o_ref, mod_ref, ng_ref, wout_ref,
                ln1g_ref, ln1b_ref, wup_ref, cw_ref, cb_ref, wdown_ref, ln2g_ref, ln2b_ref,
                out_ref, perm_ref, *, tm, n_sub):
    ts = tm // n_sub
    ns = ts // NSLAB
    n_lt = D_MODEL // DH
    starts = [sum(FFN_BLOCKS[:k]) for k in range(len(FFN_BLOCKS))]
    subs = [slice(t * ts, (t + 1) * ts) for t in range(n_sub)]

    def mixer_in(rows):
        h = hf_ref[rows, :] + hb_ref[rows, :]
        gated = []
        for g in range(HEADS):
            lanes = slice(g * DH, (g + 1) * DH)
            gated.append((_norm_lanes(h[:, lanes]) * ng_ref[:, lanes] * _sigmoid(o_ref[rows, lanes])).astype(BF16))
        return jnp.concatenate([yan_ref[rows, :]] + gated, axis=1)

    def permuted_x1(t, mix):
        x1n = _layer_norm(ALPHA * x_ref[subs[t], :] + mod_ref[2:3, :] * mix, ln1g_ref, ln1b_ref)
        for c in range(n_lt):
            perm_ref[t * n_lt + c] = x1n[:, c * DH:(c + 1) * DH]
        return jnp.concatenate(
            [jnp.concatenate([perm_ref[t * n_lt + c, pl.ds(j, ns, stride=NSLAB), :] for c in range(n_lt)], axis=1)
             for j in range(NSLAB)], axis=0)

    def up(h2, k):
        return _dot(h2, wup_ref[:, 2 * starts[k]:2 * (starts[k] + FFN_BLOCKS[k])])

    def hidden(a, k):
        blk = FFN_BLOCKS[k]
        cols = slice(2 * starts[k], 2 * (starts[k] + blk))
        q8 = lax.broadcasted_iota(jnp.int32, (ns, 2 * blk), 0) % NSLAB
        before_first = jnp.where(q8 == 0, 0.0, pltpu.roll(a[ts - ns:], 1, 0))
        after_last = jnp.where(q8 == NSLAB - 1, 0.0, pltpu.roll(a[:ns], ns - 1, 0))
        left = jnp.concatenate([before_first, a[:ts - ns]], axis=0)
        right = jnp.concatenate([a[ns:], after_last], axis=0)
        a = left * cw_ref[0:1, cols] + a * cw_ref[1:2, cols] + right * cw_ref[2:3, cols] + cb_ref[:, cols]
        prod = a[:, blk:] * a[:, :blk]
        return (prod + prod * jnp.tanh(a[:, blk:])).astype(BF16)

    def finish(t, x1, acc):
        out = _layer_norm(ALPHA * x1 + mod_ref[5:6, :] * acc, ln2g_ref, ln2b_ref)
        for c in range(n_lt):
            for j in range(NSLAB):
                perm_ref[t * n_lt + c, pl.ds(j, ns, stride=NSLAB), :] = out[j * ns:(j + 1) * ns, c * DH:(c + 1) * DH]
        for c in range(n_lt):
            out_ref[subs[t], c * DH:(c + 1) * DH] = perm_ref[t * n_lt + c]

    mixes = [_dot(mixer_in(rows), wout_ref[...]) for rows in subs]
    x1s = [permuted_x1(t, mixes[t]) for t in range(n_sub)]
    h2s = [(x1 * mod_ref[4:5, :] + mod_ref[3:4, :]).astype(BF16) for x1 in x1s]

    steps = [(t, k) for t in range(n_sub) for k in range(len(FFN_BLOCKS))]
    accs = [None] * n_sub
    pending = up(h2s[0], 0)
    for idx, (t, k) in enumerate(steps):
        following = up(h2s[steps[idx + 1][0]], steps[idx + 1][1]) if idx + 1 < len(steps) else None
        part = _dot(hidden(pending, k), wdown_ref[starts[k]:starts[k] + FFN_BLOCKS[k], :])
        accs[t] = part if accs[t] is None else accs[t] + part
        pending = following
        if k == len(FFN_BLOCKS) - 1:
            finish(t, x1s[t], accs[t])


def _ffn(x, yan, hf, hb, o_pre, mod6, wts, *, tm, n_sub):
    B, S, _ = x.shape
    kern = functools.partial(_ffn_kernel, tm=tm, n_sub=n_sub)
    tok = lambda w: pl.BlockSpec((None, tm, w), lambda b, i: (b, i, 0))
    in_specs = [tok(D_MODEL), tok(HALF_W), tok(HALF_W), tok(HALF_W), tok(HALF_W),
                pl.BlockSpec((None, 6, D_MODEL), lambda b, i: (b, 0, 0))] + [_const_spec(w.shape) for w in wts]
    return pl.pallas_call(
        kern, grid=(B, S // tm), in_specs=in_specs, out_specs=tok(D_MODEL),
        out_shape=jax.ShapeDtypeStruct((B, S, D_MODEL), F32),
        scratch_shapes=[pltpu.VMEM((n_sub * (D_MODEL // DH), tm // n_sub, DH), F32)],
        compiler_params=pltpu.CompilerParams(
            dimension_semantics=("arbitrary", "arbitrary"), vmem_limit_bytes=VMEM_LIMIT),
        name="ffn",
    )(x, yan, hf, hb, o_pre, mod6, *wts)


def _interleave_blocks(val, gate, axis):
    parts, start = [], 0
    for blk in FFN_BLOCKS:
        parts.append(lax.slice_in_dim(val, start, start + blk, axis=axis))
        parts.append(lax.slice_in_dim(gate, start, start + blk, axis=axis))
        start += blk
    return jnp.concatenate(parts, axis=axis)


def kernel(x, c, ctx, c_ctx, w_ada, b_ada, w_in, gmlp_ln_g, gmlp_ws, gmlp_bs, qk_conv_w, qk_conv_b,
           b_igate, b_fgate, mix_norm_g, w_out, ln1_g, ln1_b, w_up, ffn_conv_w, ffn_conv_b, w_down,
           ln2_g, ln2_b):
    B, S, _ = x.shape
    assert DEPTH == 1 and w_in.shape[0] == 1
    l = 0

    c_rows = jnp.concatenate([c, c_ctx[None, :], jnp.zeros((8 - B - 1, D_MODEL), F32)], axis=0)
    chunk = jnp.arange(6 * D_MODEL) // D_MODEL
    bias = b_ada[l] + jnp.where((chunk == 1) | (chunk == 4), 1.0, 0.0)
    mod, w_in_b, w_g = _modulation(c_rows, w_ada[l], bias[None, :], jnp.swapaxes(w_in[l], 0, 1))
    mod_x = mod[:B].reshape(B, 6, D_MODEL)
    mod_c = jnp.broadcast_to(mod[B].reshape(1, 6, D_MODEL), (B, 6, D_MODEL))

    gbi = jnp.broadcast_to(b_igate[l].reshape(N_HD, 1), (N_HD, CHUNK))
    gbf = jnp.broadcast_to(b_fgate[l].reshape(N_HD, 1), (N_HD, CHUNK))
    ln_g = gmlp_ln_g[l].reshape(1, HALF_W)
    ws = gmlp_ws[l].astype(BF16)
    bsb = jnp.repeat(gmlp_bs[l].T, DH, axis=1)
    norm_g = mix_norm_g[l][None, :]
    in_wts = (w_in_b, w_g, ln_g, ws, bsb, qk_conv_w[l], qk_conv_b[l][None, :], gbi, gbf,
              norm_g[:, :HALF_W])

    _, q_c, kt_c, v_c, _, *gates_c = _inproj(ctx, mod_c, in_wts, tm=ctx.shape[1], mixer=False)
    _, _, c_state, m_state = _scan(q_c, kt_c, v_c, *gates_c)

    yan, q, kt, v, o_pre, *rest = _inproj(x, mod_x, in_wts, tm=512, ffn_f32=(w_up[l], w_down[l], w_out[l]))
    gates, (w_up_b, w_down_b, w_out_b) = rest[:4], rest[4:]
    hf, hb, _, _ = _scan(q, kt, v, *gates, init=(c_state, m_state))

    cw = _interleave_blocks(ffn_conv_w[l][:, :D_FF], 0.5 * ffn_conv_w[l][:, D_FF:], 1)
    cb = _interleave_blocks(ffn_conv_b[l][None, :D_FF], 0.5 * ffn_conv_b[l][None, D_FF:], 1)
    ffn_wts = (norm_g[:, HALF_W:], w_out_b, ln1_g[l][None, :], ln1_b[l][None, :],
               w_up_b, cw, cb, w_down_b, ln2_g[l][None, :], ln2_b[l][None, :])
    return _ffn(x, yan, hf, hb, o_pre, mod_x, ffn_wts, tm=512, n_sub=2)
```

```python
import functools

import jax
import jax.numpy as jnp
from jax import lax
from jax.experimental import pallas as pl
from jax.experimental.pallas import tpu as pltpu

D_MODEL = 1024
GRID_W = 64
CHUNK = 128
HEADS = 4
DH = 128
HALF_W = HEADS * DH
N_HD = 2 * HEADS
D_FF = 2688
DEPTH = 1
ALPHA = (2 * DEPTH) ** 0.25
EPS = 1e-5
NSLAB = 8
HALO = 16
FFN_BLOCKS = (768, 768, 768, 384)
UP_AHEAD = 2

F32 = jnp.float32
BF16 = jnp.bfloat16
VMEM_LIMIT = 56 * 1024 * 1024


def _dot(a, b):
    return jnp.dot(a, b, preferred_element_type=F32)


def _norm_lanes(z):
    mu = jnp.mean(z, axis=-1, keepdims=True)
    d = z - mu
    var = jnp.mean(d * d, axis=-1, keepdims=True)
    return d * lax.rsqrt(var + EPS)


def _gelu_tanh(x):
    half = 0.5 * x
    return half + half * jnp.tanh(x * (0.7978845608028654 + 0.035677408136300125 * (x * x)))


def _sigmoid(x):
    return 0.5 * jnp.tanh(0.5 * x) + 0.5


def _log_sigmoid(x):
    return jnp.minimum(x, 0.0) - jnp.log(1.0 + jnp.exp(-jnp.abs(x)))


def _const_spec(shape):
    nd = len(shape)
    return pl.BlockSpec(shape, lambda *_: (0,) * nd, pipeline_mode=pl.Buffered(1))


def _mod_kernel(c_ref, w_ref, b_ref, win_ref, wg_ref, o_ref, winb_ref, wgb_ref):
    winb_ref[...] = win_ref[...].T.astype(BF16)
    wg = wg_ref[...]
    wg_rows = jnp.concatenate([wg, jnp.zeros((DH - wg.shape[0], wg.shape[1]), F32)], axis=0)
    wgb_ref[...] = wg_rows.T.astype(BF16)

    cs = c_ref[...]
    a = cs * _sigmoid(cs)
    a_hi = a.astype(BF16)
    a_lo = (a - a_hi.astype(F32)).astype(BF16)
    w = w_ref[...]
    w_hi = w.astype(BF16)
    w_lo = (w - w_hi.astype(F32)).astype(BF16)
    by_hi = _dot(jnp.concatenate([a_hi, a_lo], axis=0), w_hi)
    rows = a.shape[0]
    o_ref[...] = by_hi[:rows] + by_hi[rows:] + _dot(a_hi, w_lo) + b_ref[...]


def _modulation(c_rows, w_ada, b_ada, w_in_t):
    n_out = w_ada.shape[1]
    n_steps = 8
    bn = n_out // n_steps
    n_main = 6 * HALF_W
    n_gate = 2 * N_HD
    rows = n_main // n_steps
    assert w_in_t.shape == (n_main + n_gate, D_MODEL) and n_main % n_gate == 0
    return pl.pallas_call(
        _mod_kernel,
        grid=(n_steps,),
        in_specs=[pl.BlockSpec((8, D_MODEL), lambda j: (0, 0)),
                  pl.BlockSpec((D_MODEL, bn), lambda j: (0, j)),
                  pl.BlockSpec((1, bn), lambda j: (0, j)),
                  pl.BlockSpec((rows, D_MODEL), lambda j: (j, 0)),
                  pl.BlockSpec((n_gate, D_MODEL), lambda j: (n_main // n_gate, 0))],
        out_specs=[pl.BlockSpec((8, bn), lambda j: (0, j)),
                   pl.BlockSpec((D_MODEL, rows), lambda j: (0, j)),
                   pl.BlockSpec((D_MODEL, DH), lambda j: (0, 0))],
        out_shape=[jax.ShapeDtypeStruct((8, n_out), F32),
                   jax.ShapeDtypeStruct((D_MODEL, n_main), BF16),
                   jax.ShapeDtypeStruct((D_MODEL, DH), BF16)],
        compiler_params=pltpu.CompilerParams(vmem_limit_bytes=VMEM_LIMIT),
        name="mod",
    )(c_rows, w_ada, b_ada, w_in_t, w_in_t)


def _inproj_kernel(x_ref, xp_ref, xn_ref, mod_ref, win_ref, wg_ref,
                   lng_ref, ws_ref, bsb_ref, cw_ref, cb_ref, gbi_ref, gbf_ref, ng_ref,
                   *rest, tm, convert, mixer):
    if convert:
        wup_ref, wdown_ref, wout_ref = rest[:3]
        wupb_ref, wdownb_ref, woutb_ref = rest[-3:]
        rest = rest[3:-3]
        start = 0
        for blk in FFN_BLOCKS:
            wupb_ref[:, 2 * start:2 * start + blk] = wup_ref[:, start:start + blk].astype(BF16)
            wupb_ref[:, 2 * start + blk:2 * (start + blk)] = wup_ref[:, D_FF + start:D_FF + start + blk].astype(BF16)
            start += blk
        wdownb_ref[...] = wdown_ref[...].astype(BF16)
        woutb_ref[...] = wout_ref[...].astype(BF16)
    yan_ref, q_ref, kt_ref, vm_ref, o_ref, gb_ref, gcm_ref, gct_ref, gwk_ref = rest
    i = pl.program_id(1)
    n_tiles = pl.num_programs(1)
    shift = mod_ref[0:1, :]
    scale = mod_ref[1:2, :]

    def modulate(xv):
        return (xv * scale + shift).astype(BF16)

    hx = modulate(x_ref[...])
    h_prev = jnp.where(i > 0, modulate(xp_ref[...]), jnp.zeros((), BF16))
    h_next = jnp.where(i < n_tiles - 1, modulate(xn_ref[...]), jnp.zeros((), BF16))
    gp = _dot(hx, wg_ref[...])
    uv = _dot(hx, win_ref[:, 0:2 * HALF_W]) if mixer else None
    ext = _dot(jnp.concatenate([h_prev, hx, h_next], axis=0), win_ref[:, 2 * HALF_W:4 * HALF_W])
    vo = _dot(hx, win_ref[:, 4 * HALF_W:(6 if mixer else 5) * HALF_W])

    if mixer:
        u = _gelu_tanh(uv[:, :HALF_W])
        v = _gelu_tanh(uv[:, HALF_W:])
    else:
        yan_ref[...] = jnp.zeros(yan_ref.shape, yan_ref.dtype)
        o_ref[...] = jnp.zeros(o_ref.shape, o_ref.dtype)
    for g in range(HEADS if mixer else 0):
        lanes = slice(g * DH, (g + 1) * DH)
        vh = (_norm_lanes(v[:, lanes]) * lng_ref[:, lanes]).astype(BF16)
        for c in range(tm // CHUNK):
            rows = slice(c * CHUNK, (c + 1) * CHUNK)
            mixed = _dot(ws_ref[g], vh[rows]) + bsb_ref[:, lanes]
            ya = u[rows, lanes] * mixed
            yan_ref[rows, lanes] = (_norm_lanes(ya) * ng_ref[:, lanes]).astype(BF16)

    n_ext = tm + 2 * HALO
    pre = ext[HALO:HALO + tm]
    down = pltpu.roll(ext, 1, 0)[HALO:HALO + tm]
    up = pltpu.roll(ext, n_ext - 1, 0)[HALO:HALO + tm]
    conv = down * cw_ref[0:1, :] + pre * cw_ref[1:2, :] + up * cw_ref[2:3, :] + cb_ref[...]
    qk = conv * _sigmoid(conv)
    q_ref[...] = qk[:, :HALF_W].astype(BF16)
    kt_ref[...] = (qk[:, HALF_W:] * (DH ** -0.5)).T.astype(BF16)

    vm_ref[...] = vo[:, :HALF_W].astype(BF16)
    if mixer:
        o_ref[...] = vo[:, HALF_W:]

    pos = lax.broadcasted_iota(jnp.int32, (N_HD, CHUNK), 1)
    is_fwd = lax.broadcasted_iota(jnp.int32, (N_HD, CHUNK), 0) < HEADS
    pad = jnp.zeros((DH - N_HD, CHUNK), F32)

    def scan_lanes(z, op, fill, reverse):
        step = 1
        while step < CHUNK:
            if reverse:
                moved = jnp.where(pos < CHUNK - step, pltpu.roll(z, CHUNK - step, 1), fill)
            else:
                moved = jnp.where(pos >= step, pltpu.roll(z, step, 1), fill)
            z = op(z, moved)
            step *= 2
        return z

    for c in range(tm // CHUNK):
        rows = slice(c * CHUNK, (c + 1) * CHUNK)
        gt = gp[rows].T
        log_i = gt[0:N_HD] + gbi_ref[...]
        lf = _log_sigmoid(gt[N_HD:2 * N_HD] + gbf_ref[...])
        csum = scan_lanes(lf, jnp.add, 0.0, False)
        suffix = csum[:, CHUNK - 1:CHUNK] - csum + lf
        b = jnp.where(is_fwd, csum, suffix)
        cdiff = b - log_i
        cmin = jnp.where(is_fwd, scan_lanes(cdiff, jnp.minimum, jnp.inf, False),
                         scan_lanes(cdiff, jnp.minimum, jnp.inf, True))
        cmin_end = jnp.where(is_fwd[:, 0:1], cmin[:, CHUNK - 1:CHUNK], cmin[:, 0:1])
        gct_ref[:, rows] = cdiff
        gwk_ref[:, rows] = jnp.exp(cmin_end - cdiff)
        gb_ref[rows, :] = jnp.concatenate([b, pad], axis=0).T
        gcm_ref[rows, :] = jnp.concatenate([cmin, pad], axis=0).T


def _inproj(xs, mod6, wts, *, tm, ffn_f32=(), mixer=True):
    B, S, _ = xs.shape
    nt = S // tm
    hb = tm // HALO
    n_halo_blocks = S // HALO
    kern = functools.partial(_inproj_kernel, tm=tm, convert=bool(ffn_f32), mixer=mixer)
    tok = lambda w: pl.BlockSpec((None, tm, w), lambda b, i: (b, i, 0))
    in_specs = [
        tok(D_MODEL),
        pl.BlockSpec((None, HALO, D_MODEL), lambda b, i: (b, jnp.maximum(i * hb - 1, 0), 0)),
        pl.BlockSpec((None, HALO, D_MODEL), lambda b, i: (b, jnp.minimum((i + 1) * hb, n_halo_blocks - 1), 0)),
        pl.BlockSpec((None, 6, D_MODEL), lambda b, i: (b, 0, 0)),
    ] + [_const_spec(w.shape) for w in wts]
    out_shape = [
        jax.ShapeDtypeStruct((B, S, HALF_W), BF16),
        jax.ShapeDtypeStruct((B, S, HALF_W), BF16),
        jax.ShapeDtypeStruct((B, HALF_W, S), BF16),
        jax.ShapeDtypeStruct((B, S, HALF_W), BF16),
        jax.ShapeDtypeStruct((B, S, HALF_W), F32),
        jax.ShapeDtypeStruct((B, S, DH), F32),
        jax.ShapeDtypeStruct((B, S, DH), F32),
        jax.ShapeDtypeStruct((B, N_HD, S), F32),
        jax.ShapeDtypeStruct((B, N_HD, S), F32),
    ]
    out_specs = [
        tok(HALF_W), tok(HALF_W),
        pl.BlockSpec((None, HALF_W, tm), lambda b, i: (b, 0, i)),
        tok(HALF_W), tok(HALF_W), tok(DH), tok(DH),
        pl.BlockSpec((None, N_HD, tm), lambda b, i: (b, 0, i)),
        pl.BlockSpec((None, N_HD, tm), lambda b, i: (b, 0, i)),
    ]
    if ffn_f32:
        for w, n_blocks in zip(ffn_f32, (B * nt, 8, B * nt)):
            rows = w.shape[0] // n_blocks
            assert rows * n_blocks == w.shape[0] and rows % 16 == 0 and n_blocks <= B * nt
            spec = pl.BlockSpec((rows, w.shape[1]), lambda b, i, n=n_blocks: (jnp.minimum(b * nt + i, n - 1), 0))
            in_specs.append(spec)
            out_specs.append(spec)
            out_shape.append(jax.ShapeDtypeStruct(w.shape, BF16))
    return pl.pallas_call(
        kern, grid=(B, nt), in_specs=in_specs, out_specs=out_specs, out_shape=out_shape,
        compiler_params=pltpu.CompilerParams(
            dimension_semantics=("arbitrary", "arbitrary"), vmem_limit_bytes=VMEM_LIMIT),
        name="inproj",
    )(xs, xs, xs, mod6, *wts, *ffn_f32)


def _scan_kernel(qf_ref, ktf_ref, vf_ref, gbf_ref, gcmf_ref, gctf_ref, gwkf_ref,
                 qb_ref, ktb_ref, vb_ref, gbb_ref, gcmb_ref, gctb_ref, gwkb_ref,
                 *rest, zero_init, cps):
    if zero_init:
        hf_ref, hb_ref, cst_ref, mst_ref = rest
    else:
        c0_ref, m0_ref, hf_ref, hb_ref, cst_ref, mst_ref = rest

    @pl.when(pl.program_id(1) == 0)
    def _():
        if zero_init:
            cst_ref[...] = jnp.zeros(cst_ref.shape, F32)
            mst_ref[...] = jnp.zeros(mst_ref.shape, F32)
        else:
            cst_ref[...] = c0_ref[...]
            mst_ref[...] = m0_ref[...]

    t_idx = lax.broadcasted_iota(jnp.int32, (CHUNK, CHUNK), 0)
    s_idx = lax.broadcasted_iota(jnp.int32, (CHUNK, CHUNK), 1)
    is_fwd = lax.broadcasted_iota(jnp.int32, (CHUNK, DH), 1) < HEADS
    ones = jnp.ones((CHUNK, DH), BF16)
    f_rows = [slice(k * CHUNK, (k + 1) * CHUNK) for k in range(cps)]
    b_rows = f_rows[::-1]

    terms = []
    m_row = mst_ref[0:1, :]
    for fr, br in zip(f_rows, b_rows):
        b = jnp.where(is_fwd, gbf_ref[fr, :], gbb_ref[br, :])
        cmin = jnp.where(is_fwd, gcmf_ref[fr, :], gcmb_ref[br, :])
        total = jnp.where(is_fwd[0:1], gbf_ref[fr.stop - 1:fr.stop, :], gbb_ref[br.start:br.start + 1, :])
        cmin_end = jnp.where(is_fwd[0:1], gcmf_ref[fr.stop - 1:fr.stop, :], gcmb_ref[br.start:br.start + 1, :])
        rmax = b - cmin
        inter = b + m_row
        m_t = jnp.maximum(inter, rmax)
        gmax = total - cmin_end
        m_new = jnp.maximum(total + m_row, gmax)
        terms.append(dict(lead=cmin + (rmax - m_t), w_state=jnp.exp(inter - m_t),
                          floor=jnp.exp(-m_t), w_old=jnp.exp(total + m_row - m_new), w_new=jnp.exp(gmax - m_new)))
        m_row = m_new
    mst_ref[...] = jnp.broadcast_to(m_row, (N_HD, DH))

    pairs = []
    for j in range(N_HD):
        fwd = j < HEADS
        lanes = slice((j % HEADS) * DH, (j % HEADS + 1) * DH)
        refs = ((qf_ref, ktf_ref, vf_ref, gctf_ref, gwkf_ref, hf_ref) if fwd else
                (qb_ref, ktb_ref, vb_ref, gctb_ref, gwkb_ref, hb_ref))
        pairs.append((fwd, lanes, f_rows if fwd else b_rows) + refs)

    scores, updates, v_augs = [], [], []
    for k in range(cps):
        scores.append([_dot(q_ref[rows[k], lanes], kt_ref[lanes, rows[k]])
                       for (fwd, lanes, rows, q_ref, kt_ref, v_ref, gct_ref, gwk_ref, out_ref) in pairs])
    for k in range(cps):
        v_augs.append([jnp.concatenate([v_ref[rows[k], lanes], ones], axis=1)
                       for (fwd, lanes, rows, q_ref, kt_ref, v_ref, gct_ref, gwk_ref, out_ref) in pairs])
        updates.append([_dot((kt_ref[lanes, rows[k]].astype(F32) * gwk_ref[j:j + 1, rows[k]]).astype(BF16), v_augs[k][j])
                        for j, (fwd, lanes, rows, q_ref, kt_ref, v_ref, gct_ref, gwk_ref, out_ref) in enumerate(pairs)])
    states = [cst_ref[j] for j in range(N_HD)]
    for k in range(cps):
        tk = terms[k]
        from_state = [_dot(q_ref[rows[k], lanes], states[j].astype(BF16))
                      for j, (fwd, lanes, rows, q_ref, kt_ref, v_ref, gct_ref, gwk_ref, out_ref) in enumerate(pairs)]
        states = [tk["w_old"][:, j:j + 1] * states[j] + tk["w_new"][:, j:j + 1] * updates[k][j] for j in range(N_HD)]
        for j, (fwd, lanes, rows, q_ref, kt_ref, v_ref, gct_ref, gwk_ref, out_ref) in enumerate(pairs):
            keep = (s_idx <= t_idx) if fwd else (s_idx >= t_idx)
            decay = jnp.exp(jnp.where(keep, tk["lead"][:, j:j + 1] - gct_ref[j:j + 1, rows[k]], -jnp.inf))
            sv = _dot((scores[k][j] * decay).astype(BF16), v_augs[k][j])
            ws = jnp.broadcast_to(tk["w_state"][:, j:j + 1], (CHUNK, DH))
            num = sv[:, :DH] + ws * from_state[j][:, :DH]
            den = sv[:, DH:] + ws * from_state[j][:, DH:]
            out_ref[rows[k], lanes] = num / jnp.maximum(jnp.abs(den), tk["floor"][:, j:j + 1])
    for j in range(N_HD):
        cst_ref[j] = states[j]


def _scan(q, kt, v, gb, gcm, gct, gwk, init=None):
    B, S, _ = q.shape
    cps = min(4, S // CHUNK)
    rows = cps * CHUNK
    nb = S // rows
    fwd_tok = lambda w: pl.BlockSpec((None, rows, w), lambda b, i: (b, i, 0))
    bwd_tok = lambda w: pl.BlockSpec((None, rows, w), lambda b, i: (b, nb - 1 - i, 0))
    fwd_t = lambda r: pl.BlockSpec((None, r, rows), lambda b, i: (b, 0, i))
    bwd_t = lambda r: pl.BlockSpec((None, r, rows), lambda b, i: (b, 0, nb - 1 - i))
    c_spec = pl.BlockSpec((None, N_HD, DH, 2 * DH), lambda b, i: (b, 0, 0, 0))
    m_spec = pl.BlockSpec((None, N_HD, DH), lambda b, i: (b, 0, 0))
    init = () if init is None else tuple(init)
    return pl.pallas_call(
        functools.partial(_scan_kernel, zero_init=not init, cps=cps), grid=(B, nb),
        in_specs=[fwd_tok(HALF_W), fwd_t(HALF_W), fwd_tok(HALF_W), fwd_tok(DH), fwd_tok(DH), fwd_t(N_HD), fwd_t(N_HD),
                  bwd_tok(HALF_W), bwd_t(HALF_W), bwd_tok(HALF_W), bwd_tok(DH), bwd_tok(DH), bwd_t(N_HD), bwd_t(N_HD),
                  ] + [c_spec, m_spec][:len(init)],
        out_specs=[fwd_tok(HALF_W), bwd_tok(HALF_W), c_spec, m_spec],
        out_shape=[jax.ShapeDtypeStruct((B, S, HALF_W), F32),
                   jax.ShapeDtypeStruct((B, S, HALF_W), F32),
                   jax.ShapeDtypeStruct((B, N_HD, DH, 2 * DH), F32),
                   jax.ShapeDtypeStruct((B, N_HD, DH), F32)],
        compiler_params=pltpu.CompilerParams(
            dimension_semantics=("arbitrary", "arbitrary"), vmem_limit_bytes=VMEM_LIMIT),
        name="scan",
    )(q, kt, v, gb, gcm, gct, gwk, q, kt, v, gb, gcm, gct, gwk, *init)


def _layer_norm(z, g_ref, b_ref):
    return _norm_lanes(z) * g_ref[...] + b_ref[...]


def _ffn_kernel(x_ref, yan_ref, hf_ref, hb_ref, o_ref, mod_ref, ng_ref, wout_ref,
                ln1g_ref, ln1b_ref, wup_ref, cw_ref, cb_ref, wdown_ref, ln2g_ref, ln2b_ref,
                out_ref, perm_ref, *, tm, n_sub):
    ts = tm // n_sub
    ns = ts // NSLAB
    n_lt = D_MODEL // DH
    starts = [sum(FFN_BLOCKS[:k]) for k in range(len(FFN_BLOCKS))]
    subs = [slice(t * ts, (t + 1) * ts) for t in range(n_sub)]

    def mixer_in(rows):
        h = hf_ref[rows, :] + hb_ref[rows, :]
        gated = []
        for g in range(HEADS):
            lanes = slice(g * DH, (g + 1) * DH)
            gated.append((_norm_lanes(h[:, lanes]) * ng_ref[:, lanes] * _sigmoid(o_ref[rows, lanes])).astype(BF16))
        return jnp.concatenate([yan_ref[rows, :]] + gated, axis=1)

    def permuted_x1(t, mix):
        x1n = _layer_norm(ALPHA * x_ref[subs[t], :] + mod_ref[2:3, :] * mix, ln1g_ref, ln1b_ref)
        for c in range(n_lt):
            perm_ref[t * n_lt + c] = x1n[:, c * DH:(c + 1) * DH]
        return jnp.concatenate(
            [jnp.concatenate([perm_ref[t * n_lt + c, pl.ds(j, ns, stride=NSLAB), :] for c in range(n_lt)], axis=1)
             for j in range(NSLAB)], axis=0)

    def up(h2, k):
        return _dot(h2, wup_ref[:, 2 * starts[k]:2 * (starts[k] + FFN_BLOCKS[k])])

    def hidden(a, k):
        blk = FFN_BLOCKS[k]
        cols = slice(2 * starts[k], 2 * (starts[k] + blk))
        q8 = lax.broadcasted_iota(jnp.int32, (ns, 2 * blk), 0) % NSLAB
        before_first = jnp.where(q8 == 0, 0.0, pltpu.roll(a[ts - ns:], 1, 0))
        after_last = jnp.where(q8 == NSLAB - 1, 0.0, pltpu.roll(a[:ns], ns - 1, 0))
        left = jnp.concatenate([before_first, a[:ts - ns]], axis=0)
        right = jnp.concatenate([a[ns:], after_last], axis=0)
        a = left * cw_ref[0:1, cols] + a * cw_ref[1:2, cols] + right * cw_ref[2:3, cols] + cb_ref[:, cols]
        prod = a[:, blk:] * a[:, :blk]
        return (prod + prod * jnp.tanh(a[:, blk:])).astype(BF16)

    def finish(t, x1, acc):
        out = _layer_norm(ALPHA * x1 + mod_ref[5:6, :] * acc, ln2g_ref, ln2b_ref)
        for c in range(n_lt):
            for j in range(NSLAB):
                perm_ref[t * n_lt + c, pl.ds(j, ns, stride=NSLAB), :] = out[j * ns:(j + 1) * ns, c * DH:(c + 1) * DH]
        for c in range(n_lt):
            out_ref[subs[t], c * DH:(c + 1) * DH] = perm_ref[t * n_lt + c]

    mixes = [_dot(mixer_in(rows), wout_ref[...]) for rows in subs]
    x1s = [permuted_x1(t, mixes[t]) for t in range(n_sub)]
    h2s = [(x1 * mod_ref[4:5, :] + mod_ref[3:4, :]).astype(BF16) for x1 in x1s]

    steps = [(t, k) for t in range(n_sub) for k in range(len(FFN_BLOCKS))]
    accs = [None] * n_sub
    queue = [up(h2s[t], k) for (t, k) in steps[:UP_AHEAD]]
    for idx, (t, k) in enumerate(steps):
        if idx + UP_AHEAD < len(steps):
            queue.append(up(h2s[steps[idx + UP_AHEAD][0]], steps[idx + UP_AHEAD][1]))
        part = _dot(hidden(queue.pop(0), k), wdown_ref[starts[k]:starts[k] + FFN_BLOCKS[k], :])
        accs[t] = part if accs[t] is None else accs[t] + part
        if k == len(FFN_BLOCKS) - 1:
            finish(t, x1s[t], accs[t])


def _ffn(x, yan, hf, hb, o_pre, mod6, wts, *, tm, n_sub):
    B, S, _ = x.shape
    kern = functools.partial(_ffn_kernel, tm=tm, n_sub=n_sub)
    tok = lambda w: pl.BlockSpec((None, tm, w), lambda b, i: (b, i, 0))
    in_specs = [tok(D_MODEL), tok(HALF_W), tok(HALF_W), tok(HALF_W), tok(HALF_W),
                pl.BlockSpec((None, 6, D_MODEL), lambda b, i: (b, 0, 0))] + [_const_spec(w.shape) for w in wts]
    return pl.pallas_call(
        kern, grid=(B, S // tm), in_specs=in_specs, out_specs=tok(D_MODEL),
        out_shape=jax.ShapeDtypeStruct((B, S, D_MODEL), F32),
        scratch_shapes=[pltpu.VMEM((n_sub * (D_MODEL // DH), tm // n_sub, DH), F32)],
        compiler_params=pltpu.CompilerParams(
            dimension_semantics=("arbitrary", "arbitrary"), vmem_limit_bytes=VMEM_LIMIT),
        name="ffn",
    )(x, yan, hf, hb, o_pre, mod6, *wts)


def _interleave_blocks(val, gate, axis):
    parts, start = [], 0
    for blk in FFN_BLOCKS:
        parts.append(lax.slice_in_dim(val, start, start + blk, axis=axis))
        parts.append(lax.slice_in_dim(gate, start, start + blk, axis=axis))
        start += blk
    return jnp.concatenate(parts, axis=axis)


def kernel(x, c, ctx, c_ctx, w_ada, b_ada, w_in, gmlp_ln_g, gmlp_ws, gmlp_bs, qk_conv_w, qk_conv_b,
           b_igate, b_fgate, mix_norm_g, w_out, ln1_g, ln1_b, w_up, ffn_conv_w, ffn_conv_b, w_down,
           ln2_g, ln2_b):
    B, S, _ = x.shape
    assert DEPTH == 1 and w_in.shape[0] == 1
    l = 0

    c_rows = jnp.concatenate([c, c_ctx[None, :], jnp.zeros((8 - B - 1, D_MODEL), F32)], axis=0)
    chunk = jnp.arange(6 * D_MODEL) // D_MODEL
    bias = b_ada[l] + jnp.where((chunk == 1) | (chunk == 4), 1.0, 0.0)
    mod, w_in_b, w_g = _modulation(c_rows, w_ada[l], bias[None, :], jnp.swapaxes(w_in[l], 0, 1))
    mod_x = mod[:B].reshape(B, 6, D_MODEL)
    mod_c = jnp.broadcast_to(mod[B].reshape(1, 6, D_MODEL), (B, 6, D_MODEL))

    gbi = jnp.broadcast_to(b_igate[l].reshape(N_HD, 1), (N_HD, CHUNK))
    gbf = jnp.broadcast_to(b_fgate[l].reshape(N_HD, 1), (N_HD, CHUNK))
    ln_g = gmlp_ln_g[l].reshape(1, HALF_W)
    ws = gmlp_ws[l].astype(BF16)
    bsb = jnp.repeat(gmlp_bs[l].T, DH, axis=1)
    norm_g = mix_norm_g[l][None, :]
    in_wts = (w_in_b, w_g, ln_g, ws, bsb, qk_conv_w[l], qk_conv_b[l][None, :], gbi, gbf,
              norm_g[:, :HALF_W])

    _, q_c, kt_c, v_c, _, *gates_c = _inproj(ctx, mod_c, in_wts, tm=ctx.shape[1], mixer=False)
    _, _, c_state, m_state = _scan(q_c, kt_c, v_c, *gates_c)

    yan, q, kt, v, o_pre, *rest = _inproj(x, mod_x, in_wts, tm=512, ffn_f32=(w_up[l], w_down[l], w_out[l]))
    gates, (w_up_b, w_down_b, w_out_b) = rest[:4], rest[4:]
    hf, hb, _, _ = _scan(q, kt, v, *gates, init=(c_state, m_state))

    cw = _interleave_blocks(ffn_conv_w[l][:, :D_FF], 0.5 * ffn_conv_w[l][:, D_FF:], 1)
    cb = _interleave_blocks(ffn_conv_b[l][None, :D_FF], 0.5 * ffn_conv_b[l][None, D_FF:], 1)
    ffn_wts = (norm_g[:, HALF_W:], w_out_b, ln1_g[l][None, :], ln1_b[l][None, :],
               w_up_b, cw, cb, w_down_b, ln2_g[l][None, :], ln2_b[l][None, :])
    return _ffn(x, yan, hf, hb, o_pre, mod_x, ffn_wts, tm=512, n_sub=2)
```

```python
import functools

import jax
import jax.numpy as jnp
from jax import lax
from jax.experimental import pallas as pl
from jax.experimental.pallas import tpu as pltpu

D_MODEL = 1024
GRID_W = 64
CHUNK = 128
HEADS = 4
DH = 128
HALF_W = HEADS * DH
N_HD = 2 * HEADS
D_FF = 2688
DEPTH = 1
ALPHA = (2 * DEPTH) ** 0.25
EPS = 1e-5
NSLAB = 8
HALO = 16
FFN_BLOCKS = (768, 768, 768, 384)
UP_AHEAD = 2

F32 = jnp.float32
BF16 = jnp.bfloat16
VMEM_LIMIT = 56 * 1024 * 1024


def _dot(a, b):
    return jnp.dot(a, b, preferred_element_type=F32)


def _norm_lanes(z):
    mu = jnp.mean(z, axis=-1, keepdims=True)
    d = z - mu
    var = jnp.mean(d * d, axis=-1, keepdims=True)
    return d * lax.rsqrt(var + EPS)


def _gelu_tanh(x):
    half = 0.5 * x
    return half + half * jnp.tanh(x * (0.7978845608028654 + 0.035677408136300125 * (x * x)))


def _sigmoid(x):
    return 0.5 * jnp.tanh(0.5 * x) + 0.5


def _log_sigmoid(x):
    return jnp.minimum(x, 0.0) - jnp.log(1.0 + jnp.exp(-jnp.abs(x)))


def _const_spec(shape):
    nd = len(shape)
    return pl.BlockSpec(shape, lambda *_: (0,) * nd, pipeline_mode=pl.Buffered(1))


def _mod_kernel(c_ref, w_ref, b_ref, win_ref, wg_ref, o_ref, winb_ref, wgb_ref):
    winb_ref[...] = win_ref[...].T.astype(BF16)
    wg = wg_ref[...]
    wg_rows = jnp.concatenate([wg, jnp.zeros((DH - wg.shape[0], wg.shape[1]), F32)], axis=0)
    wgb_ref[...] = wg_rows.T.astype(BF16)

    cs = c_ref[...]
    a = cs * _sigmoid(cs)
    a_hi = a.astype(BF16)
    a_lo = (a - a_hi.astype(F32)).astype(BF16)
    w = w_ref[...]
    w_hi = w.astype(BF16)
    w_lo = (w - w_hi.astype(F32)).astype(BF16)
    by_hi = _dot(jnp.concatenate([a_hi, a_lo], axis=0), w_hi)
    rows = a.shape[0]
    o_ref[...] = by_hi[:rows] + by_hi[rows:] + _dot(a_hi, w_lo) + b_ref[...]


def _modulation(c_rows, w_ada, b_ada, w_in_t):
    n_out = w_ada.shape[1]
    n_steps = 8
    bn = n_out // n_steps
    n_main = 6 * HALF_W
    n_gate = 2 * N_HD
    rows = n_main // n_steps
    assert w_in_t.shape == (n_main + n_gate, D_MODEL) and n_main % n_gate == 0
    return pl.pallas_call(
        _mod_kernel,
        grid=(n_steps,),
        in_specs=[pl.BlockSpec((8, D_MODEL), lambda j: (0, 0)),
                  pl.BlockSpec((D_MODEL, bn), lambda j: (0, j)),
                  pl.BlockSpec((1, bn), lambda j: (0, j)),
                  pl.BlockSpec((rows, D_MODEL), lambda j: (j, 0)),
                  pl.BlockSpec((n_gate, D_MODEL), lambda j: (n_main // n_gate, 0))],
        out_specs=[pl.BlockSpec((8, bn), lambda j: (0, j)),
                   pl.BlockSpec((D_MODEL, rows), lambda j: (0, j)),
                   pl.BlockSpec((D_MODEL, DH), lambda j: (0, 0))],
        out_shape=[jax.ShapeDtypeStruct((8, n_out), F32),
                   jax.ShapeDtypeStruct((D_MODEL, n_main), BF16),
                   jax.ShapeDtypeStruct((D_MODEL, DH), BF16)],
        compiler_params=pltpu.CompilerParams(vmem_limit_bytes=VMEM_LIMIT),
        name="mod",
    )(c_rows, w_ada, b_ada, w_in_t, w_in_t)


def _inproj_kernel(x_ref, xp_ref, xn_ref, mod_ref, win_ref, wg_ref,
                   lng_ref, ws_ref, bsb_ref, cw_ref, cb_ref, gbi_ref, gbf_ref, ng_ref,
                   *rest, tm, convert, mixer):
    if convert:
        wup_ref, wdown_ref, wout_ref = rest[:3]
        wupb_ref, wdownb_ref, woutb_ref = rest[-3:]
        rest = rest[3:-3]
        start = 0
        for blk in FFN_BLOCKS:
            wupb_ref[:, 2 * start:2 * start + blk] = wup_ref[:, start:start + blk].astype(BF16)
            wupb_ref[:, 2 * start + blk:2 * (start + blk)] = wup_ref[:, D_FF + start:D_FF + start + blk].astype(BF16)
            start += blk
        wdownb_ref[...] = wdown_ref[...].astype(BF16)
        woutb_ref[...] = wout_ref[...].astype(BF16)
    yan_ref, q_ref, kt_ref, vm_ref, o_ref, gb_ref, gcm_ref, gct_ref, gwk_ref = rest
    i = pl.program_id(1)
    n_tiles = pl.num_programs(1)
    shift = mod_ref[0:1, :]
    scale = mod_ref[1:2, :]

    def modulate(xv):
        return (xv * scale + shift).astype(BF16)

    hx = modulate(x_ref[...])
    h_prev = jnp.where(i > 0, modulate(xp_ref[...]), jnp.zeros((), BF16))
    h_next = jnp.where(i < n_tiles - 1, modulate(xn_ref[...]), jnp.zeros((), BF16))
    gp = _dot(hx, wg_ref[...])
    uv = _dot(hx, win_ref[:, 0:2 * HALF_W]) if mixer else None
    ext = _dot(jnp.concatenate([h_prev, hx, h_next], axis=0), win_ref[:, 2 * HALF_W:4 * HALF_W])
    vo = _dot(hx, win_ref[:, 4 * HALF_W:(6 if mixer else 5) * HALF_W])

    if mixer:
        u = _gelu_tanh(uv[:, :HALF_W])
        v = _gelu_tanh(uv[:, HALF_W:])
    else:
        yan_ref[...] = jnp.zeros(yan_ref.shape, yan_ref.dtype)
        o_ref[...] = jnp.zeros(o_ref.shape, o_ref.dtype)
    for g in range(HEADS if mixer else 0):
        lanes = slice(g * DH, (g + 1) * DH)
        vh = (_norm_lanes(v[:, lanes]) * lng_ref[:, lanes]).astype(BF16)
        for c in range(tm // CHUNK):
            rows = slice(c * CHUNK, (c + 1) * CHUNK)
            mixed = _dot(ws_ref[g], vh[rows]) + bsb_ref[:, lanes]
            ya = u[rows, lanes] * mixed
            yan_ref[rows, lanes] = (_norm_lanes(ya) * ng_ref[:, lanes]).astype(BF16)

    n_ext = tm + 2 * HALO
    pre = ext[HALO:HALO + tm]
    down = pltpu.roll(ext, 1, 0)[HALO:HALO + tm]
    up = pltpu.roll(ext, n_ext - 1, 0)[HALO:HALO + tm]
    conv = down * cw_ref[0:1, :] + pre * cw_ref[1:2, :] + up * cw_ref[2:3, :] + cb_ref[...]
    qk = conv * _sigmoid(conv)
    q_ref[...] = qk[:, :HALF_W].astype(BF16)
    kt_ref[...] = (qk[:, HALF_W:] * (DH ** -0.5)).T.astype(BF16)

    vm_ref[...] = vo[:, :HALF_W].astype(BF16)
    if mixer:
        o_ref[...] = vo[:, HALF_W:].astype(BF16)

    pos = lax.broadcasted_iota(jnp.int32, (N_HD, CHUNK), 1)
    is_fwd = lax.broadcasted_iota(jnp.int32, (N_HD, CHUNK), 0) < HEADS
    pad = jnp.zeros((DH - N_HD, CHUNK), F32)

    def scan_lanes(z, op, fill, reverse):
        step = 1
        while step < CHUNK:
            if reverse:
                moved = jnp.where(pos < CHUNK - step, pltpu.roll(z, CHUNK - step, 1), fill)
            else:
                moved = jnp.where(pos >= step, pltpu.roll(z, step, 1), fill)
            z = op(z, moved)
            step *= 2
        return z

    for c in range(tm // CHUNK):
        rows = slice(c * CHUNK, (c + 1) * CHUNK)
        gt = gp[rows].T
        log_i = gt[0:N_HD] + gbi_ref[...]
        lf = _log_sigmoid(gt[N_HD:2 * N_HD] + gbf_ref[...])
        csum = scan_lanes(lf, jnp.add, 0.0, False)
        suffix = csum[:, CHUNK - 1:CHUNK] - csum + lf
        b = jnp.where(is_fwd, csum, suffix)
        cdiff = b - log_i
        cmin = jnp.where(is_fwd, scan_lanes(cdiff, jnp.minimum, jnp.inf, False),
                         scan_lanes(cdiff, jnp.minimum, jnp.inf, True))
        cmin_end = jnp.where(is_fwd[:, 0:1], cmin[:, CHUNK - 1:CHUNK], cmin[:, 0:1])
        gct_ref[:, rows] = cdiff
        gwk_ref[:, rows] = jnp.exp(cmin_end - cdiff)
        gb_ref[rows, :] = jnp.concatenate([b, pad], axis=0).T
        gcm_ref[rows, :] = jnp.concatenate([cmin, pad], axis=0).T


def _inproj(xs, mod6, wts, *, tm, ffn_f32=(), mixer=True):
    B, S, _ = xs.shape
    nt = S // tm
    hb = tm // HALO
    n_halo_blocks = S // HALO
    kern = functools.partial(_inproj_kernel, tm=tm, convert=bool(ffn_f32), mixer=mixer)
    tok = lambda w: pl.BlockSpec((None, tm, w), lambda b, i: (b, i, 0))
    in_specs = [
        tok(D_MODEL),
        pl.BlockSpec((None, HALO, D_MODEL), lambda b, i: (b, jnp.maximum(i * hb - 1, 0), 0)),
        pl.BlockSpec((None, HALO, D_MODEL), lambda b, i: (b, jnp.minimum((i + 1) * hb, n_halo_blocks - 1), 0)),
        pl.BlockSpec((None, 6, D_MODEL), lambda b, i: (b, 0, 0)),
    ] + [_const_spec(w.shape) for w in wts]
    out_shape = [
        jax.ShapeDtypeStruct((B, S, HALF_W), BF16),
        jax.ShapeDtypeStruct((B, S, HALF_W), BF16),
        jax.ShapeDtypeStruct((B, HALF_W, S), BF16),
        jax.ShapeDtypeStruct((B, S, HALF_W), BF16),
        jax.ShapeDtypeStruct((B, S, HALF_W), BF16),
        jax.ShapeDtypeStruct((B, S, DH), F32),
        jax.ShapeDtypeStruct((B, S, DH), F32),
        jax.ShapeDtypeStruct((B, N_HD, S), F32),
        jax.ShapeDtypeStruct((B, N_HD, S), F32),
    ]
    out_specs = [
        tok(HALF_W), tok(HALF_W),
        pl.BlockSpec((None, HALF_W, tm), lambda b, i: (b, 0, i)),
        tok(HALF_W), tok(HALF_W), tok(DH), tok(DH),
        pl.BlockSpec((None, N_HD, tm), lambda b, i: (b, 0, i)),
        pl.BlockSpec((None, N_HD, tm), lambda b, i: (b, 0, i)),
    ]
    if ffn_f32:
        for w, n_blocks in zip(ffn_f32, (B * nt, 8, B * nt)):
            rows = w.shape[0] // n_blocks
            assert rows * n_blocks == w.shape[0] and rows % 16 == 0 and n_blocks <= B * nt
            spec = pl.BlockSpec((rows, w.shape[1]), lambda b, i, n=n_blocks: (jnp.minimum(b * nt + i, n - 1), 0))
            in_specs.append(spec)
            out_specs.append(spec)
            out_shape.append(jax.ShapeDtypeStruct(w.shape, BF16))
    return pl.pallas_call(
        kern, grid=(B, nt), in_specs=in_specs, out_specs=out_specs, out_shape=out_shape,
        compiler_params=pltpu.CompilerParams(
            dimension_semantics=("arbitrary", "arbitrary"), vmem_limit_bytes=VMEM_LIMIT),
        name="inproj",
    )(xs, xs, xs, mod6, *wts, *ffn_f32)


def _scan_kernel(qf_ref, ktf_ref, vf_ref, gbf_ref, gcmf_ref, gctf_ref, gwkf_ref,
                 qb_ref, ktb_ref, vb_ref, gbb_ref, gcmb_ref, gctb_ref, gwkb_ref,
                 *rest, zero_init, cps):
    if zero_init:
        hf_ref, hb_ref, cst_ref, mst_ref = rest
    else:
        c0_ref, m0_ref, hf_ref, hb_ref, cst_ref, mst_ref = rest

    @pl.when(pl.program_id(1) == 0)
    def _():
        if zero_init:
            cst_ref[...] = jnp.zeros(cst_ref.shape, F32)
            mst_ref[...] = jnp.zeros(mst_ref.shape, F32)
        else:
            cst_ref[...] = c0_ref[...]
            mst_ref[...] = m0_ref[...]

    t_idx = lax.broadcasted_iota(jnp.int32, (CHUNK, CHUNK), 0)
    s_idx = lax.broadcasted_iota(jnp.int32, (CHUNK, CHUNK), 1)
    is_fwd = lax.broadcasted_iota(jnp.int32, (CHUNK, DH), 1) < HEADS
    ones = jnp.ones((CHUNK, DH), BF16)
    f_rows = [slice(k * CHUNK, (k + 1) * CHUNK) for k in range(cps)]
    b_rows = f_rows[::-1]

    terms = []
    m_row = mst_ref[0:1, :]
    for fr, br in zip(f_rows, b_rows):
        b = jnp.where(is_fwd, gbf_ref[fr, :], gbb_ref[br, :])
        cmin = jnp.where(is_fwd, gcmf_ref[fr, :], gcmb_ref[br, :])
        total = jnp.where(is_fwd[0:1], gbf_ref[fr.stop - 1:fr.stop, :], gbb_ref[br.start:br.start + 1, :])
        cmin_end = jnp.where(is_fwd[0:1], gcmf_ref[fr.stop - 1:fr.stop, :], gcmb_ref[br.start:br.start + 1, :])
        rmax = b - cmin
        inter = b + m_row
        m_t = jnp.maximum(inter, rmax)
        gmax = total - cmin_end
        m_new = jnp.maximum(total + m_row, gmax)
        terms.append(dict(lead=cmin + (rmax - m_t), w_state=jnp.exp(inter - m_t),
                          floor=jnp.exp(-m_t), w_old=jnp.exp(total + m_row - m_new), w_new=jnp.exp(gmax - m_new)))
        m_row = m_new
    mst_ref[...] = jnp.broadcast_to(m_row, (N_HD, DH))

    pairs = []
    for j in range(N_HD):
        fwd = j < HEADS
        lanes = slice((j % HEADS) * DH, (j % HEADS + 1) * DH)
        refs = ((qf_ref, ktf_ref, vf_ref, gctf_ref, gwkf_ref, hf_ref) if fwd else
                (qb_ref, ktb_ref, vb_ref, gctb_ref, gwkb_ref, hb_ref))
        pairs.append((fwd, lanes, f_rows if fwd else b_rows) + refs)

    scores, updates, v_augs = [], [], []
    for k in range(cps):
        scores.append([_dot(q_ref[rows[k], lanes], kt_ref[lanes, rows[k]])
                       for (fwd, lanes, rows, q_ref, kt_ref, v_ref, gct_ref, gwk_ref, out_ref) in pairs])
    for k in range(cps):
        v_augs.append([jnp.concatenate([v_ref[rows[k], lanes], ones], axis=1)
                       for (fwd, lanes, rows, q_ref, kt_ref, v_ref, gct_ref, gwk_ref, out_ref) in pairs])
        updates.append([_dot((kt_ref[lanes, rows[k]].astype(F32) * gwk_ref[j:j + 1, rows[k]]).astype(BF16), v_augs[k][j])
                        for j, (fwd, lanes, rows, q_ref, kt_ref, v_ref, gct_ref, gwk_ref, out_ref) in enumerate(pairs)])
    states = [cst_ref[j] for j in range(N_HD)]
    for k in range(cps):
        tk = terms[k]
        from_state = [_dot(q_ref[rows[k], lanes], states[j].astype(BF16))
                      for j, (fwd, lanes, rows, q_ref, kt_ref, v_ref, gct_ref, gwk_ref, out_ref) in enumerate(pairs)]
        states = [tk["w_old"][:, j:j + 1] * states[j] + tk["w_new"][:, j:j + 1] * updates[k][j] for j in range(N_HD)]
        for j, (fwd, lanes, rows, q_ref, kt_ref, v_ref, gct_ref, gwk_ref, out_ref) in enumerate(pairs):
            keep = (s_idx <= t_idx) if fwd else (s_idx >= t_idx)
            decay = jnp.exp(jnp.where(keep, tk["lead"][:, j:j + 1] - gct_ref[j:j + 1, rows[k]], -jnp.inf))
            sv = _dot((scores[k][j] * decay).astype(BF16), v_augs[k][j])
            ws = jnp.broadcast_to(tk["w_state"][:, j:j + 1], (CHUNK, DH))
            num = sv[:, :DH] + ws * from_state[j][:, :DH]
            den = sv[:, DH:] + ws * from_state[j][:, DH:]
            out_ref[rows[k], lanes] = (num / jnp.maximum(jnp.abs(den), tk["floor"][:, j:j + 1])).astype(BF16)
    for j in range(N_HD):
        cst_ref[j] = states[j]


def _scan(q, kt, v, gb, gcm, gct, gwk, init=None):
    B, S, _ = q.shape
    cps = min(8, S // CHUNK)
    rows = cps * CHUNK
    nb = S // rows
    fwd_tok = lambda w: pl.BlockSpec((None, rows, w), lambda b, i: (b, i, 0))
    bwd_tok = lambda w: pl.BlockSpec((None, rows, w), lambda b, i: (b, nb - 1 - i, 0))
    fwd_t = lambda r: pl.BlockSpec((None, r, rows), lambda b, i: (b, 0, i))
    bwd_t = lambda r: pl.BlockSpec((None, r, rows), lambda b, i: (b, 0, nb - 1 - i))
    c_spec = pl.BlockSpec((None, N_HD, DH, 2 * DH), lambda b, i: (b, 0, 0, 0))
    m_spec = pl.BlockSpec((None, N_HD, DH), lambda b, i: (b, 0, 0))
    init = () if init is None else tuple(init)
    return pl.pallas_call(
        functools.partial(_scan_kernel, zero_init=not init, cps=cps), grid=(B, nb),
        in_specs=[fwd_tok(HALF_W), fwd_t(HALF_W), fwd_tok(HALF_W), fwd_tok(DH), fwd_tok(DH), fwd_t(N_HD), fwd_t(N_HD),
                  bwd_tok(HALF_W), bwd_t(HALF_W), bwd_tok(HALF_W), bwd_tok(DH), bwd_tok(DH), bwd_t(N_HD), bwd_t(N_HD),
                  ] + [c_spec, m_spec][:len(init)],
        out_specs=[fwd_tok(HALF_W), bwd_tok(HALF_W), c_spec, m_spec],
        out_shape=[jax.ShapeDtypeStruct((B, S, HALF_W), BF16),
                   jax.ShapeDtypeStruct((B, S, HALF_W), BF16),
                   jax.ShapeDtypeStruct((B, N_HD, DH, 2 * DH), F32),
                   jax.ShapeDtypeStruct((B, N_HD, DH), F32)],
        compiler_params=pltpu.CompilerParams(
            dimension_semantics=("arbitrary", "arbitrary"), vmem_limit_bytes=VMEM_LIMIT),
        name="scan",
    )(q, kt, v, gb, gcm, gct, gwk, q, kt, v, gb, gcm, gct, gwk, *init)


def _layer_norm(z, g_ref, b_ref):
    return _norm_lanes(z) * g_ref[...] + b_ref[...]


def _ffn_kernel(x_ref, yan_ref, hf_ref, hb_ref, o_ref, mod_ref, ng_ref, wout_ref,
                ln1g_ref, ln1b_ref, wup_ref, cw_ref, cb_ref, wdown_ref, ln2g_ref, ln2b_ref,
                out_ref, perm_ref, *, tm, n_sub):
    ts = tm // n_sub
    ns = ts // NSLAB
    n_lt = D_MODEL // DH
    starts = [sum(FFN_BLOCKS[:k]) for k in range(len(FFN_BLOCKS))]
    subs = [slice(t * ts, (t + 1) * ts) for t in range(n_sub)]

    def mixer_in(rows):
        h = hf_ref[rows, :].astype(F32) + hb_ref[rows, :].astype(F32)
        gated = []
        for g in range(HEADS):
            lanes = slice(g * DH, (g + 1) * DH)
            gated.append((_norm_lanes(h[:, lanes]) * ng_ref[:, lanes] * _sigmoid(o_ref[rows, lanes].astype(F32))).astype(BF16))
        return jnp.concatenate([yan_ref[rows, :]] + gated, axis=1)

    def permuted_x1(t, mix):
        x1n = _layer_norm(ALPHA * x_ref[subs[t], :] + mod_ref[2:3, :] * mix, ln1g_ref, ln1b_ref)
        for c in range(n_lt):
            perm_ref[t * n_lt + c] = x1n[:, c * DH:(c + 1) * DH]
        return jnp.concatenate(
            [jnp.concatenate([perm_ref[t * n_lt + c, pl.ds(j, ns, stride=NSLAB), :] for c in range(n_lt)], axis=1)
             for j in range(NSLAB)], axis=0)

    def up(h2, k):
        return _dot(h2, wup_ref[:, 2 * starts[k]:2 * (starts[k] + FFN_BLOCKS[k])])

    def hidden(a, k):
        blk = FFN_BLOCKS[k]
        cols = slice(2 * starts[k], 2 * (starts[k] + blk))
        q8 = lax.broadcasted_iota(jnp.int32, (ns, 2 * blk), 0) % NSLAB
        before_first = jnp.where(q8 == 0, 0.0, pltpu.roll(a[ts - ns:], 1, 0))
        after_last = jnp.where(q8 == NSLAB - 1, 0.0, pltpu.roll(a[:ns], ns - 1, 0))
        left = jnp.concatenate([before_first, a[:ts - ns]], axis=0)
        right = jnp.concatenate([a[ns:], after_last], axis=0)
        a = left * cw_ref[0:1, cols] + a * cw_ref[1:2, cols] + right * cw_ref[2:3, cols] + cb_ref[:, cols]
        prod = a[:, blk:] * a[:, :blk]
        return (prod + prod * jnp.tanh(a[:, blk:])).astype(BF16)

    def finish(t, x1, acc):
        out = _layer_norm(ALPHA * x1 + mod_ref[5:6, :] * acc, ln2g_ref, ln2b_ref)
        for c in range(n_lt):
            for j in range(NSLAB):
                perm_ref[t * n_lt + c, pl.ds(j, ns, stride=NSLAB), :] = out[j * ns:(j + 1) * ns, c * DH:(c + 1) * DH]
        for c in range(n_lt):
            out_ref[subs[t], c * DH:(c + 1) * DH] = perm_ref[t * n_lt + c]

    mixes = [_dot(mixer_in(rows), wout_ref[...]) for rows in subs]
    x1s = [permuted_x1(t, mixes[t]) for t in range(n_sub)]
    h2s = [(x1 * mod_ref[4:5, :] + mod_ref[3:4, :]).astype(BF16) for x1 in x1s]

    steps = [(t, k) for t in range(n_sub) for k in range(len(FFN_BLOCKS))]
    accs = [None] * n_sub
    queue = [up(h2s[t], k) for (t, k) in steps[:UP_AHEAD]]
    for idx, (t, k) in enumerate(steps):
        if idx + UP_AHEAD < len(steps):
            queue.append(up(h2s[steps[idx + UP_AHEAD][0]], steps[idx + UP_AHEAD][1]))
        part = _dot(hidden(queue.pop(0), k), wdown_ref[starts[k]:starts[k] + FFN_BLOCKS[k], :])
        accs[t] = part if accs[t] is None else accs[t] + part
        if k == len(FFN_BLOCKS) - 1:
            finish(t, x1s[t], accs[t])


def _ffn(x, yan, hf, hb, o_pre, mod6, wts, *, tm, n_sub):
    B, S, _ = x.shape
    kern = functools.partial(_ffn_kernel, tm=tm, n_sub=n_sub)
    tok = lambda w: pl.BlockSpec((None, tm, w), lambda b, i: (b, i, 0))
    in_specs = [tok(D_MODEL), tok(HALF_W), tok(HALF_W), tok(HALF_W), tok(HALF_W),
                pl.BlockSpec((None, 6, D_MODEL), lambda b, i: (b, 0, 0))] + [_const_spec(w.shape) for w in wts]
    return pl.pallas_call(
        kern, grid=(B, S // tm), in_specs=in_specs, out_specs=tok(D_MODEL),
        out_shape=jax.ShapeDtypeStruct((B, S, D_MODEL), F32),
        scratch_shapes=[pltpu.VMEM((n_sub * (D_MODEL // DH), tm // n_sub, DH), F32)],
        compiler_params=pltpu.CompilerParams(
            dimension_semantics=("arbitrary", "arbitrary"), vmem_limit_bytes=VMEM_LIMIT),
        name="ffn",
    )(x, yan, hf, hb, o_pre, mod6, *wts)


def _interleave_blocks(val, gate, axis):
    parts, start = [], 0
    for blk in FFN_BLOCKS:
        parts.append(lax.slice_in_dim(val, start, start + blk, axis=axis))
        parts.append(lax.slice_in_dim(gate, start, start + blk, axis=axis))
        start += blk
    return jnp.concatenate(parts, axis=axis)


def kernel(x, c, ctx, c_ctx, w_ada, b_ada, w_in, gmlp_ln_g, gmlp_ws, gmlp_bs, qk_conv_w, qk_conv_b,
           b_igate, b_fgate, mix_norm_g, w_out, ln1_g, ln1_b, w_up, ffn_conv_w, ffn_conv_b, w_down,
           ln2_g, ln2_b):
    B, S, _ = x.shape
    assert DEPTH == 1 and w_in.shape[0] == 1
    l = 0

    c_rows = jnp.concatenate([c, c_ctx[None, :], jnp.zeros((8 - B - 1, D_MODEL), F32)], axis=0)
    chunk = jnp.arange(6 * D_MODEL) // D_MODEL
    bias = b_ada[l] + jnp.where((chunk == 1) | (chunk == 4), 1.0, 0.0)
    mod, w_in_b, w_g = _modulation(c_rows, w_ada[l], bias[None, :], jnp.swapaxes(w_in[l], 0, 1))
    mod_x = mod[:B].reshape(B, 6, D_MODEL)
    mod_c = jnp.broadcast_to(mod[B].reshape(1, 6, D_MODEL), (B, 6, D_MODEL))

    gbi = jnp.broadcast_to(b_igate[l].reshape(N_HD, 1), (N_HD, CHUNK))
    gbf = jnp.broadcast_to(b_fgate[l].reshape(N_HD, 1), (N_HD, CHUNK))
    ln_g = gmlp_ln_g[l].reshape(1, HALF_W)
    ws = gmlp_ws[l].astype(BF16)
    bsb = jnp.repeat(gmlp_bs[l].T, DH, axis=1)
    norm_g = mix_norm_g[l][None, :]
    in_wts = (w_in_b, w_g, ln_g, ws, bsb, qk_conv_w[l], qk_conv_b[l][None, :], gbi, gbf,
              norm_g[:, :HALF_W])

    _, q_c, kt_c, v_c, _, *gates_c = _inproj(ctx, mod_c, in_wts, tm=ctx.shape[1], mixer=False)
    _, _, c_state, m_state = _scan(q_c, kt_c, v_c, *gates_c)

    yan, q, kt, v, o_pre, *rest = _inproj(x, mod_x, in_wts, tm=512, ffn_f32=(w_up[l], w_down[l], w_out[l]))
    gates, (w_up_b, w_down_b, w_out_b) = rest[:4], rest[4:]
    hf, hb, _, _ = _scan(q, kt, v, *gates, init=(c_state, m_state))

    cw = _interleave_blocks(ffn_conv_w[l][:, :D_FF], 0.5 * ffn_conv_w[l][:, D_FF:], 1)
    cb = _interleave_blocks(ffn_conv_b[l][None, :D_FF], 0.5 * ffn_conv_b[l][None, D_FF:], 1)
    ffn_wts = (norm_g[:, HALF_W:], w_out_b, ln1_g[l][None, :], ln1_b[l][None, :],
               w_up_b, cw, cb, w_down_b, ln2_g[l][None, :], ln2_b[l][None, :])
    return _ffn(x, yan, hf, hb, o_pre, mod_x, ffn_wts, tm=512, n_sub=2)
```

```python
import functools

import jax
import jax.numpy as jnp
from jax import lax
from jax.experimental import pallas as pl
from jax.experimental.pallas import tpu as pltpu

D_MODEL = 1024
GRID_W = 64
CHUNK = 128
HEADS = 4
DH = 128
HALF_W = HEADS * DH
N_HD = 2 * HEADS
D_FF = 2688
DEPTH = 1
ALPHA = (2 * DEPTH) ** 0.25
EPS = 1e-5
NSLAB = 8
HALO = 16
FFN_BLOCKS = (768, 768, 768, 384)
UP_AHEAD = 2

F32 = jnp.float32
BF16 = jnp.bfloat16
VMEM_LIMIT = 56 * 1024 * 1024


def _dot(a, b):
    return jnp.dot(a, b, preferred_element_type=F32)


def _norm_lanes(z):
    mu = jnp.mean(z, axis=-1, keepdims=True)
    d = z - mu
    var = jnp.mean(d * d, axis=-1, keepdims=True)
    return d * lax.rsqrt(var + EPS)


def _gelu_tanh(x):
    half = 0.5 * x
    return half + half * jnp.tanh(x * (0.7978845608028654 + 0.035677408136300125 * (x * x)))


def _sigmoid(x):
    return 0.5 * jnp.tanh(0.5 * x) + 0.5


def _log_sigmoid(x):
    return jnp.minimum(x, 0.0) - jnp.log(1.0 + jnp.exp(-jnp.abs(x)))


def _const_spec(shape):
    nd = len(shape)
    return pl.BlockSpec(shape, lambda *_: (0,) * nd, pipeline_mode=pl.Buffered(1))


def _mod_kernel(c_ref, w_ref, b_ref, win_ref, wg_ref, o_ref, winb_ref, wgb_ref):
    winb_ref[...] = win_ref[...].T.astype(BF16)
    wg = wg_ref[...]
    wg_rows = jnp.concatenate([wg, jnp.zeros((DH - wg.shape[0], wg.shape[1]), F32)], axis=0)
    wgb_ref[...] = wg_rows.T.astype(BF16)

    cs = c_ref[...]
    a = cs * _sigmoid(cs)
    a_hi = a.astype(BF16)
    a_lo = (a - a_hi.astype(F32)).astype(BF16)
    w = w_ref[...]
    w_hi = w.astype(BF16)
    w_lo = (w - w_hi.astype(F32)).astype(BF16)
    by_hi = _dot(jnp.concatenate([a_hi, a_lo], axis=0), w_hi)
    rows = a.shape[0]
    o_ref[...] = by_hi[:rows] + by_hi[rows:] + _dot(a_hi, w_lo) + b_ref[...]


def _modulation(c_rows, w_ada, b_ada, w_in_t):
    n_out = w_ada.shape[1]
    n_steps = 8
    bn = n_out // n_steps
    n_main = 6 * HALF_W
    n_gate = 2 * N_HD
    rows = n_main // n_steps
    assert w_in_t.shape == (n_main + n_gate, D_MODEL) and n_main % n_gate == 0
    return pl.pallas_call(
        _mod_kernel,
        grid=(n_steps,),
        in_specs=[pl.BlockSpec((8, D_MODEL), lambda j: (0, 0)),
                  pl.BlockSpec((D_MODEL, bn), lambda j: (0, j)),
                  pl.BlockSpec((1, bn), lambda j: (0, j)),
                  pl.BlockSpec((rows, D_MODEL), lambda j: (j, 0)),
                  pl.BlockSpec((n_gate, D_MODEL), lambda j: (n_main // n_gate, 0))],
        out_specs=[pl.BlockSpec((8, bn), lambda j: (0, j)),
                   pl.BlockSpec((D_MODEL, rows), lambda j: (0, j)),
                   pl.BlockSpec((D_MODEL, DH), lambda j: (0, 0))],
        out_shape=[jax.ShapeDtypeStruct((8, n_out), F32),
                   jax.ShapeDtypeStruct((D_MODEL, n_main), BF16),
                   jax.ShapeDtypeStruct((D_MODEL, DH), BF16)],
        compiler_params=pltpu.CompilerParams(vmem_limit_bytes=VMEM_LIMIT),
        name="mod",
    )(c_rows, w_ada, b_ada, w_in_t, w_in_t)


def _inproj_kernel(x_ref, xp_ref, xn_ref, mod_ref, win_ref, wg_ref,
                   lng_ref, ws_ref, bsb_ref, cw_ref, cb_ref, gbi_ref, gbf_ref, ng_ref,
                   *rest, tm, convert, mixer):
    if convert:
        wup_ref, wdown_ref, wout_ref = rest[:3]
        wupb_ref, wdownb_ref, woutb_ref = rest[-3:]
        rest = rest[3:-3]
        start = 0
        for blk in FFN_BLOCKS:
            wupb_ref[:, 2 * start:2 * start + blk] = wup_ref[:, start:start + blk].astype(BF16)
            wupb_ref[:, 2 * start + blk:2 * (start + blk)] = wup_ref[:, D_FF + start:D_FF + start + blk].astype(BF16)
            start += blk
        wdownb_ref[...] = wdown_ref[...].astype(BF16)
        woutb_ref[...] = wout_ref[...].astype(BF16)
    yan_ref, q_ref, kt_ref, vm_ref, o_ref, gb_ref, gcm_ref, gct_ref, gwk_ref = rest
    i = pl.program_id(1)
    n_tiles = pl.num_programs(1)
    shift = mod_ref[0:1, :]
    scale = mod_ref[1:2, :]

    def modulate(xv):
        return (xv * scale + shift).astype(BF16)

    hx = modulate(x_ref[...])
    h_prev = jnp.where(i > 0, modulate(xp_ref[...]), jnp.zeros((), BF16))
    h_next = jnp.where(i < n_tiles - 1, modulate(xn_ref[...]), jnp.zeros((), BF16))
    gp = _dot(hx, wg_ref[...])
    uv = _dot(hx, win_ref[:, 0:2 * HALF_W]) if mixer else None
    ext = _dot(jnp.concatenate([h_prev, hx, h_next], axis=0), win_ref[:, 2 * HALF_W:4 * HALF_W])
    vo = _dot(hx, win_ref[:, 4 * HALF_W:(6 if mixer else 5) * HALF_W])

    if mixer:
        u = _gelu_tanh(uv[:, :HALF_W])
        v = _gelu_tanh(uv[:, HALF_W:])
    else:
        yan_ref[...] = jnp.zeros(yan_ref.shape, yan_ref.dtype)
        o_ref[...] = jnp.zeros(o_ref.shape, o_ref.dtype)
    for g in range(HEADS if mixer else 0):
        lanes = slice(g * DH, (g + 1) * DH)
        vh = (_norm_lanes(v[:, lanes]) * lng_ref[:, lanes]).astype(BF16)
        for c in range(tm // CHUNK):
            rows = slice(c * CHUNK, (c + 1) * CHUNK)
            mixed = _dot(ws_ref[g], vh[rows]) + bsb_ref[:, lanes]
            ya = u[rows, lanes] * mixed
            yan_ref[rows, lanes] = (_norm_lanes(ya) * ng_ref[:, lanes]).astype(BF16)

    n_ext = tm + 2 * HALO
    pre = ext[HALO:HALO + tm]
    down = pltpu.roll(ext, 1, 0)[HALO:HALO + tm]
    up = pltpu.roll(ext, n_ext - 1, 0)[HALO:HALO + tm]
    conv = down * cw_ref[0:1, :] + pre * cw_ref[1:2, :] + up * cw_ref[2:3, :] + cb_ref[...]
    qk = conv * _sigmoid(conv)
    q_ref[...] = qk[:, :HALF_W].astype(BF16)
    kt_ref[...] = (qk[:, HALF_W:] * (DH ** -0.5)).T.astype(BF16)

    vm_ref[...] = vo[:, :HALF_W].astype(BF16)
    if mixer:
        o_ref[...] = vo[:, HALF_W:]

    pos = lax.broadcasted_iota(jnp.int32, (N_HD, CHUNK), 1)
    is_fwd = lax.broadcasted_iota(jnp.int32, (N_HD, CHUNK), 0) < HEADS
    pad = jnp.zeros((DH - N_HD, CHUNK), F32)

    def scan_lanes(z, op, fill, reverse):
        step = 1
        while step < CHUNK:
            if reverse:
                moved = jnp.where(pos < CHUNK - step, pltpu.roll(z, CHUNK - step, 1), fill)
            else:
                moved = jnp.where(pos >= step, pltpu.roll(z, step, 1), fill)
            z = op(z, moved)
            step *= 2
        return z

    for c in range(tm // CHUNK):
        rows = slice(c * CHUNK, (c + 1) * CHUNK)
        gt = gp[rows].T
        log_i = gt[0:N_HD] + gbi_ref[...]
        lf = _log_sigmoid(gt[N_HD:2 * N_HD] + gbf_ref[...])
        csum = scan_lanes(lf, jnp.add, 0.0, False)
        suffix = csum[:, CHUNK - 1:CHUNK] - csum + lf
        b = jnp.where(is_fwd, csum, suffix)
        cdiff = b - log_i
        cmin = jnp.where(is_fwd, scan_lanes(cdiff, jnp.minimum, jnp.inf, False),
                         scan_lanes(cdiff, jnp.minimum, jnp.inf, True))
        cmin_end = jnp.where(is_fwd[:, 0:1], cmin[:, CHUNK - 1:CHUNK], cmin[:, 0:1])
        gct_ref[:, rows] = cdiff
        gwk_ref[:, rows] = jnp.exp(cmin_end - cdiff)
        gb_ref[rows, :] = jnp.concatenate([b, pad], axis=0).T
        gcm_ref[rows, :] = jnp.concatenate([cmin, pad], axis=0).T


def _inproj(xs, mod6, wts, *, tm, ffn_f32=(), mixer=True):
    B, S, _ = xs.shape
    nt = S // tm
    hb = tm // HALO
    n_halo_blocks = S // HALO
    kern = functools.partial(_inproj_kernel, tm=tm, convert=bool(ffn_f32), mixer=mixer)
    tok = lambda w: pl.BlockSpec((None, tm, w), lambda b, i: (b, i, 0))
    in_specs = [
        tok(D_MODEL),
        pl.BlockSpec((None, HALO, D_MODEL), lambda b, i: (b, jnp.maximum(i * hb - 1, 0), 0)),
        pl.BlockSpec((None, HALO, D_MODEL), lambda b, i: (b, jnp.minimum((i + 1) * hb, n_halo_blocks - 1), 0)),
        pl.BlockSpec((None, 6, D_MODEL), lambda b, i: (b, 0, 0)),
    ] + [_const_spec(w.shape) for w in wts]
    out_shape = [
        jax.ShapeDtypeStruct((B, S, HALF_W), BF16),
        jax.ShapeDtypeStruct((B, S, HALF_W), BF16),
        jax.ShapeDtypeStruct((B, HALF_W, S), BF16),
        jax.ShapeDtypeStruct((B, S, HALF_W), BF16),
        jax.ShapeDtypeStruct((B, S, HALF_W), F32),
        jax.ShapeDtypeStruct((B, S, DH), F32),
        jax.ShapeDtypeStruct((B, S, DH), F32),
        jax.ShapeDtypeStruct((B, N_HD, S), F32),
        jax.ShapeDtypeStruct((B, N_HD, S), F32),
    ]
    out_specs = [
        tok(HALF_W), tok(HALF_W),
        pl.BlockSpec((None, HALF_W, tm), lambda b, i: (b, 0, i)),
        tok(HALF_W), tok(HALF_W), tok(DH), tok(DH),
        pl.BlockSpec((None, N_HD, tm), lambda b, i: (b, 0, i)),
        pl.BlockSpec((None, N_HD, tm), lambda b, i: (b, 0, i)),
    ]
    if ffn_f32:
        for w, n_blocks in zip(ffn_f32, (B * nt, 8, B * nt)):
            rows = w.shape[0] // n_blocks
            assert rows * n_blocks == w.shape[0] and rows % 16 == 0 and n_blocks <= B * nt
            spec = pl.BlockSpec((rows, w.shape[1]), lambda b, i, n=n_blocks: (jnp.minimum(b * nt + i, n - 1), 0))
            in_specs.append(spec)
            out_specs.append(spec)
            out_shape.append(jax.ShapeDtypeStruct(w.shape, BF16))
    return pl.pallas_call(
        kern, grid=(B, nt), in_specs=in_specs, out_specs=out_specs, out_shape=out_shape,
        compiler_params=pltpu.CompilerParams(
            dimension_semantics=("arbitrary", "arbitrary"), vmem_limit_bytes=VMEM_LIMIT),
        name="inproj",
    )(xs, xs, xs, mod6, *wts, *ffn_f32)


def _scan_kernel(qf_ref, ktf_ref, vf_ref, gbf_ref, gcmf_ref, gctf_ref, gwkf_ref,
                 qb_ref, ktb_ref, vb_ref, gbb_ref, gcmb_ref, gctb_ref, gwkb_ref,
                 *rest, zero_init, cps):
    if zero_init:
        hf_ref, hb_ref, cst_ref, mst_ref = rest
    else:
        c0_ref, m0_ref, hf_ref, hb_ref, cst_ref, mst_ref = rest

    @pl.when(pl.program_id(1) == 0)
    def _():
        if zero_init:
            cst_ref[...] = jnp.zeros(cst_ref.shape, F32)
            mst_ref[...] = jnp.zeros(mst_ref.shape, F32)
        else:
            cst_ref[...] = c0_ref[...]
            mst_ref[...] = m0_ref[...]

    t_idx = lax.broadcasted_iota(jnp.int32, (CHUNK, CHUNK), 0)
    s_idx = lax.broadcasted_iota(jnp.int32, (CHUNK, CHUNK), 1)
    is_fwd = lax.broadcasted_iota(jnp.int32, (CHUNK, DH), 1) < HEADS
    ones = jnp.ones((CHUNK, DH), BF16)
    f_rows = [slice(k * CHUNK, (k + 1) * CHUNK) for k in range(cps)]
    b_rows = f_rows[::-1]

    terms = []
    m_row = mst_ref[0:1, :]
    for fr, br in zip(f_rows, b_rows):
        b = jnp.where(is_fwd, gbf_ref[fr, :], gbb_ref[br, :])
        cmin = jnp.where(is_fwd, gcmf_ref[fr, :], gcmb_ref[br, :])
        total = jnp.where(is_fwd[0:1], gbf_ref[fr.stop - 1:fr.stop, :], gbb_ref[br.start:br.start + 1, :])
        cmin_end = jnp.where(is_fwd[0:1], gcmf_ref[fr.stop - 1:fr.stop, :], gcmb_ref[br.start:br.start + 1, :])
        rmax = b - cmin
        inter = b + m_row
        m_t = jnp.maximum(inter, rmax)
        gmax = total - cmin_end
        m_new = jnp.maximum(total + m_row, gmax)
        terms.append(dict(lead=cmin + (rmax - m_t), w_state=jnp.exp(inter - m_t),
                          floor=jnp.exp(-m_t), w_old=jnp.exp(total + m_row - m_new), w_new=jnp.exp(gmax - m_new)))
        m_row = m_new
    mst_ref[...] = jnp.broadcast_to(m_row, (N_HD, DH))

    pairs = []
    for j in range(N_HD):
        fwd = j < HEADS
        lanes = slice((j % HEADS) * DH, (j % HEADS + 1) * DH)
        refs = ((qf_ref, ktf_ref, vf_ref, gctf_ref, gwkf_ref, hf_ref) if fwd else
                (qb_ref, ktb_ref, vb_ref, gctb_ref, gwkb_ref, hb_ref))
        pairs.append((fwd, lanes, f_rows if fwd else b_rows) + refs)

    scores, updates, v_augs = [], [], []
    for k in range(cps):
        scores.append([_dot(q_ref[rows[k], lanes], kt_ref[lanes, rows[k]])
                       for (fwd, lanes, rows, q_ref, kt_ref, v_ref, gct_ref, gwk_ref, out_ref) in pairs])
    for k in range(cps):
        v_augs.append([jnp.concatenate([v_ref[rows[k], lanes], ones], axis=1)
                       for (fwd, lanes, rows, q_ref, kt_ref, v_ref, gct_ref, gwk_ref, out_ref) in pairs])
        updates.append([_dot((kt_ref[lanes, rows[k]].astype(F32) * gwk_ref[j:j + 1, rows[k]]).astype(BF16), v_augs[k][j])
                        for j, (fwd, lanes, rows, q_ref, kt_ref, v_ref, gct_ref, gwk_ref, out_ref) in enumerate(pairs)])
    states = [cst_ref[j] for j in range(N_HD)]
    for k in range(cps):
        tk = terms[k]
        from_state = [_dot(q_ref[rows[k], lanes], states[j].astype(BF16))
                      for j, (fwd, lanes, rows, q_ref, kt_ref, v_ref, gct_ref, gwk_ref, out_ref) in enumerate(pairs)]
        states = [tk["w_old"][:, j:j + 1] * states[j] + tk["w_new"][:, j:j + 1] * updates[k][j] for j in range(N_HD)]
        for j, (fwd, lanes, rows, q_ref, kt_ref, v_ref, gct_ref, gwk_ref, out_ref) in enumerate(pairs):
            keep = (s_idx <= t_idx) if fwd else (s_idx >= t_idx)
            decay = jnp.exp(jnp.where(keep, tk["lead"][:, j:j + 1] - gct_ref[j:j + 1, rows[k]], -jnp.inf))
            sv = _dot((scores[k][j] * decay).astype(BF16), v_augs[k][j])
            ws = jnp.broadcast_to(tk["w_state"][:, j:j + 1], (CHUNK, DH))
            num = sv[:, :DH] + ws * from_state[j][:, :DH]
            den = sv[:, DH:] + ws * from_state[j][:, DH:]
            out_ref[rows[k], lanes] = num / jnp.maximum(jnp.abs(den), tk["floor"][:, j:j + 1])
    for j in range(N_HD):
        cst_ref[j] = states[j]


def _scan(q, kt, v, gb, gcm, gct, gwk, init=None):
    B, S, _ = q.shape
    cps = min(8, S // CHUNK)
    rows = cps * CHUNK
    nb = S // rows
    fwd_tok = lambda w: pl.BlockSpec((None, rows, w), lambda b, i: (b, i, 0))
    bwd_tok = lambda w: pl.BlockSpec((None, rows, w), lambda b, i: (b, nb - 1 - i, 0))
    fwd_t = lambda r: pl.BlockSpec((None, r, rows), lambda b, i: (b, 0, i))
    bwd_t = lambda r: pl.BlockSpec((None, r, rows), lambda b, i: (b, 0, nb - 1 - i))
    c_spec = pl.BlockSpec((None, N_HD, DH, 2 * DH), lambda b, i: (b, 0, 0, 0))
    m_spec = pl.BlockSpec((None, N_HD, DH), lambda b, i: (b, 0, 0))
    init = () if init is None else tuple(init)
    return pl.pallas_call(
        functools.partial(_scan_kernel, zero_init=not init, cps=cps), grid=(B, nb),
        in_specs=[fwd_tok(HALF_W), fwd_t(HALF_W), fwd_tok(HALF_W), fwd_tok(DH), fwd_tok(DH), fwd_t(N_HD), fwd_t(N_HD),
                  bwd_tok(HALF_W), bwd_t(HALF_W), bwd_tok(HALF_W), bwd_tok(DH), bwd_tok(DH), bwd_t(N_HD), bwd_t(N_HD),
                  ] + [c_spec, m_spec][:len(init)],
        out_specs=[fwd_tok(HALF_W), bwd_tok(HALF_W), c_spec, m_spec],
        out_shape=[jax.ShapeDtypeStruct((B, S, HALF_W), F32),
                   jax.ShapeDtypeStruct((B, S, HALF_W), F32),
                   jax.ShapeDtypeStruct((B, N_HD, DH, 2 * DH), F32),
                   jax.ShapeDtypeStruct((B, N_HD, DH), F32)],
        compiler_params=pltpu.CompilerParams(
            dimension_semantics=("arbitrary", "arbitrary"), vmem_limit_bytes=VMEM_LIMIT),
        name="scan",
    )(q, kt, v, gb, gcm, gct, gwk, q, kt, v, gb, gcm, gct, gwk, *init)


def _layer_norm(z, g_ref, b_ref):
    return _norm_lanes(z) * g_ref[...] + b_ref[...]


def _ffn_kernel(x_ref, yan_ref, hf_ref, hb_ref, o_ref, mod_ref, ng_ref, wout_ref,
                ln1g_ref, ln1b_ref, wup_ref, cw_ref, cb_ref, wdown_ref, ln2g_ref, ln2b_ref,
                out_ref, perm_ref, *, tm, n_sub):
    ts = tm // n_sub
    ns = ts // NSLAB
    n_lt = D_MODEL // DH
    starts = [sum(FFN_BLOCKS[:k]) for k in range(len(FFN_BLOCKS))]
    subs = [slice(t * ts, (t + 1) * ts) for t in range(n_sub)]

    def mixer_in(rows):
        h = hf_ref[rows, :] + hb_ref[rows, :]
        gated = []
        for g in range(HEADS):
            lanes = slice(g * DH, (g + 1) * DH)
            gated.append((_norm_lanes(h[:, lanes]) * ng_ref[:, lanes] * _sigmoid(o_ref[rows, lanes])).astype(BF16))
        return jnp.concatenate([yan_ref[rows, :]] + gated, axis=1)

    def permuted_x1(t, mix):
        x1n = _layer_norm(ALPHA * x_ref[subs[t], :] + mod_ref[2:3, :] * mix, ln1g_ref, ln1b_ref)
        for c in range(n_lt):
            perm_ref[t * n_lt + c] = x1n[:, c * DH:(c + 1) * DH]
        return jnp.concatenate(
            [jnp.concatenate([perm_ref[t * n_lt + c, pl.ds(j, ns, stride=NSLAB), :] for c in range(n_lt)], axis=1)
             for j in range(NSLAB)], axis=0)

    def up(h2, k):
        return _dot(h2, wup_ref[:, 2 * starts[k]:2 * (starts[k] + FFN_BLOCKS[k])])

    def hidden(a, k):
        blk = FFN_BLOCKS[k]
        cols = slice(2 * starts[k], 2 * (starts[k] + blk))
        q8 = lax.broadcasted_iota(jnp.int32, (ns, 2 * blk), 0) % NSLAB
        before_first = jnp.where(q8 == 0, 0.0, pltpu.roll(a[ts - ns:], 1, 0))
        after_last = jnp.where(q8 == NSLAB - 1, 0.0, pltpu.roll(a[:ns], ns - 1, 0))
        left = jnp.concatenate([before_first, a[:ts - ns]], axis=0)
        right = jnp.concatenate([a[ns:], after_last], axis=0)
        a = left * cw_ref[0:1, cols] + a * cw_ref[1:2, cols] + right * cw_ref[2:3, cols] + cb_ref[:, cols]
        prod = a[:, blk:] * a[:, :blk]
        return (prod + prod * jnp.tanh(a[:, blk:])).astype(BF16)

    def finish(t, x1, acc):
        out = _layer_norm(ALPHA * x1 + mod_ref[5:6, :] * acc, ln2g_ref, ln2b_ref)
        for c in range(n_lt):
            for j in range(NSLAB):
                perm_ref[t * n_lt + c, pl.ds(j, ns, stride=NSLAB), :] = out[j * ns:(j + 1) * ns, c * DH:(c + 1) * DH]
        for c in range(n_lt):
            out_ref[subs[t], c * DH:(c + 1) * DH] = perm_ref[t * n_lt + c]

    mixes = [_dot(mixer_in(rows), wout_ref[...]) for rows in subs]
    x1s = [permuted_x1(t, mixes[t]) for t in range(n_sub)]
    h2s = [(x1 * mod_ref[4:5, :] + mod_ref[3:4, :]).astype(BF16) for x1 in x1s]

    steps = [(t, k) for t in range(n_sub) for k in range(len(FFN_BLOCKS))]
    accs = [None] * n_sub
    queue = [up(h2s[t], k) for (t, k) in steps[:UP_AHEAD]]
    for idx, (t, k) in enumerate(steps):
        if idx + UP_AHEAD < len(steps):
            queue.append(up(h2s[steps[idx + UP_AHEAD][0]], steps[idx + UP_AHEAD][1]))
        part = _dot(hidden(queue.pop(0), k), wdown_ref[starts[k]:starts[k] + FFN_BLOCKS[k], :])
        accs[t] = part if accs[t] is None else accs[t] + part
        if k == len(FFN_BLOCKS) - 1:
            finish(t, x1s[t], accs[t])


def _ffn(x, yan, hf, hb, o_pre, mod6, wts, *, tm, n_sub):
    B, S, _ = x.shape
    kern = functools.partial(_ffn_kernel, tm=tm, n_sub=n_sub)
    tok = lambda w: pl.BlockSpec((None, tm, w), lambda b, i: (b, i, 0))
    in_specs = [tok(D_MODEL), tok(HALF_W), tok(HALF_W), tok(HALF_W), tok(HALF_W),
                pl.BlockSpec((None, 6, D_MODEL), lambda b, i: (b, 0, 0))] + [_const_spec(w.shape) for w in wts]
    return pl.pallas_call(
        kern, grid=(B, S // tm), in_specs=in_specs, out_specs=tok(D_MODEL),
        out_shape=jax.ShapeDtypeStruct((B, S, D_MODEL), F32),
        scratch_shapes=[pltpu.VMEM((n_sub * (D_MODEL // DH), tm // n_sub, DH), F32)],
        compiler_params=pltpu.CompilerParams(
            dimension_semantics=("arbitrary", "arbitrary"), vmem_limit_bytes=VMEM_LIMIT),
        name="ffn",
    )(x, yan, hf, hb, o_pre, mod6, *wts)


def _interleave_blocks(val, gate, axis):
    parts, start = [], 0
    for blk in FFN_BLOCKS:
        parts.append(lax.slice_in_dim(val, start, start + blk, axis=axis))
        parts.append(lax.slice_in_dim(gate, start, start + blk, axis=axis))
        start += blk
    return jnp.concatenate(parts, axis=axis)


def kernel(x, c, ctx, c_ctx, w_ada, b_ada, w_in, gmlp_ln_g, gmlp_ws, gmlp_bs, qk_conv_w, qk_conv_b,
           b_igate, b_fgate, mix_norm_g, w_out, ln1_g, ln1_b, w_up, ffn_conv_w, ffn_conv_b, w_down,
           ln2_g, ln2_b):
    B, S, _ = x.shape
    assert DEPTH == 1 and w_in.shape[0] == 1
    l = 0

    c_rows = jnp.concatenate([c, c_ctx[None, :], jnp.zeros((8 - B - 1, D_MODEL), F32)], axis=0)
    chunk = jnp.arange(6 * D_MODEL) // D_MODEL
    bias = b_ada[l] + jnp.where((chunk == 1) | (chunk == 4), 1.0, 0.0)
    mod, w_in_b, w_g = _modulation(c_rows, w_ada[l], bias[None, :], jnp.swapaxes(w_in[l], 0, 1))
    mod_x = mod[:B].reshape(B, 6, D_MODEL)
    mod_c = jnp.broadcast_to(mod[B].reshape(1, 6, D_MODEL), (B, 6, D_MODEL))

    gbi = jnp.broadcast_to(b_igate[l].reshape(N_HD, 1), (N_HD, CHUNK))
    gbf = jnp.broadcast_to(b_fgate[l].reshape(N_HD, 1), (N_HD, CHUNK))
    ln_g = gmlp_ln_g[l].reshape(1, HALF_W)
    ws = gmlp_ws[l].astype(BF16)
    bsb = jnp.repeat(gmlp_bs[l].T, DH, axis=1)
    norm_g = mix_norm_g[l][None, :]
    in_wts = (w_in_b, w_g, ln_g, ws, bsb, qk_conv_w[l], qk_conv_b[l][None, :], gbi, gbf,
              norm_g[:, :HALF_W])

    _, q_c, kt_c, v_c, _, *gates_c = _inproj(ctx, mod_c, in_wts, tm=ctx.shape[1], mixer=False)
    _, _, c_state, m_state = _scan(q_c, kt_c, v_c, *gates_c)

    yan, q, kt, v, o_pre, *rest = _inproj(x, mod_x, in_wts, tm=512, ffn_f32=(w_up[l], w_down[l], w_out[l]))
    gates, (w_up_b, w_down_b, w_out_b) = rest[:4], rest[4:]
    hf, hb, _, _ = _scan(q, kt, v, *gates, init=(c_state, m_state))

    cw = _interleave_blocks(ffn_conv_w[l][:, :D_FF], 0.5 * ffn_conv_w[l][:, D_FF:], 1)
    cb = _interleave_blocks(ffn_conv_b[l][None, :D_FF], 0.5 * ffn_conv_b[l][None, D_FF:], 1)
    ffn_wts = (norm_g[:, HALF_W:], w_out_b, ln1_g[l][None, :], ln1_b[l][None, :],
               w_up_b, cw, cb, w_down_b, ln2_g[l][None, :], ln2_b[l][None, :])
    return _ffn(x, yan, hf, hb, o_pre, mod_x, ffn_wts, tm=512, n_sub=2)
```

```python
import functools

import jax
import jax.numpy as jnp
from jax import lax
from jax.experimental import pallas as pl
from jax.experimental.pallas import tpu as pltpu

D_MODEL = 1024
GRID_W = 64
CHUNK = 128
HEADS = 4
DH = 128
HALF_W = HEADS * DH
N_HD = 2 * HEADS
D_FF = 2688
DEPTH = 1
ALPHA = (2 * DEPTH) ** 0.25
EPS = 1e-5
NSLAB = 8
HALO = 16
FFN_BLOCKS = (768, 768, 768, 384)
UP_AHEAD = 2
LATENT_TILE = 512
FFN_SUB = 2
SCAN_CHUNKS = 4
MOD_STEPS = 8
BF16_ROWS = 16

F32 = jnp.float32
BF16 = jnp.bfloat16
V7X_VMEM_BYTES = 64 * 1024 * 1024
VMEM_LIMIT = V7X_VMEM_BYTES - 8 * 1024 * 1024


def _dot(a, b):
    return jnp.dot(a, b, preferred_element_type=F32)


def _norm_lanes(z):
    mu = jnp.mean(z, axis=-1, keepdims=True)
    d = z - mu
    var = jnp.mean(d * d, axis=-1, keepdims=True)
    return d * lax.rsqrt(var + EPS)


def _gelu_tanh(x):
    half = 0.5 * x
    return half + half * jnp.tanh(x * (0.7978845608028654 + 0.035677408136300125 * (x * x)))


def _sigmoid(x):
    return 0.5 * jnp.tanh(0.5 * x) + 0.5


def _log_sigmoid(x):
    return jnp.minimum(x, 0.0) - jnp.log(1.0 + jnp.exp(-jnp.abs(x)))


def _const_spec(shape):
    nd = len(shape)
    return pl.BlockSpec(shape, lambda *_: (0,) * nd, pipeline_mode=pl.Buffered(1))


def _mod_kernel(c_ref, w_ref, b_ref, win_ref, wg_ref, o_ref, winb_ref, wgb_ref):
    winb_ref[...] = win_ref[...].T.astype(BF16)
    wg = wg_ref[...]
    wg_rows = jnp.concatenate([wg, jnp.zeros((DH - wg.shape[0], wg.shape[1]), F32)], axis=0)
    wgb_ref[...] = wg_rows.T.astype(BF16)

    cs = c_ref[...]
    a = cs * _sigmoid(cs)
    a_hi = a.astype(BF16)
    a_lo = (a - a_hi.astype(F32)).astype(BF16)
    w = w_ref[...]
    w_hi = w.astype(BF16)
    w_lo = (w - w_hi.astype(F32)).astype(BF16)
    by_hi = _dot(jnp.concatenate([a_hi, a_lo], axis=0), w_hi)
    rows = a.shape[0]
    o_ref[...] = by_hi[:rows] + by_hi[rows:] + _dot(a_hi, w_lo) + b_ref[...]


def _modulation(c_rows, w_ada, b_ada, w_in_t):
    n_out = w_ada.shape[1]
    n_steps = MOD_STEPS
    bn = n_out // n_steps
    n_main = 6 * HALF_W
    n_gate = 2 * N_HD
    rows = n_main // n_steps
    assert w_in_t.shape == (n_main + n_gate, D_MODEL) and n_main % n_gate == 0
    return pl.pallas_call(
        _mod_kernel,
        grid=(n_steps,),
        in_specs=[pl.BlockSpec((8, D_MODEL), lambda j: (0, 0)),
                  pl.BlockSpec((D_MODEL, bn), lambda j: (0, j)),
                  pl.BlockSpec((1, bn), lambda j: (0, j)),
                  pl.BlockSpec((rows, D_MODEL), lambda j: (j, 0)),
                  pl.BlockSpec((n_gate, D_MODEL), lambda j: (n_main // n_gate, 0))],
        out_specs=[pl.BlockSpec((8, bn), lambda j: (0, j)),
                   pl.BlockSpec((D_MODEL, rows), lambda j: (0, j)),
                   pl.BlockSpec((D_MODEL, DH), lambda j: (0, 0))],
        out_shape=[jax.ShapeDtypeStruct((8, n_out), F32),
                   jax.ShapeDtypeStruct((D_MODEL, n_main), BF16),
                   jax.ShapeDtypeStruct((D_MODEL, DH), BF16)],
        compiler_params=pltpu.CompilerParams(vmem_limit_bytes=VMEM_LIMIT),
        name="mod",
    )(c_rows, w_ada, b_ada, w_in_t, w_in_t)


def _inproj_kernel(x_ref, xp_ref, xn_ref, mod_ref, win_ref, wg_ref,
                   lng_ref, ws_ref, bsb_ref, cw_ref, cb_ref, gbi_ref, gbf_ref, ng_ref,
                   *rest, tm, convert, mixer):
    if convert:
        wup_ref, wdown_ref, wout_ref = rest[:3]
        wupb_ref, wdownb_ref, woutb_ref = rest[-3:]
        rest = rest[3:-3]
        start = 0
        for blk in FFN_BLOCKS:
            wupb_ref[:, 2 * start:2 * start + blk] = wup_ref[:, start:start + blk].astype(BF16)
            wupb_ref[:, 2 * start + blk:2 * (start + blk)] = wup_ref[:, D_FF + start:D_FF + start + blk].astype(BF16)
            start += blk
        wdownb_ref[...] = wdown_ref[...].astype(BF16)
        woutb_ref[...] = wout_ref[...].astype(BF16)
    yan_ref, q_ref, kt_ref, vm_ref, o_ref, gb_ref, gcm_ref, gct_ref, gwk_ref = rest
    i = pl.program_id(1)
    n_tiles = pl.num_programs(1)
    shift = mod_ref[0:1, :]
    scale = mod_ref[1:2, :]

    def modulate(xv):
        return (xv * scale + shift).astype(BF16)

    hx = modulate(x_ref[...])
    h_prev = jnp.where(i > 0, modulate(xp_ref[...]), jnp.zeros((), BF16))
    h_next = jnp.where(i < n_tiles - 1, modulate(xn_ref[...]), jnp.zeros((), BF16))
    gp = _dot(hx, wg_ref[...])
    uv = _dot(hx, win_ref[:, 0:2 * HALF_W]) if mixer else None
    ext = _dot(jnp.concatenate([h_prev, hx, h_next], axis=0), win_ref[:, 2 * HALF_W:4 * HALF_W])
    vo = _dot(hx, win_ref[:, 4 * HALF_W:(6 if mixer else 5) * HALF_W])

    if mixer:
        u = _gelu_tanh(uv[:, :HALF_W])
        v = _gelu_tanh(uv[:, HALF_W:])
    else:
        yan_ref[...] = jnp.zeros(yan_ref.shape, yan_ref.dtype)
        o_ref[...] = jnp.zeros(o_ref.shape, o_ref.dtype)
    for g in range(HEADS if mixer else 0):
        lanes = slice(g * DH, (g + 1) * DH)
        vh = (_norm_lanes(v[:, lanes]) * lng_ref[:, lanes]).astype(BF16)
        for c in range(tm // CHUNK):
            rows = slice(c * CHUNK, (c + 1) * CHUNK)
            mixed = _dot(ws_ref[g], vh[rows]) + bsb_ref[:, lanes]
            ya = u[rows, lanes] * mixed
            yan_ref[rows, lanes] = (_norm_lanes(ya) * ng_ref[:, lanes]).astype(BF16)

    n_ext = tm + 2 * HALO
    pre = ext[HALO:HALO + tm]
    down = pltpu.roll(ext, 1, 0)[HALO:HALO + tm]
    up = pltpu.roll(ext, n_ext - 1, 0)[HALO:HALO + tm]
    conv = down * cw_ref[0:1, :] + pre * cw_ref[1:2, :] + up * cw_ref[2:3, :] + cb_ref[...]
    qk = conv * _sigmoid(conv)
    q_ref[...] = qk[:, :HALF_W].astype(BF16)
    kt_ref[...] = (qk[:, HALF_W:] * (DH ** -0.5)).T.astype(BF16)

    vm_ref[...] = vo[:, :HALF_W].astype(BF16)
    if mixer:
        o_ref[...] = vo[:, HALF_W:]

    pos = lax.broadcasted_iota(jnp.int32, (N_HD, CHUNK), 1)
    is_fwd = lax.broadcasted_iota(jnp.int32, (N_HD, CHUNK), 0) < HEADS
    pad = jnp.zeros((DH - N_HD, CHUNK), F32)

    def scan_lanes(z, op, fill, reverse):
        step = 1
        while step < CHUNK:
            if reverse:
                moved = jnp.where(pos < CHUNK - step, pltpu.roll(z, CHUNK - step, 1), fill)
            else:
                moved = jnp.where(pos >= step, pltpu.roll(z, step, 1), fill)
            z = op(z, moved)
            step *= 2
        return z

    for c in range(tm // CHUNK):
        rows = slice(c * CHUNK, (c + 1) * CHUNK)
        gt = gp[rows].T
        log_i = gt[0:N_HD] + gbi_ref[...]
        lf = _log_sigmoid(gt[N_HD:2 * N_HD] + gbf_ref[...])
        csum = scan_lanes(lf, jnp.add, 0.0, False)
        suffix = csum[:, CHUNK - 1:CHUNK] - csum + lf
        b = jnp.where(is_fwd, csum, suffix)
        cdiff = b - log_i
        cmin = jnp.where(is_fwd, scan_lanes(cdiff, jnp.minimum, jnp.inf, False),
                         scan_lanes(cdiff, jnp.minimum, jnp.inf, True))
        cmin_end = jnp.where(is_fwd[:, 0:1], cmin[:, CHUNK - 1:CHUNK], cmin[:, 0:1])
        gct_ref[:, rows] = cdiff
        gwk_ref[:, rows] = jnp.exp(cmin_end - cdiff)
        gb_ref[rows, :] = jnp.concatenate([b, pad], axis=0).T
        gcm_ref[rows, :] = jnp.concatenate([cmin, pad], axis=0).T


def _inproj(xs, mod6, wts, *, tm, ffn_f32=(), mixer=True):
    B, S, _ = xs.shape
    nt = S // tm
    hb = tm // HALO
    n_halo_blocks = S // HALO
    kern = functools.partial(_inproj_kernel, tm=tm, convert=bool(ffn_f32), mixer=mixer)
    tok = lambda w: pl.BlockSpec((None, tm, w), lambda b, i: (b, i, 0))
    in_specs = [
        tok(D_MODEL),
        pl.BlockSpec((None, HALO, D_MODEL), lambda b, i: (b, jnp.maximum(i * hb - 1, 0), 0)),
        pl.BlockSpec((None, HALO, D_MODEL), lambda b, i: (b, jnp.minimum((i + 1) * hb, n_halo_blocks - 1), 0)),
        pl.BlockSpec((None, 6, D_MODEL), lambda b, i: (b, 0, 0)),
    ] + [_const_spec(w.shape) for w in wts]
    out_shape = [
        jax.ShapeDtypeStruct((B, S, HALF_W), BF16),
        jax.ShapeDtypeStruct((B, S, HALF_W), BF16),
        jax.ShapeDtypeStruct((B, HALF_W, S), BF16),
        jax.ShapeDtypeStruct((B, S, HALF_W), BF16),
        jax.ShapeDtypeStruct((B, S, HALF_W), F32),
        jax.ShapeDtypeStruct((B, S, DH), F32),
        jax.ShapeDtypeStruct((B, S, DH), F32),
        jax.ShapeDtypeStruct((B, N_HD, S), F32),
        jax.ShapeDtypeStruct((B, N_HD, S), F32),
    ]
    out_specs = [
        tok(HALF_W), tok(HALF_W),
        pl.BlockSpec((None, HALF_W, tm), lambda b, i: (b, 0, i)),
        tok(HALF_W), tok(HALF_W), tok(DH), tok(DH),
        pl.BlockSpec((None, N_HD, tm), lambda b, i: (b, 0, i)),
        pl.BlockSpec((None, N_HD, tm), lambda b, i: (b, 0, i)),
    ]
    if ffn_f32:
        for w in ffn_f32:
            n_blocks = max(n for n in range(1, B * nt + 1)
                           if w.shape[0] % n == 0 and (w.shape[0] // n) % BF16_ROWS == 0)
            rows = w.shape[0] // n_blocks
            spec = pl.BlockSpec((rows, w.shape[1]), lambda b, i, n=n_blocks: (jnp.minimum(b * nt + i, n - 1), 0))
            in_specs.append(spec)
            out_specs.append(spec)
            out_shape.append(jax.ShapeDtypeStruct(w.shape, BF16))
    return pl.pallas_call(
        kern, grid=(B, nt), in_specs=in_specs, out_specs=out_specs, out_shape=out_shape,
        compiler_params=pltpu.CompilerParams(
            dimension_semantics=("arbitrary", "arbitrary"), vmem_limit_bytes=VMEM_LIMIT),
        name="inproj",
    )(xs, xs, xs, mod6, *wts, *ffn_f32)


def _scan_kernel(qf_ref, ktf_ref, vf_ref, gbf_ref, gcmf_ref, gctf_ref, gwkf_ref,
                 qb_ref, ktb_ref, vb_ref, gbb_ref, gcmb_ref, gctb_ref, gwkb_ref,
                 *rest, zero_init, cps):
    if zero_init:
        hf_ref, hb_ref, cst_ref, mst_ref = rest
    else:
        c0_ref, m0_ref, hf_ref, hb_ref, cst_ref, mst_ref = rest

    @pl.when(pl.program_id(1) == 0)
    def _():
        if zero_init:
            cst_ref[...] = jnp.zeros(cst_ref.shape, F32)
            mst_ref[...] = jnp.zeros(mst_ref.shape, F32)
        else:
            cst_ref[...] = c0_ref[...]
            mst_ref[...] = m0_ref[...]

    t_idx = lax.broadcasted_iota(jnp.int32, (CHUNK, CHUNK), 0)
    s_idx = lax.broadcasted_iota(jnp.int32, (CHUNK, CHUNK), 1)
    is_fwd = lax.broadcasted_iota(jnp.int32, (CHUNK, DH), 1) < HEADS
    ones = jnp.ones((CHUNK, DH), BF16)
    f_rows = [slice(k * CHUNK, (k + 1) * CHUNK) for k in range(cps)]
    b_rows = f_rows[::-1]

    terms = []
    m_row = mst_ref[0:1, :]
    for fr, br in zip(f_rows, b_rows):
        b = jnp.where(is_fwd, gbf_ref[fr, :], gbb_ref[br, :])
        cmin = jnp.where(is_fwd, gcmf_ref[fr, :], gcmb_ref[br, :])
        total = jnp.where(is_fwd[0:1], gbf_ref[fr.stop - 1:fr.stop, :], gbb_ref[br.start:br.start + 1, :])
        cmin_end = jnp.where(is_fwd[0:1], gcmf_ref[fr.stop - 1:fr.stop, :], gcmb_ref[br.start:br.start + 1, :])
        rmax = b - cmin
        inter = b + m_row
        m_t = jnp.maximum(inter, rmax)
        gmax = total - cmin_end
        m_new = jnp.maximum(total + m_row, gmax)
        terms.append(dict(lead=cmin + (rmax - m_t), w_state=jnp.exp(inter - m_t),
                          floor=jnp.exp(-m_t), w_old=jnp.exp(total + m_row - m_new), w_new=jnp.exp(gmax - m_new)))
        m_row = m_new
    mst_ref[...] = jnp.broadcast_to(m_row, (N_HD, DH))

    pairs = []
    for j in range(N_HD):
        fwd = j < HEADS
        lanes = slice((j % HEADS) * DH, (j % HEADS + 1) * DH)
        refs = ((qf_ref, ktf_ref, vf_ref, gctf_ref, gwkf_ref, hf_ref) if fwd else
                (qb_ref, ktb_ref, vb_ref, gctb_ref, gwkb_ref, hb_ref))
        pairs.append((fwd, lanes, f_rows if fwd else b_rows) + refs)

    scores, updates, v_augs = [], [], []
    for k in range(cps):
        scores.append([_dot(q_ref[rows[k], lanes], kt_ref[lanes, rows[k]])
                       for (fwd, lanes, rows, q_ref, kt_ref, v_ref, gct_ref, gwk_ref, out_ref) in pairs])
    for k in range(cps):
        v_augs.append([jnp.concatenate([v_ref[rows[k], lanes], ones], axis=1)
                       for (fwd, lanes, rows, q_ref, kt_ref, v_ref, gct_ref, gwk_ref, out_ref) in pairs])
        updates.append([_dot((kt_ref[lanes, rows[k]].astype(F32) * gwk_ref[j:j + 1, rows[k]]).astype(BF16), v_augs[k][j])
                        for j, (fwd, lanes, rows, q_ref, kt_ref, v_ref, gct_ref, gwk_ref, out_ref) in enumerate(pairs)])
    states = [cst_ref[j] for j in range(N_HD)]
    for k in range(cps):
        tk = terms[k]
        from_state = [_dot(q_ref[rows[k], lanes], states[j].astype(BF16))
                      for j, (fwd, lanes, rows, q_ref, kt_ref, v_ref, gct_ref, gwk_ref, out_ref) in enumerate(pairs)]
        states = [tk["w_old"][:, j:j + 1] * states[j] + tk["w_new"][:, j:j + 1] * updates[k][j] for j in range(N_HD)]
        for j, (fwd, lanes, rows, q_ref, kt_ref, v_ref, gct_ref, gwk_ref, out_ref) in enumerate(pairs):
            keep = (s_idx <= t_idx) if fwd else (s_idx >= t_idx)
            decay = jnp.exp(jnp.where(keep, tk["lead"][:, j:j + 1] - gct_ref[j:j + 1, rows[k]], -jnp.inf))
            sv = _dot((scores[k][j] * decay).astype(BF16), v_augs[k][j])
            ws = jnp.broadcast_to(tk["w_state"][:, j:j + 1], (CHUNK, DH))
            num = sv[:, :DH] + ws * from_state[j][:, :DH]
            den = sv[:, DH:] + ws * from_state[j][:, DH:]
            out_ref[rows[k], lanes] = num / jnp.maximum(jnp.abs(den), tk["floor"][:, j:j + 1])
    for j in range(N_HD):
        cst_ref[j] = states[j]


def _scan(q, kt, v, gb, gcm, gct, gwk, init=None):
    B, S, _ = q.shape
    cps = min(SCAN_CHUNKS, S // CHUNK)
    rows = cps * CHUNK
    nb = S // rows
    fwd_tok = lambda w: pl.BlockSpec((None, rows, w), lambda b, i: (b, i, 0))
    bwd_tok = lambda w: pl.BlockSpec((None, rows, w), lambda b, i: (b, nb - 1 - i, 0))
    fwd_t = lambda r: pl.BlockSpec((None, r, rows), lambda b, i: (b, 0, i))
    bwd_t = lambda r: pl.BlockSpec((None, r, rows), lambda b, i: (b, 0, nb - 1 - i))
    c_spec = pl.BlockSpec((None, N_HD, DH, 2 * DH), lambda b, i: (b, 0, 0, 0))
    m_spec = pl.BlockSpec((None, N_HD, DH), lambda b, i: (b, 0, 0))
    init = () if init is None else tuple(init)
    return pl.pallas_call(
        functools.partial(_scan_kernel, zero_init=not init, cps=cps), grid=(B, nb),
        in_specs=[fwd_tok(HALF_W), fwd_t(HALF_W), fwd_tok(HALF_W), fwd_tok(DH), fwd_tok(DH), fwd_t(N_HD), fwd_t(N_HD),
                  bwd_tok(HALF_W), bwd_t(HALF_W), bwd_tok(HALF_W), bwd_tok(DH), bwd_tok(DH), bwd_t(N_HD), bwd_t(N_HD),
                  ] + [c_spec, m_spec][:len(init)],
        out_specs=[fwd_tok(HALF_W), bwd_tok(HALF_W), c_spec, m_spec],
        out_shape=[jax.ShapeDtypeStruct((B, S, HALF_W), F32),
                   jax.ShapeDtypeStruct((B, S, HALF_W), F32),
                   jax.ShapeDtypeStruct((B, N_HD, DH, 2 * DH), F32),
                   jax.ShapeDtypeStruct((B, N_HD, DH), F32)],
        compiler_params=pltpu.CompilerParams(
            dimension_semantics=("arbitrary", "arbitrary"), vmem_limit_bytes=VMEM_LIMIT),
        name="scan",
    )(q, kt, v, gb, gcm, gct, gwk, q, kt, v, gb, gcm, gct, gwk, *init)


def _layer_norm(z, g_ref, b_ref):
    return _norm_lanes(z) * g_ref[...] + b_ref[...]


def _ffn_kernel(x_ref, yan_ref, hf_ref, hb_ref, o_ref, mod_ref, ng_ref, wout_ref,
                ln1g_ref, ln1b_ref, wup_ref, cw_ref, cb_ref, wdown_ref, ln2g_ref, ln2b_ref,
                out_ref, perm_ref, *, tm, n_sub):
    ts = tm // n_sub
    ns = ts // NSLAB
    n_lt = D_MODEL // DH
    starts = [sum(FFN_BLOCKS[:k]) for k in range(len(FFN_BLOCKS))]
    subs = [slice(t * ts, (t + 1) * ts) for t in range(n_sub)]

    def mixer_in(rows):
        h = hf_ref[rows, :] + hb_ref[rows, :]
        gated = []
        for g in range(HEADS):
            lanes = slice(g * DH, (g + 1) * DH)
            gated.append((_norm_lanes(h[:, lanes]) * ng_ref[:, lanes] * _sigmoid(o_ref[rows, lanes])).astype(BF16))
        return jnp.concatenate([yan_ref[rows, :]] + gated, axis=1)

    def permuted_x1(t, mix):
        x1n = _layer_norm(ALPHA * x_ref[subs[t], :] + mod_ref[2:3, :] * mix, ln1g_ref, ln1b_ref)
        for c in range(n_lt):
            perm_ref[t * n_lt + c] = x1n[:, c * DH:(c + 1) * DH]
        return jnp.concatenate(
            [jnp.concatenate([perm_ref[t * n_lt + c, pl.ds(j, ns, stride=NSLAB), :] for c in range(n_lt)], axis=1)
             for j in range(NSLAB)], axis=0)

    def up(h2, k):
        return _dot(h2, wup_ref[:, 2 * starts[k]:2 * (starts[k] + FFN_BLOCKS[k])])

    def hidden(a, k):
        blk = FFN_BLOCKS[k]
        cols = slice(2 * starts[k], 2 * (starts[k] + blk))
        q8 = lax.broadcasted_iota(jnp.int32, (ns, 2 * blk), 0) % NSLAB
        before_first = jnp.where(q8 == 0, 0.0, pltpu.roll(a[ts - ns:], 1, 0))
        after_last = jnp.where(q8 == NSLAB - 1, 0.0, pltpu.roll(a[:ns], ns - 1, 0))
        left = jnp.concatenate([before_first, a[:ts - ns]], axis=0)
        right = jnp.concatenate([a[ns:], after_last], axis=0)
        a = left * cw_ref[0:1, cols] + a * cw_ref[1:2, cols] + right * cw_ref[2:3, cols] + cb_ref[:, cols]
        prod = a[:, blk:] * a[:, :blk]
        return (prod + prod * jnp.tanh(a[:, blk:])).astype(BF16)

    def finish(t, x1, acc):
        out = _layer_norm(ALPHA * x1 + mod_ref[5:6, :] * acc, ln2g_ref, ln2b_ref)
        for c in range(n_lt):
            for j in range(NSLAB):
                perm_ref[t * n_lt + c, pl.ds(j, ns, stride=NSLAB), :] = out[j * ns:(j + 1) * ns, c * DH:(c + 1) * DH]
        for c in range(n_lt):
            out_ref[subs[t], c * DH:(c + 1) * DH] = perm_ref[t * n_lt + c]

    mixes = [_dot(mixer_in(rows), wout_ref[...]) for rows in subs]
    x1s = [permuted_x1(t, mixes[t]) for t in range(n_sub)]
    h2s = [(x1 * mod_ref[4:5, :] + mod_ref[3:4, :]).astype(BF16) for x1 in x1s]

    steps = [(t, k) for t in range(n_sub) for k in range(len(FFN_BLOCKS))]
    accs = [None] * n_sub
    queue = [up(h2s[t], k) for (t, k) in steps[:UP_AHEAD]]
    for idx, (t, k) in enumerate(steps):
        if idx + UP_AHEAD < len(steps):
            queue.append(up(h2s[steps[idx + UP_AHEAD][0]], steps[idx + UP_AHEAD][1]))
        part = _dot(hidden(queue.pop(0), k), wdown_ref[starts[k]:starts[k] + FFN_BLOCKS[k], :])
        accs[t] = part if accs[t] is None else accs[t] + part
        if k == len(FFN_BLOCKS) - 1:
            finish(t, x1s[t], accs[t])


def _ffn(x, yan, hf, hb, o_pre, mod6, wts, *, tm, n_sub):
    B, S, _ = x.shape
    kern = functools.partial(_ffn_kernel, tm=tm, n_sub=n_sub)
    tok = lambda w: pl.BlockSpec((None, tm, w), lambda b, i: (b, i, 0))
    in_specs = [tok(D_MODEL), tok(HALF_W), tok(HALF_W), tok(HALF_W), tok(HALF_W),
                pl.BlockSpec((None, 6, D_MODEL), lambda b, i: (b, 0, 0))] + [_const_spec(w.shape) for w in wts]
    return pl.pallas_call(
        kern, grid=(B, S // tm), in_specs=in_specs, out_specs=tok(D_MODEL),
        out_shape=jax.ShapeDtypeStruct((B, S, D_MODEL), F32),
        scratch_shapes=[pltpu.VMEM((n_sub * (D_MODEL // DH), tm // n_sub, DH), F32)],
        compiler_params=pltpu.CompilerParams(
            dimension_semantics=("arbitrary", "arbitrary"), vmem_limit_bytes=VMEM_LIMIT),
        name="ffn",
    )(x, yan, hf, hb, o_pre, mod6, *wts)


def _interleave_blocks(val, gate, axis):
    parts, start = [], 0
    for blk in FFN_BLOCKS:
        parts.append(lax.slice_in_dim(val, start, start + blk, axis=axis))
        parts.append(lax.slice_in_dim(gate, start, start + blk, axis=axis))
        start += blk
    return jnp.concatenate(parts, axis=axis)


def kernel(x, c, ctx, c_ctx, w_ada, b_ada, w_in, gmlp_ln_g, gmlp_ws, gmlp_bs, qk_conv_w, qk_conv_b,
           b_igate, b_fgate, mix_norm_g, w_out, ln1_g, ln1_b, w_up, ffn_conv_w, ffn_conv_b, w_down,
           ln2_g, ln2_b):
    B, S, _ = x.shape
    assert DEPTH == 1 and w_in.shape[0] == 1
    l = 0

    c_rows = jnp.concatenate([c, c_ctx[None, :], jnp.zeros((8 - B - 1, D_MODEL), F32)], axis=0)
    chunk = jnp.arange(6 * D_MODEL) // D_MODEL
    bias = b_ada[l] + jnp.where((chunk == 1) | (chunk == 4), 1.0, 0.0)
    mod, w_in_b, w_g = _modulation(c_rows, w_ada[l], bias[None, :], jnp.swapaxes(w_in[l], 0, 1))
    mod_x = mod[:B].reshape(B, 6, D_MODEL)
    mod_c = jnp.broadcast_to(mod[B].reshape(1, 6, D_MODEL), (B, 6, D_MODEL))

    gbi = jnp.broadcast_to(b_igate[l].reshape(N_HD, 1), (N_HD, CHUNK))
    gbf = jnp.broadcast_to(b_fgate[l].reshape(N_HD, 1), (N_HD, CHUNK))
    ln_g = gmlp_ln_g[l].reshape(1, HALF_W)
    ws = gmlp_ws[l].astype(BF16)
    bsb = jnp.repeat(gmlp_bs[l].T, DH, axis=1)
    norm_g = mix_norm_g[l][None, :]
    in_wts = (w_in_b, w_g, ln_g, ws, bsb, qk_conv_w[l], qk_conv_b[l][None, :], gbi, gbf,
              norm_g[:, :HALF_W])

    _, q_c, kt_c, v_c, _, *gates_c = _inproj(ctx, mod_c, in_wts, tm=ctx.shape[1], mixer=False)
    _, _, c_state, m_state = _scan(q_c, kt_c, v_c, *gates_c)

    yan, q, kt, v, o_pre, *rest = _inproj(x, mod_x, in_wts, tm=LATENT_TILE,
                                          ffn_f32=(w_up[l], w_down[l], w_out[l]))
    gates, (w_up_b, w_down_b, w_out_b) = rest[:4], rest[4:]
    hf, hb, _, _ = _scan(q, kt, v, *gates, init=(c_state, m_state))

    cw = _interleave_blocks(ffn_conv_w[l][:, :D_FF], 0.5 * ffn_conv_w[l][:, D_FF:], 1)
    cb = _interleave_blocks(ffn_conv_b[l][None, :D_FF], 0.5 * ffn_conv_b[l][None, D_FF:], 1)
    ffn_wts = (norm_g[:, HALF_W:], w_out_b, ln1_g[l][None, :], ln1_b[l][None, :],
               w_up_b, cw, cb, w_down_b, ln2_g[l][None, :], ln2_b[l][None, :])
    return _ffn(x, yan, hf, hb, o_pre, mod_x, ffn_wts, tm=LATENT_TILE, n_sub=FFN_SUB)
```

```python
import functools

import jax
import jax.numpy as jnp
from jax import lax
from jax.experimental import pallas as pl
from jax.experimental.pallas import tpu as pltpu

D_MODEL = 1024
GRID_W = 64
CHUNK = 128
HEADS = 4
DH = 128
HALF_W = HEADS * DH
N_HD = 2 * HEADS
D_FF = 2688
DEPTH = 1
ALPHA = (2 * DEPTH) ** 0.25
EPS = 1e-5
NSLAB = 8
HALO = 16
FFN_BLOCKS = (768, 768, 768, 384)
UP_AHEAD = 2
LATENT_TILE = 512
FFN_SUB = 2
SCAN_CHUNKS = 4
MOD_STEPS = 8
BF16_ROWS = 16

F32 = jnp.float32
BF16 = jnp.bfloat16
V7X_VMEM_BYTES = 64 * 1024 * 1024
VMEM_LIMIT = V7X_VMEM_BYTES - 8 * 1024 * 1024


def _dot(a, b):
    return jnp.dot(a, b, preferred_element_type=F32)


def _norm_lanes(z):
    mu = jnp.mean(z, axis=-1, keepdims=True)
    d = z - mu
    var = jnp.mean(d * d, axis=-1, keepdims=True)
    return d * lax.rsqrt(var + EPS)


def _gelu_tanh(x):
    half = 0.5 * x
    return half + half * jnp.tanh(x * (0.7978845608028654 + 0.035677408136300125 * (x * x)))


def _sigmoid(x):
    return 0.5 * jnp.tanh(0.5 * x) + 0.5


def _log_sigmoid(x):
    return jnp.minimum(x, 0.0) - jnp.log(1.0 + jnp.exp(-jnp.abs(x)))


def _const_spec(shape):
    nd = len(shape)
    return pl.BlockSpec(shape, lambda *_: (0,) * nd, pipeline_mode=pl.Buffered(1))


def _mod_kernel(c_ref, w_ref, b_ref, win_ref, wg_ref, o_ref, winb_ref, wgb_ref):
    winb_ref[...] = win_ref[...].T.astype(BF16)
    wg = wg_ref[...]
    wg_rows = jnp.concatenate([wg, jnp.zeros((DH - wg.shape[0], wg.shape[1]), F32)], axis=0)
    wgb_ref[...] = wg_rows.T.astype(BF16)

    cs = c_ref[...]
    a = cs * _sigmoid(cs)
    a_hi = a.astype(BF16)
    a_lo = (a - a_hi.astype(F32)).astype(BF16)
    w = w_ref[...]
    w_hi = w.astype(BF16)
    w_lo = (w - w_hi.astype(F32)).astype(BF16)
    by_hi = _dot(jnp.concatenate([a_hi, a_lo], axis=0), w_hi)
    rows = a.shape[0]
    o_ref[...] = by_hi[:rows] + by_hi[rows:] + _dot(a_hi, w_lo) + b_ref[...]


def _modulation(c_rows, w_ada, b_ada, w_in_t):
    n_out = w_ada.shape[1]
    n_steps = MOD_STEPS
    bn = n_out // n_steps
    n_main = 6 * HALF_W
    n_gate = 2 * N_HD
    rows = n_main // n_steps
    assert w_in_t.shape == (n_main + n_gate, D_MODEL) and n_main % n_gate == 0
    return pl.pallas_call(
        _mod_kernel,
        grid=(n_steps,),
        in_specs=[pl.BlockSpec((8, D_MODEL), lambda j: (0, 0)),
                  pl.BlockSpec((D_MODEL, bn), lambda j: (0, j)),
                  pl.BlockSpec((1, bn), lambda j: (0, j)),
                  pl.BlockSpec((rows, D_MODEL), lambda j: (j, 0)),
                  pl.BlockSpec((n_gate, D_MODEL), lambda j: (n_main // n_gate, 0))],
        out_specs=[pl.BlockSpec((8, bn), lambda j: (0, j)),
                   pl.BlockSpec((D_MODEL, rows), lambda j: (0, j)),
                   pl.BlockSpec((D_MODEL, DH), lambda j: (0, 0))],
        out_shape=[jax.ShapeDtypeStruct((8, n_out), F32),
                   jax.ShapeDtypeStruct((D_MODEL, n_main), BF16),
                   jax.ShapeDtypeStruct((D_MODEL, DH), BF16)],
        compiler_params=pltpu.CompilerParams(vmem_limit_bytes=VMEM_LIMIT),
        name="mod",
    )(c_rows, w_ada, b_ada, w_in_t, w_in_t)


def _inproj_kernel(x_ref, xp_ref, xn_ref, mod_ref, win_ref, wg_ref,
                   lng_ref, ws_ref, bsb_ref, cw_ref, cb_ref, gbi_ref, gbf_ref, ng_ref,
                   *rest, tm, convert, mixer):
    if convert:
        wup_ref, wdown_ref, wout_ref = rest[:3]
        wupb_ref, wdownb_ref, woutb_ref = rest[-3:]
        rest = rest[3:-3]
        start = 0
        for blk in FFN_BLOCKS:
            wupb_ref[:, 2 * start:2 * start + blk] = wup_ref[:, start:start + blk].astype(BF16)
            wupb_ref[:, 2 * start + blk:2 * (start + blk)] = wup_ref[:, D_FF + start:D_FF + start + blk].astype(BF16)
            start += blk
        wdownb_ref[...] = wdown_ref[...].astype(BF16)
        woutb_ref[...] = wout_ref[...].astype(BF16)
    yan_ref, q_ref, kt_ref, vm_ref, o_ref, gb_ref, gcm_ref, gct_ref, gwk_ref = rest
    i = pl.program_id(1)
    n_tiles = pl.num_programs(1)
    shift = mod_ref[0:1, :]
    scale = mod_ref[1:2, :]

    def modulate(xv):
        return (xv * scale + shift).astype(BF16)

    hx = modulate(x_ref[...])
    h_prev = jnp.where(i > 0, modulate(xp_ref[...]), jnp.zeros((), BF16))
    h_next = jnp.where(i < n_tiles - 1, modulate(xn_ref[...]), jnp.zeros((), BF16))
    gp = _dot(hx, wg_ref[...])
    uv = _dot(hx, win_ref[:, 0:2 * HALF_W]) if mixer else None
    ext = _dot(jnp.concatenate([h_prev, hx, h_next], axis=0), win_ref[:, 2 * HALF_W:4 * HALF_W])
    vo = _dot(hx, win_ref[:, 4 * HALF_W:(6 if mixer else 5) * HALF_W])

    if mixer:
        u = _gelu_tanh(uv[:, :HALF_W])
        v = _gelu_tanh(uv[:, HALF_W:])
    else:
        yan_ref[...] = jnp.zeros(yan_ref.shape, yan_ref.dtype)
        o_ref[...] = jnp.zeros(o_ref.shape, o_ref.dtype)
    for g in range(HEADS if mixer else 0):
        lanes = slice(g * DH, (g + 1) * DH)
        vh = (_norm_lanes(v[:, lanes]) * lng_ref[:, lanes]).astype(BF16)
        for c in range(tm // CHUNK):
            rows = slice(c * CHUNK, (c + 1) * CHUNK)
            mixed = _dot(ws_ref[g], vh[rows]) + bsb_ref[:, lanes]
            ya = u[rows, lanes] * mixed
            yan_ref[rows, lanes] = (_norm_lanes(ya) * ng_ref[:, lanes]).astype(BF16)

    n_ext = tm + 2 * HALO
    pre = ext[HALO:HALO + tm]
    down = pltpu.roll(ext, 1, 0)[HALO:HALO + tm]
    up = pltpu.roll(ext, n_ext - 1, 0)[HALO:HALO + tm]
    conv = down * cw_ref[0:1, :] + pre * cw_ref[1:2, :] + up * cw_ref[2:3, :] + cb_ref[...]
    qk = conv * _sigmoid(conv)
    q_ref[...] = qk[:, :HALF_W].astype(BF16)
    kt_ref[...] = (qk[:, HALF_W:] * (DH ** -0.5)).T.astype(BF16)

    vm_ref[...] = vo[:, :HALF_W].astype(BF16)
    if mixer:
        o_ref[...] = vo[:, HALF_W:]

    pos = lax.broadcasted_iota(jnp.int32, (N_HD, CHUNK), 1)
    is_fwd = lax.broadcasted_iota(jnp.int32, (N_HD, CHUNK), 0) < HEADS
    pad = jnp.zeros((DH - N_HD, CHUNK), F32)

    def scan_lanes(z, op, fill, reverse):
        step = 1
        while step < CHUNK:
            if reverse:
                moved = jnp.where(pos < CHUNK - step, pltpu.roll(z, CHUNK - step, 1), fill)
            else:
                moved = jnp.where(pos >= step, pltpu.roll(z, step, 1), fill)
            z = op(z, moved)
            step *= 2
        return z

    for c in range(tm // CHUNK):
        rows = slice(c * CHUNK, (c + 1) * CHUNK)
        gt = gp[rows].T
        log_i = gt[0:N_HD] + gbi_ref[...]
        lf = _log_sigmoid(gt[N_HD:2 * N_HD] + gbf_ref[...])
        csum = scan_lanes(lf, jnp.add, 0.0, False)
        suffix = csum[:, CHUNK - 1:CHUNK] - csum + lf
        b = jnp.where(is_fwd, csum, suffix)
        cdiff = b - log_i
        cmin = jnp.where(is_fwd, scan_lanes(cdiff, jnp.minimum, jnp.inf, False),
                         scan_lanes(cdiff, jnp.minimum, jnp.inf, True))
        cmin_end = jnp.where(is_fwd[:, 0:1], cmin[:, CHUNK - 1:CHUNK], cmin[:, 0:1])
        gct_ref[:, rows] = cdiff
        gwk_ref[:, rows] = jnp.exp(cmin_end - cdiff)
        gb_ref[rows, :] = jnp.concatenate([b, pad], axis=0).T
        gcm_ref[rows, :] = jnp.concatenate([cmin, pad], axis=0).T


def _inproj(xs, mod6, wts, *, tm, ffn_f32=(), mixer=True):
    B, S, _ = xs.shape
    nt = S // tm
    hb = tm // HALO
    n_halo_blocks = S // HALO
    kern = functools.partial(_inproj_kernel, tm=tm, convert=bool(ffn_f32), mixer=mixer)
    tok = lambda w: pl.BlockSpec((None, tm, w), lambda b, i: (b, i, 0))
    in_specs = [
        tok(D_MODEL),
        pl.BlockSpec((None, HALO, D_MODEL), lambda b, i: (b, jnp.maximum(i * hb - 1, 0), 0)),
        pl.BlockSpec((None, HALO, D_MODEL), lambda b, i: (b, jnp.minimum((i + 1) * hb, n_halo_blocks - 1), 0)),
        pl.BlockSpec((None, 6, D_MODEL), lambda b, i: (b, 0, 0)),
    ] + [_const_spec(w.shape) for w in wts]
    out_shape = [
        jax.ShapeDtypeStruct((B, S, HALF_W), BF16),
        jax.ShapeDtypeStruct((B, S, HALF_W), BF16),
        jax.ShapeDtypeStruct((B, nt, HALF_W, tm), BF16),
        jax.ShapeDtypeStruct((B, S, HALF_W), BF16),
        jax.ShapeDtypeStruct((B, S, HALF_W), F32),
        jax.ShapeDtypeStruct((B, S, DH), F32),
        jax.ShapeDtypeStruct((B, S, DH), F32),
        jax.ShapeDtypeStruct((B, nt, N_HD, tm), F32),
        jax.ShapeDtypeStruct((B, nt, N_HD, tm), F32),
    ]
    out_specs = [
        tok(HALF_W), tok(HALF_W),
        pl.BlockSpec((None, None, HALF_W, tm), lambda b, i: (b, i, 0, 0)),
        tok(HALF_W), tok(HALF_W), tok(DH), tok(DH),
        pl.BlockSpec((None, None, N_HD, tm), lambda b, i: (b, i, 0, 0)),
        pl.BlockSpec((None, None, N_HD, tm), lambda b, i: (b, i, 0, 0)),
    ]
    if ffn_f32:
        for w in ffn_f32:
            n_blocks = max(n for n in range(1, B * nt + 1)
                           if w.shape[0] % n == 0 and (w.shape[0] // n) % BF16_ROWS == 0)
            rows = w.shape[0] // n_blocks
            spec = pl.BlockSpec((rows, w.shape[1]), lambda b, i, n=n_blocks: (jnp.minimum(b * nt + i, n - 1), 0))
            in_specs.append(spec)
            out_specs.append(spec)
            out_shape.append(jax.ShapeDtypeStruct(w.shape, BF16))
    return pl.pallas_call(
        kern, grid=(B, nt), in_specs=in_specs, out_specs=out_specs, out_shape=out_shape,
        compiler_params=pltpu.CompilerParams(
            dimension_semantics=("arbitrary", "arbitrary"), vmem_limit_bytes=VMEM_LIMIT),
        name="inproj",
    )(xs, xs, xs, mod6, *wts, *ffn_f32)


def _scan_kernel(qf_ref, ktf_ref, vf_ref, gbf_ref, gcmf_ref, gctf_ref, gwkf_ref,
                 qb_ref, ktb_ref, vb_ref, gbb_ref, gcmb_ref, gctb_ref, gwkb_ref,
                 *rest, zero_init, cps):
    if zero_init:
        hf_ref, hb_ref, cst_ref, mst_ref = rest
    else:
        c0_ref, m0_ref, hf_ref, hb_ref, cst_ref, mst_ref = rest

    @pl.when(pl.program_id(1) == 0)
    def _():
        if zero_init:
            cst_ref[...] = jnp.zeros(cst_ref.shape, F32)
            mst_ref[...] = jnp.zeros(mst_ref.shape, F32)
        else:
            cst_ref[...] = c0_ref[...]
            mst_ref[...] = m0_ref[...]

    t_idx = lax.broadcasted_iota(jnp.int32, (CHUNK, CHUNK), 0)
    s_idx = lax.broadcasted_iota(jnp.int32, (CHUNK, CHUNK), 1)
    is_fwd = lax.broadcasted_iota(jnp.int32, (CHUNK, DH), 1) < HEADS
    ones = jnp.ones((CHUNK, DH), BF16)
    f_rows = [slice(k * CHUNK, (k + 1) * CHUNK) for k in range(cps)]
    b_rows = f_rows[::-1]

    terms = []
    m_row = mst_ref[0:1, :]
    for fr, br in zip(f_rows, b_rows):
        b = jnp.where(is_fwd, gbf_ref[fr, :], gbb_ref[br, :])
        cmin = jnp.where(is_fwd, gcmf_ref[fr, :], gcmb_ref[br, :])
        total = jnp.where(is_fwd[0:1], gbf_ref[fr.stop - 1:fr.stop, :], gbb_ref[br.start:br.start + 1, :])
        cmin_end = jnp.where(is_fwd[0:1], gcmf_ref[fr.stop - 1:fr.stop, :], gcmb_ref[br.start:br.start + 1, :])
        rmax = b - cmin
        inter = b + m_row
        m_t = jnp.maximum(inter, rmax)
        gmax = total - cmin_end
        m_new = jnp.maximum(total + m_row, gmax)
        terms.append(dict(lead=cmin + (rmax - m_t), w_state=jnp.exp(inter - m_t),
                          floor=jnp.exp(-m_t), w_old=jnp.exp(total + m_row - m_new), w_new=jnp.exp(gmax - m_new)))
        m_row = m_new
    mst_ref[...] = jnp.broadcast_to(m_row, (N_HD, DH))

    pairs = []
    for j in range(N_HD):
        fwd = j < HEADS
        lanes = slice((j % HEADS) * DH, (j % HEADS + 1) * DH)
        refs = ((qf_ref, ktf_ref, vf_ref, gctf_ref, gwkf_ref, hf_ref) if fwd else
                (qb_ref, ktb_ref, vb_ref, gctb_ref, gwkb_ref, hb_ref))
        pairs.append((fwd, lanes, f_rows if fwd else b_rows) + refs)

    scores, updates, v_augs = [], [], []
    for k in range(cps):
        scores.append([_dot(q_ref[rows[k], lanes], kt_ref[lanes, rows[k]])
                       for (fwd, lanes, rows, q_ref, kt_ref, v_ref, gct_ref, gwk_ref, out_ref) in pairs])
    for k in range(cps):
        v_augs.append([jnp.concatenate([v_ref[rows[k], lanes], ones], axis=1)
                       for (fwd, lanes, rows, q_ref, kt_ref, v_ref, gct_ref, gwk_ref, out_ref) in pairs])
        updates.append([_dot((kt_ref[lanes, rows[k]].astype(F32) * gwk_ref[j:j + 1, rows[k]]).astype(BF16), v_augs[k][j])
                        for j, (fwd, lanes, rows, q_ref, kt_ref, v_ref, gct_ref, gwk_ref, out_ref) in enumerate(pairs)])
    states = [cst_ref[j] for j in range(N_HD)]
    for k in range(cps):
        tk = terms[k]
        from_state = [_dot(q_ref[rows[k], lanes], states[j].astype(BF16))
                      for j, (fwd, lanes, rows, q_ref, kt_ref, v_ref, gct_ref, gwk_ref, out_ref) in enumerate(pairs)]
        states = [tk["w_old"][:, j:j + 1] * states[j] + tk["w_new"][:, j:j + 1] * updates[k][j] for j in range(N_HD)]
        for j, (fwd, lanes, rows, q_ref, kt_ref, v_ref, gct_ref, gwk_ref, out_ref) in enumerate(pairs):
            keep = (s_idx <= t_idx) if fwd else (s_idx >= t_idx)
            decay = jnp.exp(jnp.where(keep, tk["lead"][:, j:j + 1] - gct_ref[j:j + 1, rows[k]], -jnp.inf))
            sv = _dot((scores[k][j] * decay).astype(BF16), v_augs[k][j])
            ws = jnp.broadcast_to(tk["w_state"][:, j:j + 1], (CHUNK, DH))
            num = sv[:, :DH] + ws * from_state[j][:, :DH]
            den = sv[:, DH:] + ws * from_state[j][:, DH:]
            out_ref[rows[k], lanes] = num / jnp.maximum(jnp.abs(den), tk["floor"][:, j:j + 1])
    for j in range(N_HD):
        cst_ref[j] = states[j]


def _scan(q, kt, v, gb, gcm, gct, gwk, init=None):
    B, S, _ = q.shape
    cps = min(SCAN_CHUNKS, S // CHUNK)
    rows = cps * CHUNK
    nb = S // rows
    fwd_tok = lambda w: pl.BlockSpec((None, rows, w), lambda b, i: (b, i, 0))
    bwd_tok = lambda w: pl.BlockSpec((None, rows, w), lambda b, i: (b, nb - 1 - i, 0))
    assert kt.shape[1:] == (nb, HALF_W, rows)
    fwd_t = lambda r: pl.BlockSpec((None, None, r, rows), lambda b, i: (b, i, 0, 0))
    bwd_t = lambda r: pl.BlockSpec((None, None, r, rows), lambda b, i: (b, nb - 1 - i, 0, 0))
    c_spec = pl.BlockSpec((None, N_HD, DH, 2 * DH), lambda b, i: (b, 0, 0, 0))
    m_spec = pl.BlockSpec((None, N_HD, DH), lambda b, i: (b, 0, 0))
    init = () if init is None else tuple(init)
    return pl.pallas_call(
        functools.partial(_scan_kernel, zero_init=not init, cps=cps), grid=(B, nb),
        in_specs=[fwd_tok(HALF_W), fwd_t(HALF_W), fwd_tok(HALF_W), fwd_tok(DH), fwd_tok(DH), fwd_t(N_HD), fwd_t(N_HD),
                  bwd_tok(HALF_W), bwd_t(HALF_W), bwd_tok(HALF_W), bwd_tok(DH), bwd_tok(DH), bwd_t(N_HD), bwd_t(N_HD),
                  ] + [c_spec, m_spec][:len(init)],
        out_specs=[fwd_tok(HALF_W), bwd_tok(HALF_W), c_spec, m_spec],
        out_shape=[jax.ShapeDtypeStruct((B, S, HALF_W), F32),
                   jax.ShapeDtypeStruct((B, S, HALF_W), F32),
                   jax.ShapeDtypeStruct((B, N_HD, DH, 2 * DH), F32),
                   jax.ShapeDtypeStruct((B, N_HD, DH), F32)],
        compiler_params=pltpu.CompilerParams(
            dimension_semantics=("arbitrary", "arbitrary"), vmem_limit_bytes=VMEM_LIMIT),
        name="scan",
    )(q, kt, v, gb, gcm, gct, gwk, q, kt, v, gb, gcm, gct, gwk, *init)


def _layer_norm(z, g_ref, b_ref):
    return _norm_lanes(z) * g_ref[...] + b_ref[...]


def _ffn_kernel(x_ref, yan_ref, hf_ref, hb_ref, o_ref, mod_ref, ng_ref, wout_ref,
                ln1g_ref, ln1b_ref, wup_ref, cw_ref, cb_ref, wdown_ref, ln2g_ref, ln2b_ref,
                out_ref, perm_ref, *, tm, n_sub):
    ts = tm // n_sub
    ns = ts // NSLAB
    n_lt = D_MODEL // DH
    starts = [sum(FFN_BLOCKS[:k]) for k in range(len(FFN_BLOCKS))]
    subs = [slice(t * ts, (t + 1) * ts) for t in range(n_sub)]

    def mixer_in(rows):
        h = hf_ref[rows, :] + hb_ref[rows, :]
        gated = []
        for g in range(HEADS):
            lanes = slice(g * DH, (g + 1) * DH)
            gated.append((_norm_lanes(h[:, lanes]) * ng_ref[:, lanes] * _sigmoid(o_ref[rows, lanes])).astype(BF16))
        return jnp.concatenate([yan_ref[rows, :]] + gated, axis=1)

    def permuted_x1(t, mix):
        x1n = _layer_norm(ALPHA * x_ref[subs[t], :] + mod_ref[2:3, :] * mix, ln1g_ref, ln1b_ref)
        for c in range(n_lt):
            perm_ref[t * n_lt + c] = x1n[:, c * DH:(c + 1) * DH]
        return jnp.concatenate(
            [jnp.concatenate([perm_ref[t * n_lt + c, pl.ds(j, ns, stride=NSLAB), :] for c in range(n_lt)], axis=1)
             for j in range(NSLAB)], axis=0)

    def up(h2, k):
        return _dot(h2, wup_ref[:, 2 * starts[k]:2 * (starts[k] + FFN_BLOCKS[k])])

    def hidden(a, k):
        blk = FFN_BLOCKS[k]
        cols = slice(2 * starts[k], 2 * (starts[k] + blk))
        q8 = lax.broadcasted_iota(jnp.int32, (ns, 2 * blk), 0) % NSLAB
        before_first = jnp.where(q8 == 0, 0.0, pltpu.roll(a[ts - ns:], 1, 0))
        after_last = jnp.where(q8 == NSLAB - 1, 0.0, pltpu.roll(a[:ns], ns - 1, 0))
        left = jnp.concatenate([before_first, a[:ts - ns]], axis=0)
        right = jnp.concatenate([a[ns:], after_last], axis=0)
        a = left * cw_ref[0:1, cols] + a * cw_ref[1:2, cols] + right * cw_ref[2:3, cols] + cb_ref[:, cols]
        prod = a[:, blk:] * a[:, :blk]
        return (prod + prod * jnp.tanh(a[:, blk:])).astype(BF16)

    def finish(t, x1, acc):
        out = _layer_norm(ALPHA * x1 + mod_ref[5:6, :] * acc, ln2g_ref, ln2b_ref)
        for c in range(n_lt):
            for j in range(NSLAB):
                perm_ref[t * n_lt + c, pl.ds(j, ns, stride=NSLAB), :] = out[j * ns:(j + 1) * ns, c * DH:(c + 1) * DH]
        for c in range(n_lt):
            out_ref[subs[t], c * DH:(c + 1) * DH] = perm_ref[t * n_lt + c]

    mixes = [_dot(mixer_in(rows), wout_ref[...]) for rows in subs]
    x1s = [permuted_x1(t, mixes[t]) for t in range(n_sub)]
    h2s = [(x1 * mod_ref[4:5, :] + mod_ref[3:4, :]).astype(BF16) for x1 in x1s]

    steps = [(t, k) for t in range(n_sub) for k in range(len(FFN_BLOCKS))]
    accs = [None] * n_sub
    queue = [up(h2s[t], k) for (t, k) in steps[:UP_AHEAD]]
    for idx, (t, k) in enumerate(steps):
        if idx + UP_AHEAD < len(steps):
            queue.append(up(h2s[steps[idx + UP_AHEAD][0]], steps[idx + UP_AHEAD][1]))
        part = _dot(hidden(queue.pop(0), k), wdown_ref[starts[k]:starts[k] + FFN_BLOCKS[k], :])
        accs[t] = part if accs[t] is None else accs[t] + part
        if k == len(FFN_BLOCKS) - 1:
            finish(t, x1s[t], accs[t])


def _ffn(x, yan, hf, hb, o_pre, mod6, wts, *, tm, n_sub):
    B, S, _ = x.shape
    kern = functools.partial(_ffn_kernel, tm=tm, n_sub=n_sub)
    tok = lambda w: pl.BlockSpec((None, tm, w), lambda b, i: (b, i, 0))
    in_specs = [tok(D_MODEL), tok(HALF_W), tok(HALF_W), tok(HALF_W), tok(HALF_W),
                pl.BlockSpec((None, 6, D_MODEL), lambda b, i: (b, 0, 0))] + [_const_spec(w.shape) for w in wts]
    return pl.pallas_call(
        kern, grid=(B, S // tm), in_specs=in_specs, out_specs=tok(D_MODEL),
        out_shape=jax.ShapeDtypeStruct((B, S, D_MODEL), F32),
        scratch_shapes=[pltpu.VMEM((n_sub * (D_MODEL // DH), tm // n_sub, DH), F32)],
        compiler_params=pltpu.CompilerParams(
            dimension_semantics=("arbitrary", "arbitrary"), vmem_limit_bytes=VMEM_LIMIT),
        name="ffn",
    )(x, yan, hf, hb, o_pre, mod6, *wts)


def _interleave_blocks(val, gate, axis):
    parts, start = [], 0
    for blk in FFN_BLOCKS:
        parts.append(lax.slice_in_dim(val, start, start + blk, axis=axis))
        parts.append(lax.slice_in_dim(gate, start, start + blk, axis=axis))
        start += blk
    return jnp.concatenate(parts, axis=axis)


def kernel(x, c, ctx, c_ctx, w_ada, b_ada, w_in, gmlp_ln_g, gmlp_ws, gmlp_bs, qk_conv_w, qk_conv_b,
           b_igate, b_fgate, mix_norm_g, w_out, ln1_g, ln1_b, w_up, ffn_conv_w, ffn_conv_b, w_down,
           ln2_g, ln2_b):
    B, S, _ = x.shape
    assert DEPTH == 1 and w_in.shape[0] == 1
    l = 0

    c_rows = jnp.concatenate([c, c_ctx[None, :], jnp.zeros((8 - B - 1, D_MODEL), F32)], axis=0)
    chunk = jnp.arange(6 * D_MODEL) // D_MODEL
    bias = b_ada[l] + jnp.where((chunk == 1) | (chunk == 4), 1.0, 0.0)
    mod, w_in_b, w_g = _modulation(c_rows, w_ada[l], bias[None, :], jnp.swapaxes(w_in[l], 0, 1))
    mod_x = mod[:B].reshape(B, 6, D_MODEL)
    mod_c = jnp.broadcast_to(mod[B].reshape(1, 6, D_MODEL), (B, 6, D_MODEL))

    gbi = jnp.broadcast_to(b_igate[l].reshape(N_HD, 1), (N_HD, CHUNK))
    gbf = jnp.broadcast_to(b_fgate[l].reshape(N_HD, 1), (N_HD, CHUNK))
    ln_g = gmlp_ln_g[l].reshape(1, HALF_W)
    ws = gmlp_ws[l].astype(BF16)
    bsb = jnp.repeat(gmlp_bs[l].T, DH, axis=1)
    norm_g = mix_norm_g[l][None, :]
    in_wts = (w_in_b, w_g, ln_g, ws, bsb, qk_conv_w[l], qk_conv_b[l][None, :], gbi, gbf,
              norm_g[:, :HALF_W])

    _, q_c, kt_c, v_c, _, *gates_c = _inproj(ctx, mod_c, in_wts, tm=ctx.shape[1], mixer=False)
    _, _, c_state, m_state = _scan(q_c, kt_c, v_c, *gates_c)

    yan, q, kt, v, o_pre, *rest = _inproj(x, mod_x, in_wts, tm=LATENT_TILE,
                                          ffn_f32=(w_up[l], w_down[l], w_out[l]))
    gates, (w_up_b, w_down_b, w_out_b) = rest[:4], rest[4:]
    hf, hb, _, _ = _scan(q, kt, v, *gates, init=(c_state, m_state))

    cw = _interleave_blocks(ffn_conv_w[l][:, :D_FF], 0.5 * ffn_conv_w[l][:, D_FF:], 1)
    cb = _interleave_blocks(ffn_conv_b[l][None, :D_FF], 0.5 * ffn_conv_b[l][None, D_FF:], 1)
    ffn_wts = (norm_g[:, HALF_W:], w_out_b, ln1_g[l][None, :], ln1_b[l][None, :],
               w_up_b, cw, cb, w_down_b, ln2_g[l][None, :], ln2_b[l][None, :])
    return _ffn(x, yan, hf, hb, o_pre, mod_x, ffn_wts, tm=LATENT_TILE, n_sub=FFN_SUB)
```

```python
import functools

import jax
import jax.numpy as jnp
from jax import lax
from jax.experimental import pallas as pl
from jax.experimental.pallas import tpu as pltpu

D_MODEL = 1024
GRID_W = 64
CHUNK = 128
HEADS = 4
DH = 128
HALF_W = HEADS * DH
N_HD = 2 * HEADS
D_FF = 2688
DEPTH = 1
ALPHA = (2 * DEPTH) ** 0.25
EPS = 1e-5
NSLAB = 8
HALO = 16
FFN_BLOCKS = (768, 768, 768, 384)
UP_AHEAD = 2
LATENT_TILE = 512
FFN_SUB = 2
SCAN_CHUNKS = 4
MOD_STEPS = 8
BF16_ROWS = 16

F32 = jnp.float32
BF16 = jnp.bfloat16
V7X_VMEM_BYTES = 64 * 1024 * 1024
VMEM_LIMIT = V7X_VMEM_BYTES - 8 * 1024 * 1024


def _dot(a, b):
    return jnp.dot(a, b, preferred_element_type=F32)


def _norm_lanes(z):
    mu = jnp.mean(z, axis=-1, keepdims=True)
    d = z - mu
    var = jnp.mean(d * d, axis=-1, keepdims=True)
    return d * lax.rsqrt(var + EPS)


def _gelu_tanh(x):
    half = 0.5 * x
    return half + half * jnp.tanh(x * (0.7978845608028654 + 0.035677408136300125 * (x * x)))


def _sigmoid(x):
    return 0.5 * jnp.tanh(0.5 * x) + 0.5


def _log_sigmoid(x):
    return jnp.minimum(x, 0.0) - jnp.log(1.0 + jnp.exp(-jnp.abs(x)))


def _const_spec(shape):
    nd = len(shape)
    return pl.BlockSpec(shape, lambda *_: (0,) * nd, pipeline_mode=pl.Buffered(1))


def _mod_kernel(c_ref, w_ref, b_ref, win_ref, wg_ref, o_ref, winb_ref, wgb_ref):
    winb_ref[...] = win_ref[...].T.astype(BF16)
    wg = wg_ref[...]
    wg_rows = jnp.concatenate([wg, jnp.zeros((DH - wg.shape[0], wg.shape[1]), F32)], axis=0)
    wgb_ref[...] = wg_rows.T.astype(BF16)

    cs = c_ref[...]
    a = cs * _sigmoid(cs)
    a_hi = a.astype(BF16)
    a_lo = (a - a_hi.astype(F32)).astype(BF16)
    w = w_ref[...]
    w_hi = w.astype(BF16)
    w_lo = (w - w_hi.astype(F32)).astype(BF16)
    by_hi = _dot(jnp.concatenate([a_hi, a_lo], axis=0), w_hi)
    rows = a.shape[0]
    o_ref[...] = by_hi[:rows] + by_hi[rows:] + _dot(a_hi, w_lo) + b_ref[...]


def _modulation(c_rows, w_ada, b_ada, w_in_t):
    n_out = w_ada.shape[1]
    n_steps = MOD_STEPS
    bn = n_out // n_steps
    n_main = 6 * HALF_W
    n_gate = 2 * N_HD
    rows = n_main // n_steps
    assert w_in_t.shape == (n_main + n_gate, D_MODEL) and n_main % n_gate == 0
    return pl.pallas_call(
        _mod_kernel,
        grid=(n_steps,),
        in_specs=[pl.BlockSpec((8, D_MODEL), lambda j: (0, 0)),
                  pl.BlockSpec((D_MODEL, bn), lambda j: (0, j)),
                  pl.BlockSpec((1, bn), lambda j: (0, j)),
                  pl.BlockSpec((rows, D_MODEL), lambda j: (j, 0)),
                  pl.BlockSpec((n_gate, D_MODEL), lambda j: (n_main // n_gate, 0))],
        out_specs=[pl.BlockSpec((8, bn), lambda j: (0, j)),
                   pl.BlockSpec((D_MODEL, rows), lambda j: (0, j)),
                   pl.BlockSpec((D_MODEL, DH), lambda j: (0, 0))],
        out_shape=[jax.ShapeDtypeStruct((8, n_out), F32),
                   jax.ShapeDtypeStruct((D_MODEL, n_main), BF16),
                   jax.ShapeDtypeStruct((D_MODEL, DH), BF16)],
        compiler_params=pltpu.CompilerParams(vmem_limit_bytes=VMEM_LIMIT),
        name="mod",
    )(c_rows, w_ada, b_ada, w_in_t, w_in_t)


def _inproj_kernel(x_ref, xp_ref, xn_ref, mod_ref, win_ref, wg_ref,
                   lng_ref, ws_ref, bsb_ref, cw_ref, cb_ref, gbi_ref, gbf_ref, ng_ref,
                   *rest, tm, convert, mixer):
    if convert:
        wup_ref, wdown_ref, wout_ref = rest[:3]
        wupb_ref, wdownb_ref, woutb_ref = rest[-3:]
        rest = rest[3:-3]
        start = 0
        for blk in FFN_BLOCKS:
            wupb_ref[:, 2 * start:2 * start + blk] = wup_ref[:, start:start + blk].astype(BF16)
            wupb_ref[:, 2 * start + blk:2 * (start + blk)] = wup_ref[:, D_FF + start:D_FF + start + blk].astype(BF16)
            start += blk
        wdownb_ref[...] = wdown_ref[...].astype(BF16)
        woutb_ref[...] = wout_ref[...].astype(BF16)
    yan_ref, q_ref, kt_ref, vm_ref, o_ref, gb_ref, gcm_ref, gct_ref, gwk_ref = rest
    i = pl.program_id(1)
    n_tiles = pl.num_programs(1)
    shift = mod_ref[0:1, :]
    scale = mod_ref[1:2, :]

    def modulate(xv):
        return (xv * scale + shift).astype(BF16)

    hx = modulate(x_ref[...])
    h_prev = jnp.where(i > 0, modulate(xp_ref[...]), jnp.zeros((), BF16))
    h_next = jnp.where(i < n_tiles - 1, modulate(xn_ref[...]), jnp.zeros((), BF16))
    uv = _dot(hx, win_ref[:, 0:2 * HALF_W]) if mixer else None
    gp = _dot(hx, wg_ref[...])
    ext = _dot(jnp.concatenate([h_prev, hx, h_next], axis=0), win_ref[:, 2 * HALF_W:4 * HALF_W])
    vo = _dot(hx, win_ref[:, 4 * HALF_W:(6 if mixer else 5) * HALF_W])

    if mixer:
        u = _gelu_tanh(uv[:, :HALF_W])
        v = _gelu_tanh(uv[:, HALF_W:])
    else:
        yan_ref[...] = jnp.zeros(yan_ref.shape, yan_ref.dtype)
        o_ref[...] = jnp.zeros(o_ref.shape, o_ref.dtype)
    for g in range(HEADS if mixer else 0):
        lanes = slice(g * DH, (g + 1) * DH)
        vh = (_norm_lanes(v[:, lanes]) * lng_ref[:, lanes]).astype(BF16)
        for c in range(tm // CHUNK):
            rows = slice(c * CHUNK, (c + 1) * CHUNK)
            mixed = _dot(ws_ref[g], vh[rows]) + bsb_ref[:, lanes]
            ya = u[rows, lanes] * mixed
            yan_ref[rows, lanes] = (_norm_lanes(ya) * ng_ref[:, lanes]).astype(BF16)

    n_ext = tm + 2 * HALO
    pre = ext[HALO:HALO + tm]
    down = pltpu.roll(ext, 1, 0)[HALO:HALO + tm]
    up = pltpu.roll(ext, n_ext - 1, 0)[HALO:HALO + tm]
    conv = down * cw_ref[0:1, :] + pre * cw_ref[1:2, :] + up * cw_ref[2:3, :] + cb_ref[...]
    qk = conv * _sigmoid(conv)
    q_ref[...] = qk[:, :HALF_W].astype(BF16)
    kt_ref[...] = (qk[:, HALF_W:] * (DH ** -0.5)).T.astype(BF16)

    vm_ref[...] = vo[:, :HALF_W].astype(BF16)
    if mixer:
        o_ref[...] = vo[:, HALF_W:]

    pos = lax.broadcasted_iota(jnp.int32, (N_HD, CHUNK), 1)
    is_fwd = lax.broadcasted_iota(jnp.int32, (N_HD, CHUNK), 0) < HEADS
    pad = jnp.zeros((DH - N_HD, CHUNK), F32)

    def scan_lanes(z, op, fill, reverse):
        step = 1
        while step < CHUNK:
            if reverse:
                moved = jnp.where(pos < CHUNK - step, pltpu.roll(z, CHUNK - step, 1), fill)
            else:
                moved = jnp.where(pos >= step, pltpu.roll(z, step, 1), fill)
            z = op(z, moved)
            step *= 2
        return z

    for c in range(tm // CHUNK):
        rows = slice(c * CHUNK, (c + 1) * CHUNK)
        gt = gp[rows].T
        log_i = gt[0:N_HD] + gbi_ref[...]
        lf = _log_sigmoid(gt[N_HD:2 * N_HD] + gbf_ref[...])
        csum = scan_lanes(lf, jnp.add, 0.0, False)
        suffix = csum[:, CHUNK - 1:CHUNK] - csum + lf
        b = jnp.where(is_fwd, csum, suffix)
        cdiff = b - log_i
        cmin = jnp.where(is_fwd, scan_lanes(cdiff, jnp.minimum, jnp.inf, False),
                         scan_lanes(cdiff, jnp.minimum, jnp.inf, True))
        cmin_end = jnp.where(is_fwd[:, 0:1], cmin[:, CHUNK - 1:CHUNK], cmin[:, 0:1])
        gct_ref[:, rows] = cdiff
        gwk_ref[:, rows] = jnp.exp(cmin_end - cdiff)
        gb_ref[rows, :] = jnp.concatenate([b, pad], axis=0).T
        gcm_ref[rows, :] = jnp.concatenate([cmin, pad], axis=0).T


def _inproj(xs, mod6, wts, *, tm, ffn_f32=(), mixer=True):
    B, S, _ = xs.shape
    nt = S // tm
    hb = tm // HALO
    n_halo_blocks = S // HALO
    kern = functools.partial(_inproj_kernel, tm=tm, convert=bool(ffn_f32), mixer=mixer)
    tok = lambda w: pl.BlockSpec((None, tm, w), lambda b, i: (b, i, 0))
    in_specs = [
        tok(D_MODEL),
        pl.BlockSpec((None, HALO, D_MODEL), lambda b, i: (b, jnp.maximum(i * hb - 1, 0), 0)),
        pl.BlockSpec((None, HALO, D_MODEL), lambda b, i: (b, jnp.minimum((i + 1) * hb, n_halo_blocks - 1), 0)),
        pl.BlockSpec((None, 6, D_MODEL), lambda b, i: (b, 0, 0)),
    ] + [_const_spec(w.shape) for w in wts]
    out_shape = [
        jax.ShapeDtypeStruct((B, S, HALF_W), BF16),
        jax.ShapeDtypeStruct((B, S, HALF_W), BF16),
        jax.ShapeDtypeStruct((B, nt, HALF_W, tm), BF16),
        jax.ShapeDtypeStruct((B, S, HALF_W), BF16),
        jax.ShapeDtypeStruct((B, S, HALF_W), F32),
        jax.ShapeDtypeStruct((B, S, DH), F32),
        jax.ShapeDtypeStruct((B, S, DH), F32),
        jax.ShapeDtypeStruct((B, nt, N_HD, tm), F32),
        jax.ShapeDtypeStruct((B, nt, N_HD, tm), F32),
    ]
    out_specs = [
        tok(HALF_W), tok(HALF_W),
        pl.BlockSpec((None, None, HALF_W, tm), lambda b, i: (b, i, 0, 0)),
        tok(HALF_W), tok(HALF_W), tok(DH), tok(DH),
        pl.BlockSpec((None, None, N_HD, tm), lambda b, i: (b, i, 0, 0)),
        pl.BlockSpec((None, None, N_HD, tm), lambda b, i: (b, i, 0, 0)),
    ]
    if ffn_f32:
        for w in ffn_f32:
            n_blocks = max(n for n in range(1, B * nt + 1)
                           if w.shape[0] % n == 0 and (w.shape[0] // n) % BF16_ROWS == 0)
            rows = w.shape[0] // n_blocks
            spec = pl.BlockSpec((rows, w.shape[1]), lambda b, i, n=n_blocks: (jnp.minimum(b * nt + i, n - 1), 0))
            in_specs.append(spec)
            out_specs.append(spec)
            out_shape.append(jax.ShapeDtypeStruct(w.shape, BF16))
    return pl.pallas_call(
        kern, grid=(B, nt), in_specs=in_specs, out_specs=out_specs, out_shape=out_shape,
        compiler_params=pltpu.CompilerParams(
            dimension_semantics=("arbitrary", "arbitrary"), vmem_limit_bytes=VMEM_LIMIT),
        name="inproj",
    )(xs, xs, xs, mod6, *wts, *ffn_f32)


def _scan_kernel(qf_ref, ktf_ref, vf_ref, gbf_ref, gcmf_ref, gctf_ref, gwkf_ref,
                 qb_ref, ktb_ref, vb_ref, gbb_ref, gcmb_ref, gctb_ref, gwkb_ref,
                 *rest, zero_init, cps):
    if zero_init:
        hf_ref, hb_ref, cst_ref, mst_ref = rest
    else:
        c0_ref, m0_ref, hf_ref, hb_ref, cst_ref, mst_ref = rest

    @pl.when(pl.program_id(1) == 0)
    def _():
        if zero_init:
            cst_ref[...] = jnp.zeros(cst_ref.shape, F32)
            mst_ref[...] = jnp.zeros(mst_ref.shape, F32)
        else:
            cst_ref[...] = c0_ref[...]
            mst_ref[...] = m0_ref[...]

    t_idx = lax.broadcasted_iota(jnp.int32, (CHUNK, CHUNK), 0)
    s_idx = lax.broadcasted_iota(jnp.int32, (CHUNK, CHUNK), 1)
    is_fwd = lax.broadcasted_iota(jnp.int32, (CHUNK, DH), 1) < HEADS
    ones = jnp.ones((CHUNK, DH), BF16)
    f_rows = [slice(k * CHUNK, (k + 1) * CHUNK) for k in range(cps)]
    b_rows = f_rows[::-1]

    terms = []
    m_row = mst_ref[0:1, :]
    for fr, br in zip(f_rows, b_rows):
        b = jnp.where(is_fwd, gbf_ref[fr, :], gbb_ref[br, :])
        cmin = jnp.where(is_fwd, gcmf_ref[fr, :], gcmb_ref[br, :])
        total = jnp.where(is_fwd[0:1], gbf_ref[fr.stop - 1:fr.stop, :], gbb_ref[br.start:br.start + 1, :])
        cmin_end = jnp.where(is_fwd[0:1], gcmf_ref[fr.stop - 1:fr.stop, :], gcmb_ref[br.start:br.start + 1, :])
        rmax = b - cmin
        inter = b + m_row
        m_t = jnp.maximum(inter, rmax)
        gmax = total - cmin_end
        m_new = jnp.maximum(total + m_row, gmax)
        terms.append(dict(lead=cmin + (rmax - m_t), w_state=jnp.exp(inter - m_t),
                          floor=jnp.exp(-m_t), w_old=jnp.exp(total + m_row - m_new), w_new=jnp.exp(gmax - m_new)))
        m_row = m_new
    mst_ref[...] = jnp.broadcast_to(m_row, (N_HD, DH))

    pairs = []
    for j in range(N_HD):
        fwd = j < HEADS
        lanes = slice((j % HEADS) * DH, (j % HEADS + 1) * DH)
        refs = ((qf_ref, ktf_ref, vf_ref, gctf_ref, gwkf_ref, hf_ref) if fwd else
                (qb_ref, ktb_ref, vb_ref, gctb_ref, gwkb_ref, hb_ref))
        pairs.append((fwd, lanes, f_rows if fwd else b_rows) + refs)

    scores, updates, v_augs = [], [], []
    for k in range(cps):
        scores.append([_dot(q_ref[rows[k], lanes], kt_ref[lanes, rows[k]])
                       for (fwd, lanes, rows, q_ref, kt_ref, v_ref, gct_ref, gwk_ref, out_ref) in pairs])
    for k in range(cps):
        v_augs.append([jnp.concatenate([v_ref[rows[k], lanes], ones], axis=1)
                       for (fwd, lanes, rows, q_ref, kt_ref, v_ref, gct_ref, gwk_ref, out_ref) in pairs])
        updates.append([_dot((kt_ref[lanes, rows[k]].astype(F32) * gwk_ref[j:j + 1, rows[k]]).astype(BF16), v_augs[k][j])
                        for j, (fwd, lanes, rows, q_ref, kt_ref, v_ref, gct_ref, gwk_ref, out_ref) in enumerate(pairs)])
    states = [cst_ref[j] for j in range(N_HD)]
    for k in range(cps):
        tk = terms[k]
        from_state = [_dot(q_ref[rows[k], lanes], states[j].astype(BF16))
                      for j, (fwd, lanes, rows, q_ref, kt_ref, v_ref, gct_ref, gwk_ref, out_ref) in enumerate(pairs)]
        states = [tk["w_old"][:, j:j + 1] * states[j] + tk["w_new"][:, j:j + 1] * updates[k][j] for j in range(N_HD)]
        for j, (fwd, lanes, rows, q_ref, kt_ref, v_ref, gct_ref, gwk_ref, out_ref) in enumerate(pairs):
            keep = (s_idx <= t_idx) if fwd else (s_idx >= t_idx)
            decay = jnp.exp(jnp.where(keep, tk["lead"][:, j:j + 1] - gct_ref[j:j + 1, rows[k]], -jnp.inf))
            sv = _dot((scores[k][j] * decay).astype(BF16), v_augs[k][j])
            ws = jnp.broadcast_to(tk["w_state"][:, j:j + 1], (CHUNK, DH))
            num = sv[:, :DH] + ws * from_state[j][:, :DH]
            den = sv[:, DH:] + ws * from_state[j][:, DH:]
            out_ref[rows[k], lanes] = num / jnp.maximum(jnp.abs(den), tk["floor"][:, j:j + 1])
    for j in range(N_HD):
        cst_ref[j] = states[j]


def _scan(q, kt, v, gb, gcm, gct, gwk, init=None):
    B, S, _ = q.shape
    cps = min(SCAN_CHUNKS, S // CHUNK)
    rows = cps * CHUNK
    nb = S // rows
    fwd_tok = lambda w: pl.BlockSpec((None, rows, w), lambda b, i: (b, i, 0))
    bwd_tok = lambda w: pl.BlockSpec((None, rows, w), lambda b, i: (b, nb - 1 - i, 0))
    assert kt.shape[1:] == (nb, HALF_W, rows)
    fwd_t = lambda r: pl.BlockSpec((None, None, r, rows), lambda b, i: (b, i, 0, 0))
    bwd_t = lambda r: pl.BlockSpec((None, None, r, rows), lambda b, i: (b, nb - 1 - i, 0, 0))
    c_spec = pl.BlockSpec((None, N_HD, DH, 2 * DH), lambda b, i: (b, 0, 0, 0))
    m_spec = pl.BlockSpec((None, N_HD, DH), lambda b, i: (b, 0, 0))
    init = () if init is None else tuple(init)
    return pl.pallas_call(
        functools.partial(_scan_kernel, zero_init=not init, cps=cps), grid=(B, nb),
        in_specs=[fwd_tok(HALF_W), fwd_t(HALF_W), fwd_tok(HALF_W), fwd_tok(DH), fwd_tok(DH), fwd_t(N_HD), fwd_t(N_HD),
                  bwd_tok(HALF_W), bwd_t(HALF_W), bwd_tok(HALF_W), bwd_tok(DH), bwd_tok(DH), bwd_t(N_HD), bwd_t(N_HD),
                  ] + [c_spec, m_spec][:len(init)],
        out_specs=[fwd_tok(HALF_W), bwd_tok(HALF_W), c_spec, m_spec],
        out_shape=[jax.ShapeDtypeStruct((B, S, HALF_W), F32),
                   jax.ShapeDtypeStruct((B, S, HALF_W), F32),
                   jax.ShapeDtypeStruct((B, N_HD, DH, 2 * DH), F32),
                   jax.ShapeDtypeStruct((B, N_HD, DH), F32)],
        compiler_params=pltpu.CompilerParams(
            dimension_semantics=("arbitrary", "arbitrary"), vmem_limit_bytes=VMEM_LIMIT),
        name="scan",
    )(q, kt, v, gb, gcm, gct, gwk, q, kt, v, gb, gcm, gct, gwk, *init)


def _layer_norm(z, g_ref, b_ref):
    return _norm_lanes(z) * g_ref[...] + b_ref[...]


def _ffn_kernel(x_ref, yan_ref, hf_ref, hb_ref, o_ref, mod_ref, ng_ref, wout_ref,
                ln1g_ref, ln1b_ref, wup_ref, cw_ref, cb_ref, wdown_ref, ln2g_ref, ln2b_ref,
                out_ref, perm_ref, *, tm, n_sub):
    ts = tm // n_sub
    ns = ts // NSLAB
    n_lt = D_MODEL // DH
    starts = [sum(FFN_BLOCKS[:k]) for k in range(len(FFN_BLOCKS))]
    subs = [slice(t * ts, (t + 1) * ts) for t in range(n_sub)]

    def mixer_in(rows):
        h = hf_ref[rows, :] + hb_ref[rows, :]
        gated = []
        for g in range(HEADS):
            lanes = slice(g * DH, (g + 1) * DH)
            gated.append((_norm_lanes(h[:, lanes]) * ng_ref[:, lanes] * _sigmoid(o_ref[rows, lanes])).astype(BF16))
        return jnp.concatenate([yan_ref[rows, :]] + gated, axis=1)

    def permuted_x1(t, mix):
        x1n = _layer_norm(ALPHA * x_ref[subs[t], :] + mod_ref[2:3, :] * mix, ln1g_ref, ln1b_ref)
        for c in range(n_lt):
            perm_ref[t * n_lt + c] = x1n[:, c * DH:(c + 1) * DH]
        return jnp.concatenate(
            [jnp.concatenate([perm_ref[t * n_lt + c, pl.ds(j, ns, stride=NSLAB), :] for c in range(n_lt)], axis=1)
             for j in range(NSLAB)], axis=0)

    def up(h2, k):
        return _dot(h2, wup_ref[:, 2 * starts[k]:2 * (starts[k] + FFN_BLOCKS[k])])

    def hidden(a, k):
        blk = FFN_BLOCKS[k]
        cols = slice(2 * starts[k], 2 * (starts[k] + blk))
        q8 = lax.broadcasted_iota(jnp.int32, (ns, 2 * blk), 0) % NSLAB
        before_first = jnp.where(q8 == 0, 0.0, pltpu.roll(a[ts - ns:], 1, 0))
        after_last = jnp.where(q8 == NSLAB - 1, 0.0, pltpu.roll(a[:ns], ns - 1, 0))
        left = jnp.concatenate([before_first, a[:ts - ns]], axis=0)
        right = jnp.concatenate([a[ns:], after_last], axis=0)
        a = left * cw_ref[0:1, cols] + a * cw_ref[1:2, cols] + right * cw_ref[2:3, cols] + cb_ref[:, cols]
        prod = a[:, blk:] * a[:, :blk]
        return (prod + prod * jnp.tanh(a[:, blk:])).astype(BF16)

    def finish(t, x1, acc):
        out = _layer_norm(ALPHA * x1 + mod_ref[5:6, :] * acc, ln2g_ref, ln2b_ref)
        for c in range(n_lt):
            for j in range(NSLAB):
                perm_ref[t * n_lt + c, pl.ds(j, ns, stride=NSLAB), :] = out[j * ns:(j + 1) * ns, c * DH:(c + 1) * DH]
        for c in range(n_lt):
            out_ref[subs[t], c * DH:(c + 1) * DH] = perm_ref[t * n_lt + c]

    mixes = [_dot(mixer_in(rows), wout_ref[...]) for rows in subs]
    x1s = [permuted_x1(t, mixes[t]) for t in range(n_sub)]
    h2s = [(x1 * mod_ref[4:5, :] + mod_ref[3:4, :]).astype(BF16) for x1 in x1s]

    steps = [(t, k) for t in range(n_sub) for k in range(len(FFN_BLOCKS))]
    accs = [None] * n_sub
    queue = [up(h2s[t], k) for (t, k) in steps[:UP_AHEAD]]
    for idx, (t, k) in enumerate(steps):
        if idx + UP_AHEAD < len(steps):
            queue.append(up(h2s[steps[idx + UP_AHEAD][0]], steps[idx + UP_AHEAD][1]))
        part = _dot(hidden(queue.pop(0), k), wdown_ref[starts[k]:starts[k] + FFN_BLOCKS[k], :])
        accs[t] = part if accs[t] is None else accs[t] + part
        if k == len(FFN_BLOCKS) - 1:
            finish(t, x1s[t], accs[t])


def _ffn(x, yan, hf, hb, o_pre, mod6, wts, *, tm, n_sub):
    B, S, _ = x.shape
    kern = functools.partial(_ffn_kernel, tm=tm, n_sub=n_sub)
    tok = lambda w: pl.BlockSpec((None, tm, w), lambda b, i: (b, i, 0))
    in_specs = [tok(D_MODEL), tok(HALF_W), tok(HALF_W), tok(HALF_W), tok(HALF_W),
                pl.BlockSpec((None, 6, D_MODEL), lambda b, i: (b, 0, 0))] + [_const_spec(w.shape) for w in wts]
    return pl.pallas_call(
        kern, grid=(B, S // tm), in_specs=in_specs, out_specs=tok(D_MODEL),
        out_shape=jax.ShapeDtypeStruct((B, S, D_MODEL), F32),
        scratch_shapes=[pltpu.VMEM((n_sub * (D_MODEL // DH), tm // n_sub, DH), F32)],
        compiler_params=pltpu.CompilerParams(
            dimension_semantics=("arbitrary", "arbitrary"), vmem_limit_bytes=VMEM_LIMIT),
        name="ffn",
    )(x, yan, hf, hb, o_pre, mod6, *wts)


def _interleave_blocks(val, gate, axis):
    parts, start = [], 0
    for blk in FFN_BLOCKS:
        parts.append(lax.slice_in_dim(val, start, start + blk, axis=axis))
        parts.append(lax.slice_in_dim(gate, start, start + blk, axis=axis))
        start += blk
    return jnp.concatenate(parts, axis=axis)


def kernel(x, c, ctx, c_ctx, w_ada, b_ada, w_in, gmlp_ln_g, gmlp_ws, gmlp_bs, qk_conv_w, qk_conv_b,
           b_igate, b_fgate, mix_norm_g, w_out, ln1_g, ln1_b, w_up, ffn_conv_w, ffn_conv_b, w_down,
           ln2_g, ln2_b):
    B, S, _ = x.shape
    assert DEPTH == 1 and w_in.shape[0] == 1
    l = 0

    c_rows = jnp.concatenate([c, c_ctx[None, :], jnp.zeros((8 - B - 1, D_MODEL), F32)], axis=0)
    chunk = jnp.arange(6 * D_MODEL) // D_MODEL
    bias = b_ada[l] + jnp.where((chunk == 1) | (chunk == 4), 1.0, 0.0)
    mod, w_in_b, w_g = _modulation(c_rows, w_ada[l], bias[None, :], jnp.swapaxes(w_in[l], 0, 1))
    mod_x = mod[:B].reshape(B, 6, D_MODEL)
    mod_c = jnp.broadcast_to(mod[B].reshape(1, 6, D_MODEL), (B, 6, D_MODEL))

    gbi = jnp.broadcast_to(b_igate[l].reshape(N_HD, 1), (N_HD, CHUNK))
    gbf = jnp.broadcast_to(b_fgate[l].reshape(N_HD, 1), (N_HD, CHUNK))
    ln_g = gmlp_ln_g[l].reshape(1, HALF_W)
    ws = gmlp_ws[l].astype(BF16)
    bsb = jnp.repeat(gmlp_bs[l].T, DH, axis=1)
    norm_g = mix_norm_g[l][None, :]
    in_wts = (w_in_b, w_g, ln_g, ws, bsb, qk_conv_w[l], qk_conv_b[l][None, :], gbi, gbf,
              norm_g[:, :HALF_W])

    _, q_c, kt_c, v_c, _, *gates_c = _inproj(ctx, mod_c, in_wts, tm=ctx.shape[1], mixer=False)
    _, _, c_state, m_state = _scan(q_c, kt_c, v_c, *gates_c)

    yan, q, kt, v, o_pre, *rest = _inproj(x, mod_x, in_wts, tm=LATENT_TILE,
                                          ffn_f32=(w_up[l], w_down[l], w_out[l]))
    gates, (w_up_b, w_down_b, w_out_b) = rest[:4], rest[4:]
    hf, hb, _, _ = _scan(q, kt, v, *gates, init=(c_state, m_state))

    cw = _interleave_blocks(ffn_conv_w[l][:, :D_FF], 0.5 * ffn_conv_w[l][:, D_FF:], 1)
    cb = _interleave_blocks(ffn_conv_b[l][None, :D_FF], 0.5 * ffn_conv_b[l][None, D_FF:], 1)
    ffn_wts = (norm_g[:, HALF_W:], w_out_b, ln1_g[l][None, :], ln1_b[l][None, :],
               w_up_b, cw, cb, w_down_b, ln2_g[l][None, :], ln2_b[l][None, :])
    return _ffn(x, yan, hf, hb, o_pre, mod_x, ffn_wts, tm=LATENT_TILE, n_sub=FFN_SUB)
```

```python
import functools

import jax
import jax.numpy as jnp
from jax import lax
from jax.experimental import pallas as pl
from jax.experimental.pallas import tpu as pltpu

D_MODEL = 1024
GRID_W = 64
CHUNK = 128
HEADS = 4
DH = 128
HALF_W = HEADS * DH
N_HD = 2 * HEADS
D_FF = 2688
DEPTH = 1
ALPHA = (2 * DEPTH) ** 0.25
EPS = 1e-5
NSLAB = 8
HALO = 16
FFN_BLOCKS = (768, 768, 768, 384)
UP_AHEAD = 2
HIDDEN_STRIP = 384
LATENT_TILE = 512
FFN_SUB = 2
SCAN_CHUNKS = 4
MOD_STEPS = 8
BF16_ROWS = 16

F32 = jnp.float32
BF16 = jnp.bfloat16
V7X_VMEM_BYTES = 64 * 1024 * 1024
VMEM_LIMIT = V7X_VMEM_BYTES - 8 * 1024 * 1024


def _dot(a, b):
    return jnp.dot(a, b, preferred_element_type=F32)


def _norm_lanes(z):
    mu = jnp.mean(z, axis=-1, keepdims=True)
    d = z - mu
    var = jnp.mean(d * d, axis=-1, keepdims=True)
    return d * lax.rsqrt(var + EPS)


def _gelu_tanh(x):
    half = 0.5 * x
    return half + half * jnp.tanh(x * (0.7978845608028654 + 0.035677408136300125 * (x * x)))


def _sigmoid(x):
    return 0.5 * jnp.tanh(0.5 * x) + 0.5


def _log_sigmoid(x):
    return jnp.minimum(x, 0.0) - jnp.log(1.0 + jnp.exp(-jnp.abs(x)))


def _const_spec(shape):
    nd = len(shape)
    return pl.BlockSpec(shape, lambda *_: (0,) * nd, pipeline_mode=pl.Buffered(1))


def _mod_kernel(c_ref, w_ref, b_ref, win_ref, wg_ref, o_ref, winb_ref, wgb_ref):
    winb_ref[...] = win_ref[...].T.astype(BF16)
    wg = wg_ref[...]
    wg_rows = jnp.concatenate([wg, jnp.zeros((DH - wg.shape[0], wg.shape[1]), F32)], axis=0)
    wgb_ref[...] = wg_rows.T.astype(BF16)

    cs = c_ref[...]
    a = cs * _sigmoid(cs)
    a_hi = a.astype(BF16)
    a_lo = (a - a_hi.astype(F32)).astype(BF16)
    w = w_ref[...]
    w_hi = w.astype(BF16)
    w_lo = (w - w_hi.astype(F32)).astype(BF16)
    by_hi = _dot(jnp.concatenate([a_hi, a_lo], axis=0), w_hi)
    rows = a.shape[0]
    o_ref[...] = by_hi[:rows] + by_hi[rows:] + _dot(a_hi, w_lo) + b_ref[...]


def _modulation(c_rows, w_ada, b_ada, w_in_t):
    n_out = w_ada.shape[1]
    n_steps = MOD_STEPS
    bn = n_out // n_steps
    n_main = 6 * HALF_W
    n_gate = 2 * N_HD
    rows = n_main // n_steps
    assert w_in_t.shape == (n_main + n_gate, D_MODEL) and n_main % n_gate == 0
    return pl.pallas_call(
        _mod_kernel,
        grid=(n_steps,),
        in_specs=[pl.BlockSpec((8, D_MODEL), lambda j: (0, 0)),
                  pl.BlockSpec((D_MODEL, bn), lambda j: (0, j)),
                  pl.BlockSpec((1, bn), lambda j: (0, j)),
                  pl.BlockSpec((rows, D_MODEL), lambda j: (j, 0)),
                  pl.BlockSpec((n_gate, D_MODEL), lambda j: (n_main // n_gate, 0))],
        out_specs=[pl.BlockSpec((8, bn), lambda j: (0, j)),
                   pl.BlockSpec((D_MODEL, rows), lambda j: (0, j)),
                   pl.BlockSpec((D_MODEL, DH), lambda j: (0, 0))],
        out_shape=[jax.ShapeDtypeStruct((8, n_out), F32),
                   jax.ShapeDtypeStruct((D_MODEL, n_main), BF16),
                   jax.ShapeDtypeStruct((D_MODEL, DH), BF16)],
        compiler_params=pltpu.CompilerParams(vmem_limit_bytes=VMEM_LIMIT),
        name="mod",
    )(c_rows, w_ada, b_ada, w_in_t, w_in_t)


def _inproj_kernel(x_ref, xp_ref, xn_ref, mod_ref, win_ref, wg_ref,
                   lng_ref, ws_ref, bsb_ref, cw_ref, cb_ref, gbi_ref, gbf_ref, ng_ref,
                   *rest, tm, convert, mixer):
    if convert:
        wup_ref, wdown_ref, wout_ref = rest[:3]
        wupb_ref, wdownb_ref, woutb_ref = rest[-3:]
        rest = rest[3:-3]
        start = 0
        for blk in FFN_BLOCKS:
            wupb_ref[:, 2 * start:2 * start + blk] = wup_ref[:, start:start + blk].astype(BF16)
            wupb_ref[:, 2 * start + blk:2 * (start + blk)] = wup_ref[:, D_FF + start:D_FF + start + blk].astype(BF16)
            start += blk
        wdownb_ref[...] = wdown_ref[...].astype(BF16)
        woutb_ref[...] = wout_ref[...].astype(BF16)
    yan_ref, q_ref, kt_ref, vm_ref, o_ref, gb_ref, gcm_ref, gct_ref, gwk_ref = rest
    i = pl.program_id(1)
    n_tiles = pl.num_programs(1)
    shift = mod_ref[0:1, :]
    scale = mod_ref[1:2, :]

    def modulate(xv):
        return (xv * scale + shift).astype(BF16)

    hx = modulate(x_ref[...])
    h_prev = jnp.where(i > 0, modulate(xp_ref[...]), jnp.zeros((), BF16))
    h_next = jnp.where(i < n_tiles - 1, modulate(xn_ref[...]), jnp.zeros((), BF16))
    uv = _dot(hx, win_ref[:, 0:2 * HALF_W]) if mixer else None
    gp = _dot(hx, wg_ref[...])
    ext = _dot(jnp.concatenate([h_prev, hx, h_next], axis=0), win_ref[:, 2 * HALF_W:4 * HALF_W])
    vo = _dot(hx, win_ref[:, 4 * HALF_W:(6 if mixer else 5) * HALF_W])

    if mixer:
        u = _gelu_tanh(uv[:, :HALF_W])
        v = _gelu_tanh(uv[:, HALF_W:])
    else:
        yan_ref[...] = jnp.zeros(yan_ref.shape, yan_ref.dtype)
        o_ref[...] = jnp.zeros(o_ref.shape, o_ref.dtype)
    for g in range(HEADS if mixer else 0):
        lanes = slice(g * DH, (g + 1) * DH)
        vh = (_norm_lanes(v[:, lanes]) * lng_ref[:, lanes]).astype(BF16)
        for c in range(tm // CHUNK):
            rows = slice(c * CHUNK, (c + 1) * CHUNK)
            mixed = _dot(ws_ref[g], vh[rows]) + bsb_ref[:, lanes]
            ya = u[rows, lanes] * mixed
            yan_ref[rows, lanes] = (_norm_lanes(ya) * ng_ref[:, lanes]).astype(BF16)

    n_ext = tm + 2 * HALO
    pre = ext[HALO:HALO + tm]
    down = pltpu.roll(ext, 1, 0)[HALO:HALO + tm]
    up = pltpu.roll(ext, n_ext - 1, 0)[HALO:HALO + tm]
    conv = down * cw_ref[0:1, :] + pre * cw_ref[1:2, :] + up * cw_ref[2:3, :] + cb_ref[...]
    qk = conv * _sigmoid(conv)
    q_ref[...] = qk[:, :HALF_W].astype(BF16)
    kt_ref[...] = (qk[:, HALF_W:] * (DH ** -0.5)).T.astype(BF16)

    vm_ref[...] = vo[:, :HALF_W].astype(BF16)
    if mixer:
        o_ref[...] = vo[:, HALF_W:]

    pos = lax.broadcasted_iota(jnp.int32, (N_HD, CHUNK), 1)
    is_fwd = lax.broadcasted_iota(jnp.int32, (N_HD, CHUNK), 0) < HEADS
    pad = jnp.zeros((DH - N_HD, CHUNK), F32)

    def scan_lanes(z, op, fill, reverse):
        step = 1
        while step < CHUNK:
            if reverse:
                moved = jnp.where(pos < CHUNK - step, pltpu.roll(z, CHUNK - step, 1), fill)
            else:
                moved = jnp.where(pos >= step, pltpu.roll(z, step, 1), fill)
            z = op(z, moved)
            step *= 2
        return z

    for c in range(tm // CHUNK):
        rows = slice(c * CHUNK, (c + 1) * CHUNK)
        gt = gp[rows].T
        log_i = gt[0:N_HD] + gbi_ref[...]
        lf = _log_sigmoid(gt[N_HD:2 * N_HD] + gbf_ref[...])
        csum = scan_lanes(lf, jnp.add, 0.0, False)
        suffix = csum[:, CHUNK - 1:CHUNK] - csum + lf
        b = jnp.where(is_fwd, csum, suffix)
        cdiff = b - log_i
        cmin = jnp.where(is_fwd, scan_lanes(cdiff, jnp.minimum, jnp.inf, False),
                         scan_lanes(cdiff, jnp.minimum, jnp.inf, True))
        cmin_end = jnp.where(is_fwd[:, 0:1], cmin[:, CHUNK - 1:CHUNK], cmin[:, 0:1])
        gct_ref[:, rows] = cdiff
        gwk_ref[:, rows] = jnp.exp(cmin_end - cdiff)
        gb_ref[rows, :] = jnp.concatenate([b, pad], axis=0).T
        gcm_ref[rows, :] = jnp.concatenate([cmin, pad], axis=0).T


def _inproj(xs, mod6, wts, *, tm, ffn_f32=(), mixer=True):
    B, S, _ = xs.shape
    nt = S // tm
    hb = tm // HALO
    n_halo_blocks = S // HALO
    kern = functools.partial(_inproj_kernel, tm=tm, convert=bool(ffn_f32), mixer=mixer)
    tok = lambda w: pl.BlockSpec((None, tm, w), lambda b, i: (b, i, 0))
    in_specs = [
        tok(D_MODEL),
        pl.BlockSpec((None, HALO, D_MODEL), lambda b, i: (b, jnp.maximum(i * hb - 1, 0), 0)),
        pl.BlockSpec((None, HALO, D_MODEL), lambda b, i: (b, jnp.minimum((i + 1) * hb, n_halo_blocks - 1), 0)),
        pl.BlockSpec((None, 6, D_MODEL), lambda b, i: (b, 0, 0)),
    ] + [_const_spec(w.shape) for w in wts]
    out_shape = [
        jax.ShapeDtypeStruct((B, S, HALF_W), BF16),
        jax.ShapeDtypeStruct((B, S, HALF_W), BF16),
        jax.ShapeDtypeStruct((B, nt, HALF_W, tm), BF16),
        jax.ShapeDtypeStruct((B, S, HALF_W), BF16),
        jax.ShapeDtypeStruct((B, S, HALF_W), F32),
        jax.ShapeDtypeStruct((B, S, DH), F32),
        jax.ShapeDtypeStruct((B, S, DH), F32),
        jax.ShapeDtypeStruct((B, nt, N_HD, tm), F32),
        jax.ShapeDtypeStruct((B, nt, N_HD, tm), F32),
    ]
    out_specs = [
        tok(HALF_W), tok(HALF_W),
        pl.BlockSpec((None, None, HALF_W, tm), lambda b, i: (b, i, 0, 0)),
        tok(HALF_W), tok(HALF_W), tok(DH), tok(DH),
        pl.BlockSpec((None, None, N_HD, tm), lambda b, i: (b, i, 0, 0)),
        pl.BlockSpec((None, None, N_HD, tm), lambda b, i: (b, i, 0, 0)),
    ]
    if ffn_f32:
        for w in ffn_f32:
            n_blocks = max(n for n in range(1, B * nt + 1)
                           if w.shape[0] % n == 0 and (w.shape[0] // n) % BF16_ROWS == 0)
            rows = w.shape[0] // n_blocks
            spec = pl.BlockSpec((rows, w.shape[1]), lambda b, i, n=n_blocks: (jnp.minimum(b * nt + i, n - 1), 0))
            in_specs.append(spec)
            out_specs.append(spec)
            out_shape.append(jax.ShapeDtypeStruct(w.shape, BF16))
    return pl.pallas_call(
        kern, grid=(B, nt), in_specs=in_specs, out_specs=out_specs, out_shape=out_shape,
        compiler_params=pltpu.CompilerParams(
            dimension_semantics=("arbitrary", "arbitrary"), vmem_limit_bytes=VMEM_LIMIT),
        name="inproj",
    )(xs, xs, xs, mod6, *wts, *ffn_f32)


def _scan_kernel(qf_ref, ktf_ref, vf_ref, gbf_ref, gcmf_ref, gctf_ref, gwkf_ref,
                 qb_ref, ktb_ref, vb_ref, gbb_ref, gcmb_ref, gctb_ref, gwkb_ref,
                 *rest, zero_init, cps):
    if zero_init:
        hf_ref, hb_ref, cst_ref, mst_ref = rest
    else:
        c0_ref, m0_ref, hf_ref, hb_ref, cst_ref, mst_ref = rest

    @pl.when(pl.program_id(1) == 0)
    def _():
        if zero_init:
            cst_ref[...] = jnp.zeros(cst_ref.shape, F32)
            mst_ref[...] = jnp.zeros(mst_ref.shape, F32)
        else:
            cst_ref[...] = c0_ref[...]
            mst_ref[...] = m0_ref[...]

    t_idx = lax.broadcasted_iota(jnp.int32, (CHUNK, CHUNK), 0)
    s_idx = lax.broadcasted_iota(jnp.int32, (CHUNK, CHUNK), 1)
    is_fwd = lax.broadcasted_iota(jnp.int32, (CHUNK, DH), 1) < HEADS
    ones = jnp.ones((CHUNK, DH), BF16)
    f_rows = [slice(k * CHUNK, (k + 1) * CHUNK) for k in range(cps)]
    b_rows = f_rows[::-1]

    terms = []
    m_row = mst_ref[0:1, :]
    for fr, br in zip(f_rows, b_rows):
        b = jnp.where(is_fwd, gbf_ref[fr, :], gbb_ref[br, :])
        cmin = jnp.where(is_fwd, gcmf_ref[fr, :], gcmb_ref[br, :])
        total = jnp.where(is_fwd[0:1], gbf_ref[fr.stop - 1:fr.stop, :], gbb_ref[br.start:br.start + 1, :])
        cmin_end = jnp.where(is_fwd[0:1], gcmf_ref[fr.stop - 1:fr.stop, :], gcmb_ref[br.start:br.start + 1, :])
        rmax = b - cmin
        inter = b + m_row
        m_t = jnp.maximum(inter, rmax)
        gmax = total - cmin_end
        m_new = jnp.maximum(total + m_row, gmax)
        terms.append(dict(lead=cmin + (rmax - m_t), w_state=jnp.exp(inter - m_t),
                          floor=jnp.exp(-m_t), w_old=jnp.exp(total + m_row - m_new), w_new=jnp.exp(gmax - m_new)))
        m_row = m_new
    mst_ref[...] = jnp.broadcast_to(m_row, (N_HD, DH))

    pairs = []
    for j in range(N_HD):
        fwd = j < HEADS
        lanes = slice((j % HEADS) * DH, (j % HEADS + 1) * DH)
        refs = ((qf_ref, ktf_ref, vf_ref, gctf_ref, gwkf_ref, hf_ref) if fwd else
                (qb_ref, ktb_ref, vb_ref, gctb_ref, gwkb_ref, hb_ref))
        pairs.append((fwd, lanes, f_rows if fwd else b_rows) + refs)

    scores, updates, v_augs = [], [], []
    for k in range(cps):
        scores.append([_dot(q_ref[rows[k], lanes], kt_ref[lanes, rows[k]])
                       for (fwd, lanes, rows, q_ref, kt_ref, v_ref, gct_ref, gwk_ref, out_ref) in pairs])
    for k in range(cps):
        v_augs.append([jnp.concatenate([v_ref[rows[k], lanes], ones], axis=1)
                       for (fwd, lanes, rows, q_ref, kt_ref, v_ref, gct_ref, gwk_ref, out_ref) in pairs])
        updates.append([_dot((kt_ref[lanes, rows[k]].astype(F32) * gwk_ref[j:j + 1, rows[k]]).astype(BF16), v_augs[k][j])
                        for j, (fwd, lanes, rows, q_ref, kt_ref, v_ref, gct_ref, gwk_ref, out_ref) in enumerate(pairs)])
    states = [cst_ref[j] for j in range(N_HD)]
    for k in range(cps):
        tk = terms[k]
        from_state = [_dot(q_ref[rows[k], lanes], states[j].astype(BF16))
                      for j, (fwd, lanes, rows, q_ref, kt_ref, v_ref, gct_ref, gwk_ref, out_ref) in enumerate(pairs)]
        states = [tk["w_old"][:, j:j + 1] * states[j] + tk["w_new"][:, j:j + 1] * updates[k][j] for j in range(N_HD)]
        for j, (fwd, lanes, rows, q_ref, kt_ref, v_ref, gct_ref, gwk_ref, out_ref) in enumerate(pairs):
            keep = (s_idx <= t_idx) if fwd else (s_idx >= t_idx)
            decay = jnp.exp(jnp.where(keep, tk["lead"][:, j:j + 1] - gct_ref[j:j + 1, rows[k]], -jnp.inf))
            sv = _dot((scores[k][j] * decay).astype(BF16), v_augs[k][j])
            ws = jnp.broadcast_to(tk["w_state"][:, j:j + 1], (CHUNK, DH))
            num = sv[:, :DH] + ws * from_state[j][:, :DH]
            den = sv[:, DH:] + ws * from_state[j][:, DH:]
            out_ref[rows[k], lanes] = num / jnp.maximum(jnp.abs(den), tk["floor"][:, j:j + 1])
    for j in range(N_HD):
        cst_ref[j] = states[j]


def _scan(q, kt, v, gb, gcm, gct, gwk, init=None):
    B, S, _ = q.shape
    cps = min(SCAN_CHUNKS, S // CHUNK)
    rows = cps * CHUNK
    nb = S // rows
    fwd_tok = lambda w: pl.BlockSpec((None, rows, w), lambda b, i: (b, i, 0))
    bwd_tok = lambda w: pl.BlockSpec((None, rows, w), lambda b, i: (b, nb - 1 - i, 0))
    assert kt.shape[1:] == (nb, HALF_W, rows)
    fwd_t = lambda r: pl.BlockSpec((None, None, r, rows), lambda b, i: (b, i, 0, 0))
    bwd_t = lambda r: pl.BlockSpec((None, None, r, rows), lambda b, i: (b, nb - 1 - i, 0, 0))
    c_spec = pl.BlockSpec((None, N_HD, DH, 2 * DH), lambda b, i: (b, 0, 0, 0))
    m_spec = pl.BlockSpec((None, N_HD, DH), lambda b, i: (b, 0, 0))
    init = () if init is None else tuple(init)
    return pl.pallas_call(
        functools.partial(_scan_kernel, zero_init=not init, cps=cps), grid=(B, nb),
        in_specs=[fwd_tok(HALF_W), fwd_t(HALF_W), fwd_tok(HALF_W), fwd_tok(DH), fwd_tok(DH), fwd_t(N_HD), fwd_t(N_HD),
                  bwd_tok(HALF_W), bwd_t(HALF_W), bwd_tok(HALF_W), bwd_tok(DH), bwd_tok(DH), bwd_t(N_HD), bwd_t(N_HD),
                  ] + [c_spec, m_spec][:len(init)],
        out_specs=[fwd_tok(HALF_W), bwd_tok(HALF_W), c_spec, m_spec],
        out_shape=[jax.ShapeDtypeStruct((B, S, HALF_W), F32),
                   jax.ShapeDtypeStruct((B, S, HALF_W), F32),
                   jax.ShapeDtypeStruct((B, N_HD, DH, 2 * DH), F32),
                   jax.ShapeDtypeStruct((B, N_HD, DH), F32)],
        compiler_params=pltpu.CompilerParams(
            dimension_semantics=("arbitrary", "arbitrary"), vmem_limit_bytes=VMEM_LIMIT),
        name="scan",
    )(q, kt, v, gb, gcm, gct, gwk, q, kt, v, gb, gcm, gct, gwk, *init)


def _layer_norm(z, g_ref, b_ref):
    return _norm_lanes(z) * g_ref[...] + b_ref[...]


def _ffn_kernel(x_ref, yan_ref, hf_ref, hb_ref, o_ref, mod_ref, ng_ref, wout_ref,
                ln1g_ref, ln1b_ref, wup_ref, cw_ref, cb_ref, wdown_ref, ln2g_ref, ln2b_ref,
                out_ref, perm_ref, *, tm, n_sub):
    ts = tm // n_sub
    ns = ts // NSLAB
    n_lt = D_MODEL // DH
    starts = [sum(FFN_BLOCKS[:k]) for k in range(len(FFN_BLOCKS))]
    subs = [slice(t * ts, (t + 1) * ts) for t in range(n_sub)]

    def mixer_in(rows):
        h = hf_ref[rows, :] + hb_ref[rows, :]
        gated = []
        for g in range(HEADS):
            lanes = slice(g * DH, (g + 1) * DH)
            gated.append((_norm_lanes(h[:, lanes]) * ng_ref[:, lanes] * _sigmoid(o_ref[rows, lanes])).astype(BF16))
        return jnp.concatenate([yan_ref[rows, :]] + gated, axis=1)

    def permuted_x1(t, mix):
        x1n = _layer_norm(ALPHA * x_ref[subs[t], :] + mod_ref[2:3, :] * mix, ln1g_ref, ln1b_ref)
        for c in range(n_lt):
            perm_ref[t * n_lt + c] = x1n[:, c * DH:(c + 1) * DH]
        return jnp.concatenate(
            [jnp.concatenate([perm_ref[t * n_lt + c, pl.ds(j, ns, stride=NSLAB), :] for c in range(n_lt)], axis=1)
             for j in range(NSLAB)], axis=0)

    def up(h2, k):
        return _dot(h2, wup_ref[:, 2 * starts[k]:2 * (starts[k] + FFN_BLOCKS[k])])

    def hidden(a, k):
        blk = FFN_BLOCKS[k]
        q8 = lax.broadcasted_iota(jnp.int32, (ns, HIDDEN_STRIP), 0) % NSLAB

        def conv(z, cols):
            before_first = jnp.where(q8 == 0, 0.0, pltpu.roll(z[ts - ns:], 1, 0))
            after_last = jnp.where(q8 == NSLAB - 1, 0.0, pltpu.roll(z[:ns], ns - 1, 0))
            left = jnp.concatenate([before_first, z[:ts - ns]], axis=0)
            right = jnp.concatenate([z[ns:], after_last], axis=0)
            return left * cw_ref[0:1, cols] + z * cw_ref[1:2, cols] + right * cw_ref[2:3, cols] + cb_ref[:, cols]

        acts = []
        for c0 in range(0, blk, HIDDEN_STRIP):
            base = 2 * starts[k] + c0
            val = conv(a[:, c0:c0 + HIDDEN_STRIP], slice(base, base + HIDDEN_STRIP))
            gate = conv(a[:, blk + c0:blk + c0 + HIDDEN_STRIP], slice(base + blk, base + blk + HIDDEN_STRIP))
            prod = gate * val
            acts.append((prod + prod * jnp.tanh(gate)).astype(BF16))
        return jnp.concatenate(acts, axis=1)

    def finish(t, x1, acc):
        out = _layer_norm(ALPHA * x1 + mod_ref[5:6, :] * acc, ln2g_ref, ln2b_ref)
        for c in range(n_lt):
            for j in range(NSLAB):
                perm_ref[t * n_lt + c, pl.ds(j, ns, stride=NSLAB), :] = out[j * ns:(j + 1) * ns, c * DH:(c + 1) * DH]
        for c in range(n_lt):
            out_ref[subs[t], c * DH:(c + 1) * DH] = perm_ref[t * n_lt + c]

    mixes = [_dot(mixer_in(rows), wout_ref[...]) for rows in subs]
    x1s = [permuted_x1(t, mixes[t]) for t in range(n_sub)]
    h2s = [(x1 * mod_ref[4:5, :] + mod_ref[3:4, :]).astype(BF16) for x1 in x1s]

    steps = [(t, k) for t in range(n_sub) for k in range(len(FFN_BLOCKS))]
    accs = [None] * n_sub
    queue = [up(h2s[t], k) for (t, k) in steps[:UP_AHEAD]]
    for idx, (t, k) in enumerate(steps):
        if idx + UP_AHEAD < len(steps):
            queue.append(up(h2s[steps[idx + UP_AHEAD][0]], steps[idx + UP_AHEAD][1]))
        part = _dot(hidden(queue.pop(0), k), wdown_ref[starts[k]:starts[k] + FFN_BLOCKS[k], :])
        accs[t] = part if accs[t] is None else accs[t] + part
        if k == len(FFN_BLOCKS) - 1:
            finish(t, x1s[t], accs[t])


def _ffn(x, yan, hf, hb, o_pre, mod6, wts, *, tm, n_sub):
    B, S, _ = x.shape
    kern = functools.partial(_ffn_kernel, tm=tm, n_sub=n_sub)
    tok = lambda w: pl.BlockSpec((None, tm, w), lambda b, i: (b, i, 0))
    in_specs = [tok(D_MODEL), tok(HALF_W), tok(HALF_W), tok(HALF_W), tok(HALF_W),
                pl.BlockSpec((None, 6, D_MODEL), lambda b, i: (b, 0, 0))] + [_const_spec(w.shape) for w in wts]
    return pl.pallas_call(
        kern, grid=(B, S // tm), in_specs=in_specs, out_specs=tok(D_MODEL),
        out_shape=jax.ShapeDtypeStruct((B, S, D_MODEL), F32),
        scratch_shapes=[pltpu.VMEM((n_sub * (D_MODEL // DH), tm // n_sub, DH), F32)],
        compiler_params=pltpu.CompilerParams(
            dimension_semantics=("arbitrary", "arbitrary"), vmem_limit_bytes=VMEM_LIMIT),
        name="ffn",
    )(x, yan, hf, hb, o_pre, mod6, *wts)


def _interleave_blocks(val, gate, axis):
    parts, start = [], 0
    for blk in FFN_BLOCKS:
        parts.append(lax.slice_in_dim(val, start, start + blk, axis=axis))
        parts.append(lax.slice_in_dim(gate, start, start + blk, axis=axis))
        start += blk
    return jnp.concatenate(parts, axis=axis)


def kernel(x, c, ctx, c_ctx, w_ada, b_ada, w_in, gmlp_ln_g, gmlp_ws, gmlp_bs, qk_conv_w, qk_conv_b,
           b_igate, b_fgate, mix_norm_g, w_out, ln1_g, ln1_b, w_up, ffn_conv_w, ffn_conv_b, w_down,
           ln2_g, ln2_b):
    B, S, _ = x.shape
    assert DEPTH == 1 and w_in.shape[0] == 1
    l = 0

    c_rows = jnp.concatenate([c, c_ctx[None, :], jnp.zeros((8 - B - 1, D_MODEL), F32)], axis=0)
    chunk = jnp.arange(6 * D_MODEL) // D_MODEL
    bias = b_ada[l] + jnp.where((chunk == 1) | (chunk == 4), 1.0, 0.0)
    mod, w_in_b, w_g = _modulation(c_rows, w_ada[l], bias[None, :], jnp.swapaxes(w_in[l], 0, 1))
    mod_x = mod[:B].reshape(B, 6, D_MODEL)
    mod_c = jnp.broadcast_to(mod[B].reshape(1, 6, D_MODEL), (B, 6, D_MODEL))

    gbi = jnp.broadcast_to(b_igate[l].reshape(N_HD, 1), (N_HD, CHUNK))
    gbf = jnp.broadcast_to(b_fgate[l].reshape(N_HD, 1), (N_HD, CHUNK))
    ln_g = gmlp_ln_g[l].reshape(1, HALF_W)
    ws = gmlp_ws[l].astype(BF16)
    bsb = jnp.repeat(gmlp_bs[l].T, DH, axis=1)
    norm_g = mix_norm_g[l][None, :]
    in_wts = (w_in_b, w_g, ln_g, ws, bsb, qk_conv_w[l], qk_conv_b[l][None, :], gbi, gbf,
              norm_g[:, :HALF_W])

    _, q_c, kt_c, v_c, _, *gates_c = _inproj(ctx, mod_c, in_wts, tm=ctx.shape[1], mixer=False)
    _, _, c_state, m_state = _scan(q_c, kt_c, v_c, *gates_c)

    yan, q, kt, v, o_pre, *rest = _inproj(x, mod_x, in_wts, tm=LATENT_TILE,
                                          ffn_f32=(w_up[l], w_down[l], w_out[l]))
    gates, (w_up_b, w_down_b, w_out_b) = rest[:4], rest[4:]
    hf, hb, _, _ = _scan(q, kt, v, *gates, init=(c_state, m_state))

    cw = _interleave_blocks(ffn_conv_w[l][:, :D_FF], 0.5 * ffn_conv_w[l][:, D_FF:], 1)
    cb = _interleave_blocks(ffn_conv_b[l][None, :D_FF], 0.5 * ffn_conv_b[l][None, D_FF:], 1)
    ffn_wts = (norm_g[:, HALF_W:], w_out_b, ln1_g[l][None, :], ln1_b[l][None, :],
               w_up_b, cw, cb, w_down_b, ln2_g[l][None, :], ln2_b[l][None, :])
    return _ffn(x, yan, hf, hb, o_pre, mod_x, ffn_wts, tm=LATENT_TILE, n_sub=FFN_SUB)
```
